```python
import jax, jax.numpy as jnp
from jax import lax
import numpy as np

D_MODEL = 1024
BATCH = 16
SEQ = 256
DEPTH = 4
DEC_BATCH = 2
DEC_SEQ = 1024
PAST_LEN = 256

GRID_W = 64
D_CONV = D_MODEL // 2
CONV_WIDTH = 31
HEAD_DIM = 64
N_HEADS_NA = D_MODEL // 128
D_NA = N_HEADS_NA * HEAD_DIM
WIN_H_MAX = 8
WIN_W = 16
QB_W = 16
KB_W = 2 * WIN_W
ROPE_BASE = 10000.0
Q_BLOCK = 128
D_GM = D_MODEL // 2
GM_CHUNK = 128
GM_GROUPS = 4
GM_CH = D_GM // GM_GROUPS
N_BRANCH = 3
D_IN = 2 * D_CONV + 3 * D_NA + 2 * D_GM + N_BRANCH * D_MODEL
N_EGROUPS = 4
EXP_PER_GROUP = 8
N_EXPERTS = N_EGROUPS * EXP_PER_GROUP
TOP_K = 2
D_EXPERT = D_MODEL // 8
ALPHA = (2 * DEPTH) ** 0.25
BETA = (8 * DEPTH) ** -0.25
LN_EPS = 1e-5
NEG_INF = -1e30

kernel_name = "hybrid_diffusion_conv_natten_gmlp_hmoe_step"


def layer_norm(x, g, b):
    xf = x.astype(jnp.float32)
    mu = jnp.mean(xf, axis=-1, keepdims=True)
    var = jnp.mean(jnp.square(xf - mu), axis=-1, keepdims=True)
    return ((xf - mu) * lax.rsqrt(var + LN_EPS)).astype(x.dtype) * g + b


def rope_axis(x, pos):
    n = x.shape[-1]
    inv = ROPE_BASE ** (-jnp.arange(0, n, 2, dtype=jnp.float32) / n)
    ang = pos[:, None] * inv[None, :]
    cos = jnp.cos(ang)[None, :, None, :]
    sin = jnp.sin(ang)[None, :, None, :]
    x1, x2 = x[..., : n // 2], x[..., n // 2:]
    return jnp.concatenate([x1 * cos - x2 * sin, x1 * sin + x2 * cos], axis=-1)


def axial_rope(x):
    L = x.shape[1]
    t = jnp.arange(L)
    pr = (t // GRID_W).astype(jnp.float32)
    pc = (t % GRID_W).astype(jnp.float32)
    half = HEAD_DIM // 2
    out = jnp.concatenate([rope_axis(x[..., :half], pr), rope_axis(x[..., half:], pc)], axis=-1)
    return out.astype(x.dtype)


def conformer_conv(a, b, conv_dw, conv_b, ln_g, ln_b, conv_pw):
    y = a * jax.nn.sigmoid(b)
    y = lax.conv_general_dilated(
        y, conv_dw[:, None, :], window_strides=(1,),
        padding=[(CONV_WIDTH // 2, CONV_WIDTH // 2)],
        dimension_numbers=("NWC", "WIO", "NWC"),
        feature_group_count=D_CONV) + conv_b
    y = jax.nn.silu(layer_norm(y, ln_g, ln_b))
    return y @ conv_pw


def spatial_gating(u, v, ln_g, ln_b, ws, bs, out):
    B, L, _ = u.shape
    n = L // GM_CHUNK
    u = jax.nn.gelu(u)
    v = layer_norm(jax.nn.gelu(v), ln_g, ln_b).reshape(B, n, GM_CHUNK, GM_GROUPS, GM_CH)
    sv = jnp.einsum('gpq,bnqgc->bnpgc', ws, v) + bs.T[:, :, None]
    return (u * sv.reshape(B, L, D_GM)) @ out


def context_attention(q, k, v):
    B, L, H, hd = q.shape
    nb = L // Q_BLOCK
    scale = hd ** -0.5
    qb = q.reshape(B, nb, Q_BLOCK, H, hd).transpose(1, 0, 2, 3, 4)

    def one_block(qi):
        s = jnp.einsum('bqhd,bkhd->bhqk', qi, k).astype(jnp.float32) * scale
        p = jax.nn.softmax(s, axis=-1).astype(v.dtype)
        return jnp.einsum('bhqk,bkhd->bqhd', p, v)

    o = lax.map(one_block, qb)
    return o.transpose(1, 0, 2, 3, 4).reshape(B, L, H * hd)


def neighbourhood_attention(q, k, v, ck, cv, rpb):
    B, L, H, hd = q.shape
    rows = L // GRID_W
    wh = min(WIN_H_MAX, rows)
    ncb = GRID_W // QB_W
    scale = hd ** -0.5
    kg = k.reshape(B, rows, GRID_W, H, hd)
    vg = v.reshape(B, rows, GRID_W, H, hd)
    qg = q.reshape(B, rows, ncb, QB_W, H, hd)
    r_all = jnp.arange(rows)
    row_start = jnp.clip(r_all - wh // 2, 0, rows - wh)
    row_idx = row_start[:, None] + jnp.arange(wh)
    j = jnp.arange(ncb)
    kb_start = jnp.clip(j * QB_W - WIN_W // 2, 0, GRID_W - KB_W)
    col_idx = kb_start[:, None] + jnp.arange(KB_W)
    qcol = j[:, None] * QB_W + jnp.arange(QB_W)
    col_start = jnp.clip(qcol - WIN_W // 2, 0, GRID_W - WIN_W)
    kc = col_idx[:, None, :]
    col_valid = (kc >= col_start[..., None]) & (kc < col_start[..., None] + WIN_W)
    col_mask = jnp.where(col_valid, 0.0, NEG_INF).astype(jnp.float32)
    dc_idx = jnp.clip(kc - qcol[..., None] + WIN_W - 1, 0, 2 * WIN_W - 2)
    n_loc = wh * KB_W

    def per_row(args):
        qr, ridx, r = args
        kr = kg[:, ridx][:, :, col_idx]
        vr = vg[:, ridx][:, :, col_idx]
        s_loc = jnp.einsum('bjqhd,bwjkhd->bhjqwk', qr, kr).astype(jnp.float32) * scale
        bias = rpb[:, ridx - r + WIN_H_MAX - 1][:, :, dc_idx]
        s_loc = s_loc + bias.transpose(0, 2, 3, 1, 4)[None] + col_mask[:, :, None, :]
        s_ctx = jnp.einsum('bjqhd,bchd->bhjqc', qr, ck).astype(jnp.float32) * scale
        s = jnp.concatenate([s_loc.reshape(B, H, ncb, QB_W, n_loc), s_ctx], axis=-1)
        p = jax.nn.softmax(s, axis=-1).astype(v.dtype)
        p_loc = p[..., :n_loc].reshape(B, H, ncb, QB_W, wh, KB_W)
        p_ctx = p[..., n_loc:]
        return (jnp.einsum('bhjqwk,bwjkhd->bjqhd', p_loc, vr)
                + jnp.einsum('bhjqc,bchd->bjqhd', p_ctx, cv))

    o = lax.map(per_row, (qg.transpose(1, 0, 2, 3, 4, 5), row_idx, r_all))
    return o.transpose(1, 0, 2, 3, 4, 5).reshape(B, L, H * hd)


def token_mixer(h, w_in, conv_dw, conv_b, conv_ln_g, conv_ln_b, conv_pw, na_rpb, na_out,
                gm_ln_g, gm_ln_b, gm_ws, gm_bs, gm_out, w_o, ctx_k, ctx_v):
    B, L, _ = h.shape
    z = h @ w_in
    cuts = [D_CONV, 2 * D_CONV, 2 * D_CONV + D_NA, 2 * D_CONV + 2 * D_NA,
            2 * D_CONV + 3 * D_NA, 2 * D_CONV + 3 * D_NA + D_GM,
            2 * D_CONV + 3 * D_NA + 2 * D_GM]
    ca, cb, q, k, v, gu, gv, gz = jnp.split(z, cuts, axis=-1)
    q = q.reshape(B, L, N_HEADS_NA, HEAD_DIM)
    k = k.reshape(B, L, N_HEADS_NA, HEAD_DIM)
    v = v.reshape(B, L, N_HEADS_NA, HEAD_DIM)
    br_conv = conformer_conv(ca, cb, conv_dw, conv_b, conv_ln_g, conv_ln_b, conv_pw)
    if ctx_k is None:
        att = context_attention(q, k, v)
    else:
        att = neighbourhood_attention(axial_rope(q), axial_rope(k), v, ctx_k, ctx_v, na_rpb)
    br_att = att @ na_out
    br_gm = spatial_gating(gu, gv, gm_ln_g, gm_ln_b, gm_ws, gm_bs, gm_out)
    g = jax.nn.sigmoid(gz).reshape(B, L, N_BRANCH, D_MODEL)
    merged = g[:, :, 0] * br_conv + g[:, :, 1] * br_att + g[:, :, 2] * br_gm
    return merged @ w_o, k, v


def hier_moe(h, rg_w, rg_b, re_w, re_b, w1, w3, w2):
    B, L, D = h.shape
    t = h.reshape(B * L, D)
    g_logits = (t @ rg_w + rg_b).astype(jnp.float32)
    g_prob = jax.nn.softmax(g_logits, axis=-1)
    g_idx = jnp.argmax(g_logits, axis=-1)
    g_p = jnp.take_along_axis(g_prob, g_idx[:, None], axis=-1)
    e_logits = (jnp.einsum('td,gde->tge', t, re_w) + re_b).astype(jnp.float32)
    e_sel = jnp.take_along_axis(e_logits, g_idx[:, None, None], axis=1)[:, 0]
    top_v, top_i = lax.top_k(e_sel, TOP_K)
    top_w = jax.nn.softmax(top_v, axis=-1) * g_p
    expert_id = g_idx[:, None] * EXP_PER_GROUP + top_i
    gate = jnp.sum(jax.nn.one_hot(expert_id, N_EXPERTS, dtype=jnp.float32) * top_w[..., None], axis=1)
    hid = jax.nn.silu(jnp.einsum('td,edf->tef', t, w1)) * jnp.einsum('td,edf->tef', t, w3)
    hid = hid * gate[..., None].astype(hid.dtype)
    return jnp.einsum('tef,efd->td', hid, w2).reshape(B, L, D)


def trunk_layer(x, cvec, w_ada, b_ada, mix_params, ln1_g, ln1_b, moe_params, ln2_g, ln2_b,
                ctx_k, ctx_v):
    m = (jax.nn.silu(cvec) @ w_ada + b_ada)[:, None, :]
    sh1, sc1, g1, sh2, sc2, g2 = jnp.split(m, 6, axis=-1)
    mix, k, v = token_mixer(x * (1 + sc1) + sh1, *mix_params, ctx_k, ctx_v)
    x = layer_norm(ALPHA * x + g1 * mix, ln1_g, ln1_b)
    y = hier_moe(x * (1 + sc2) + sh2, *moe_params)
    x = layer_norm(ALPHA * x + g2 * y, ln2_g, ln2_b)
    return x, k, v


def setup_inputs(seed: int = 0) -> dict:
    key = jax.random.key(seed)
    keys = jax.random.split(key, 48)
    counter = [0]

    def nrm(shape, s):
        kk = keys[counter[0]]
        counter[0] += 1
        return jax.random.normal(kk, shape, jnp.float32) * s

    L, D = DEPTH, D_MODEL
    return {
        "x_prompt": nrm((BATCH, SEQ, D), 1.0),
        "x_sample": nrm((DEC_BATCH, DEC_SEQ, D), 1.0),
        "cache_na_k": nrm((DEC_BATCH, DEPTH, PAST_LEN, N_HEADS_NA, HEAD_DIM), 1.0),
        "cache_na_v": nrm((DEC_BATCH, DEPTH, PAST_LEN, N_HEADS_NA, HEAD_DIM), 1.0),
        "c": nrm((DEC_BATCH, D), 1.0),
        "c_ctx": nrm((D,), 1.0),
        "w_ada": nrm((L, D, 6 * D), 0.2 * D ** -0.5),
        "b_ada": nrm((L, 6 * D), 0.01),
        "w_in": nrm((L, D, D_IN), D ** -0.5),
        "conv_dw": nrm((L, CONV_WIDTH, D_CONV), CONV_WIDTH ** -0.5),
        "conv_b": nrm((L, D_CONV), 0.01),
        "conv_ln_g": 1.0 + nrm((L, D_CONV), 0.02),
        "conv_ln_b": nrm((L, D_CONV), 0.01),
        "conv_pw": nrm((L, D_CONV, D), D_CONV ** -0.5),
        "na_rpb": nrm((L, N_HEADS_NA, 2 * WIN_H_MAX - 1, 2 * WIN_W - 1), 0.1),
        "na_out": nrm((L, D_NA, D), D_NA ** -0.5),
        "gm_ln_g": 1.0 + nrm((L, D_GM), 0.02),
        "gm_ln_b": nrm((L, D_GM), 0.01),
        "gm_ws": nrm((L, GM_GROUPS, GM_CHUNK, GM_CHUNK), GM_CHUNK ** -0.5),
        "gm_bs": 1.0 + nrm((L, GM_GROUPS, GM_CHUNK), 0.01),
        "gm_out": nrm((L, D_GM, D), D_GM ** -0.5),
        "w_o": nrm((L, D, D), BETA * D ** -0.5),
        "ln1_g": 1.0 + nrm((L, D), 0.02),
        "ln1_b": nrm((L, D), 0.01),
        "rg_w": nrm((L, D, N_EGROUPS), D ** -0.5),
        "rg_b": nrm((L, N_EGROUPS), 0.01),
        "re_w": nrm((L, N_EGROUPS, D, EXP_PER_GROUP), D ** -0.5),
        "re_b": nrm((L, N_EGROUPS, EXP_PER_GROUP), 0.01),
        "moe_w1": nrm((L, N_EXPERTS, D, D_EXPERT), D ** -0.5),
        "moe_w3": nrm((L, N_EXPERTS, D, D_EXPERT), D ** -0.5),
        "moe_w2": nrm((L, N_EXPERTS, D_EXPERT, D), BETA * D_EXPERT ** -0.5),
        "ln2_g": 1.0 + nrm((L, D), 0.02),
        "ln2_b": nrm((L, D), 0.01),
    }


def reference(x_prompt, x_sample, cache_na_k, cache_na_v, c, c_ctx, w_ada, b_ada, w_in,
              conv_dw, conv_b, conv_ln_g, conv_ln_b, conv_pw, na_rpb, na_out,
              gm_ln_g, gm_ln_b, gm_ws, gm_bs, gm_out, w_o, ln1_g, ln1_b,
              rg_w, rg_b, re_w, re_b, moe_w1, moe_w3, moe_w2, ln2_g, ln2_b):
    xp, xs = x_prompt, x_sample
    new_k, new_v = [], []
    for i in range(DEPTH):
        mix_params = (w_in[i], conv_dw[i], conv_b[i], conv_ln_g[i], conv_ln_b[i], conv_pw[i],
                      na_rpb[i], na_out[i], gm_ln_g[i], gm_ln_b[i], gm_ws[i], gm_bs[i],
                      gm_out[i], w_o[i])
        moe_params = (rg_w[i], rg_b[i], re_w[i], re_b[i], moe_w1[i], moe_w3[i], moe_w2[i])
        xp, kp, vp = trunk_layer(xp, c_ctx[None], w_ada[i], b_ada[i], mix_params,
                                 ln1_g[i], ln1_b[i], moe_params, ln2_g[i], ln2_b[i], None, None)
        new_k.append(kp)
        new_v.append(vp)
        xs, _, _ = trunk_layer(xs, c, w_ada[i], b_ada[i], mix_params,
                               ln1_g[i], ln1_b[i], moe_params, ln2_g[i], ln2_b[i],
                               cache_na_k[:, i], cache_na_v[:, i])
    new_na_k = jnp.stack(new_k, axis=1)
    new_na_v = jnp.stack(new_v, axis=1)
    return (xp, xs, new_na_k, new_na_v)
```

```python
import functools

import jax
import jax.numpy as jnp
import numpy as np
from jax import lax
from jax.experimental import pallas as pl
from jax.experimental.pallas import tpu as pltpu

F32 = jnp.float32
BF16 = jnp.bfloat16
HIGHEST = lax.Precision.HIGHEST

D_MODEL = 1024
BATCH = 16
SEQ = 256
DEPTH = 4
DEC_BATCH = 2
DEC_SEQ = 1024
PAST_LEN = 256
GRID_W = 64
GRID_H = DEC_SEQ // GRID_W
D_CONV = 512
CONV_WIDTH = 31
CONV_HALF = CONV_WIDTH // 2
HEAD_DIM = 64
N_HEADS = 8
D_NA = 512
WIN_H = 8
WIN_W = 16
ROPE_BASE = 10000.0
D_GM = 512
GM_CHUNK = 128
GM_GROUPS = 4
D_IN = 6656
N_EGROUPS = 4
EXP_PER_GROUP = 8
N_EXPERTS = 32
D_EXPERT = 128
ALPHA = (2 * DEPTH) ** 0.25
LN_EPS = 1e-5
NEG_INF = -1e30

T_PROMPT = BATCH * SEQ
T_SAMPLE = DEC_BATCH * DEC_SEQ
T_ALL = T_PROMPT + T_SAMPLE
N_COND = 8

CB_A, CB_B, CB_Q, CB_K, CB_V, CB_GU, CB_GV, CB_GZ = 0, 1, 2, 3, 4, 5, 6, 7
COL_BLK = 512

TB = 256
N_TB = T_ALL // TB
N_TB_PROMPT = T_PROMPT // TB
TB_PER_SAMPLE = DEC_SEQ // TB
HALO = 16
CONV_ROWS = 32

TM_IN = 512
TM_MOE = 1024
ROUTER_LANES = 128
VMEM_LIMIT = 56 * 1024 * 1024


def _ln(x, g, b):
    mu = jnp.mean(x, axis=-1, keepdims=True)
    xc = x - mu
    var = jnp.mean(xc * xc, axis=-1, keepdims=True)
    return xc * lax.rsqrt(var + LN_EPS) * g + b


def _sigmoid(x):
    return jax.nn.sigmoid(x)


def _gelu(x):
    return jax.nn.gelu(x, approximate=True)


def _cparams(sem):
    return pltpu.CompilerParams(dimension_semantics=sem, vmem_limit_bytes=VMEM_LIMIT)


def _ada_kernel(c_ref, w_ref, b_ref, o_ref):
    c = c_ref[...]
    s = c * _sigmoid(c)
    o_ref[...] = jnp.dot(s, w_ref[...], preferred_element_type=F32, precision=HIGHEST) + b_ref[...]


def _ada(cond, w_ada, b_ada):
    tn = 512
    return pl.pallas_call(
        _ada_kernel,
        grid=(DEPTH, 6 * D_MODEL // tn),
        in_specs=[
            pl.BlockSpec((N_COND, D_MODEL), lambda l, j: (0, 0)),
            pl.BlockSpec((None, D_MODEL, tn), lambda l, j: (l, 0, j)),
            pl.BlockSpec((None, 1, tn), lambda l, j: (l, 0, j)),
        ],
        out_specs=pl.BlockSpec((None, N_COND, tn), lambda l, j: (l, 0, j)),
        out_shape=jax.ShapeDtypeStruct((DEPTH, N_COND, 6 * D_MODEL), F32),
        compiler_params=_cparams(("arbitrary", "arbitrary")),
    )(cond, w_ada, b_ada.reshape(DEPTH, 1, 6 * D_MODEL))


def _mod_row(i, blocks_per_sample, n_prompt_blocks):
    return jnp.where(i < n_prompt_blocks, 0, 1 + (i - n_prompt_blocks) // blocks_per_sample)


def _inproj_kernel(x_ref, mod_ref, w_ref, z_ref, h_ref):
    @pl.when(pl.program_id(1) == 0)
    def _():
        h_ref[...] = (x_ref[...] * (1.0 + mod_ref[1]) + mod_ref[0]).astype(BF16)

    z_ref[...] = jnp.dot(h_ref[...], w_ref[...], preferred_element_type=F32)


def _inproj(l, x, mods, w_in):
    n_m = T_ALL // TM_IN
    bps = DEC_SEQ // TM_IN
    npb = T_PROMPT // TM_IN
    return pl.pallas_call(
        _inproj_kernel,
        grid=(n_m, D_IN // COL_BLK),
        in_specs=[
            pl.BlockSpec((TM_IN, D_MODEL), lambda i, j: (i, 0)),
            pl.BlockSpec((None, 6, None, 1, D_MODEL),
                         lambda i, j: (l, 0, _mod_row(i, bps, npb), 0, 0)),
            pl.BlockSpec((None, D_MODEL, COL_BLK), lambda i, j: (l, 0, j)),
        ],
        out_specs=pl.BlockSpec((TM_IN, COL_BLK), lambda i, j: (i, j)),
        out_shape=jax.ShapeDtypeStruct((T_ALL, D_IN), F32),
        scratch_shapes=[pltpu.VMEM((TM_IN, D_MODEL), BF16)],
        compiler_params=_cparams(("arbitrary", "arbitrary")),
    )(x, mods, w_in)


def _branch_kernel(ap_ref, ac_ref, an_ref, bp_ref, bc_ref, bn_ref, gu_ref, gv_ref,
                   dw_ref, cb_ref, clg_ref, clb_ref, glg_ref, glb_ref, ws_ref, bst_ref,
                   yc_ref, ug_ref, ypad_ref):
    i = pl.program_id(0)
    j = i - N_TB_PROMPT
    in_sample = i >= N_TB_PROMPT
    has_prev = jnp.logical_and(in_sample, j % TB_PER_SAMPLE != 0)
    has_next = jnp.logical_and(in_sample, j % TB_PER_SAMPLE != TB_PER_SAMPLE - 1)

    yp = ap_ref[...] * _sigmoid(bp_ref[...])
    yn = an_ref[...] * _sigmoid(bn_ref[...])
    ypad_ref[0:HALO, :] = jnp.where(has_prev, yp, 0.0)
    ypad_ref[HALO:HALO + TB, :] = ac_ref[...] * _sigmoid(bc_ref[...])
    ypad_ref[HALO + TB:HALO + TB + HALO, :] = jnp.where(has_next, yn, 0.0)

    off = HALO - CONV_HALF
    for c in range(TB // CONV_ROWS):
        base = c * CONV_ROWS
        acc = jnp.zeros((CONV_ROWS, D_CONV), F32)
        for k in range(CONV_WIDTH):
            acc = acc + ypad_ref[base + off + k:base + off + k + CONV_ROWS, :] * dw_ref[k:k + 1, :]
        y = _ln(acc + cb_ref[...], clg_ref[...], clb_ref[...])
        yc_ref[base:base + CONV_ROWS, :] = (y * _sigmoid(y)).astype(BF16)

    for n in range(TB // GM_CHUNK):
        rows = slice(n * GM_CHUNK, (n + 1) * GM_CHUNK)
        u = _gelu(gu_ref[rows, :])
        v = _ln(_gelu(gv_ref[rows, :]), glg_ref[...], glb_ref[...]).astype(BF16)
        for g in range(GM_GROUPS):
            cols = slice(g * GM_CHUNK, (g + 1) * GM_CHUNK)
            sv = jnp.dot(ws_ref[g], v[:, cols], preferred_element_type=F32) + bst_ref[:, g:g + 1]
            ug_ref[rows, cols] = (u[:, cols] * sv).astype(BF16)


def _branches(l, z, conv_dw, conv_b, conv_ln_g, conv_ln_b, gm_ln_g, gm_ln_b, gm_ws, gm_bs_t):
    halo_per_tb = TB // HALO
    n_halo = T_ALL // HALO

    def cur(cb):
        return pl.BlockSpec((TB, COL_BLK), lambda i: (i, cb))

    def prev(cb):
        return pl.BlockSpec((HALO, COL_BLK), lambda i: (jnp.maximum(i * halo_per_tb - 1, 0), cb))

    def nxt(cb):
        return pl.BlockSpec((HALO, COL_BLK),
                            lambda i: (jnp.minimum((i + 1) * halo_per_tb, n_halo - 1), cb))

    def vec(n):
        return pl.BlockSpec((None, 1, n), lambda i: (l, 0, 0))

    return pl.pallas_call(
        _branch_kernel,
        grid=(N_TB,),
        in_specs=[
            prev(CB_A), cur(CB_A), nxt(CB_A), prev(CB_B), cur(CB_B), nxt(CB_B),
            cur(CB_GU), cur(CB_GV),
            pl.BlockSpec((None, CONV_WIDTH, D_CONV), lambda i: (l, 0, 0)),
            vec(D_CONV), vec(D_CONV), vec(D_CONV), vec(D_GM), vec(D_GM),
            pl.BlockSpec((None, GM_GROUPS, GM_CHUNK, GM_CHUNK), lambda i: (l, 0, 0, 0)),
            pl.BlockSpec((None, GM_CHUNK, GM_GROUPS), lambda i: (l, 0, 0)),
        ],
        out_specs=[pl.BlockSpec((TB, D_CONV), lambda i: (i, 0)),
                   pl.BlockSpec((TB, D_GM), lambda i: (i, 0))],
        out_shape=[jax.ShapeDtypeStruct((T_ALL, D_CONV), BF16),
                   jax.ShapeDtypeStruct((T_ALL, D_GM), BF16)],
        scratch_shapes=[pltpu.VMEM((TB + 2 * HALO, D_CONV), F32)],
        compiler_params=_cparams(("arbitrary",)),
    )(z, z, z, z, z, z, z, z, conv_dw, conv_b, conv_ln_g, conv_ln_b, gm_ln_g, gm_ln_b,
      gm_ws, gm_bs_t)


def _ctx_attn_kernel(q_ref, k_ref, v_ref, o_ref, ko_ref, vo_ref):
    k = k_ref[...]
    v = v_ref[...]
    ko_ref[...] = k
    vo_ref[...] = v
    q = (q_ref[...] * HEAD_DIM ** -0.5).astype(BF16)
    kb = k.astype(BF16)
    vb = v.astype(BF16)
    for h in range(N_HEADS):
        cols = slice(h * HEAD_DIM, (h + 1) * HEAD_DIM)
        s = lax.dot_general(q[:, cols], kb[:, cols], (((1,), (1,)), ((), ())),
                            preferred_element_type=F32)
        m = jnp.max(s, axis=-1, keepdims=True)
        p = jnp.exp(s - m)
        den = jnp.sum(p, axis=-1, keepdims=True)
        o = jnp.dot(p.astype(BF16), vb[:, cols], preferred_element_type=F32)
        o_ref[:, cols] = (o / den).astype(BF16)


def _ctx_attn(z):
    def col(cb):
        return pl.BlockSpec((SEQ, COL_BLK), lambda b: (b, cb))

    blk = pl.BlockSpec((SEQ, D_NA), lambda b: (b, 0))
    return pl.pallas_call(
        _ctx_attn_kernel,
        grid=(BATCH,),
        in_specs=[col(CB_Q), col(CB_K), col(CB_V)],
        out_specs=[blk, blk, blk],
        out_shape=[jax.ShapeDtypeStruct((T_PROMPT, D_NA), BF16),
                   jax.ShapeDtypeStruct((T_PROMPT, D_NA), F32),
                   jax.ShapeDtypeStruct((T_PROMPT, D_NA), F32)],
        compiler_params=_cparams(("arbitrary",)),
    )(z, z, z)


def _rope(x, cos, sin_up, sin_dn):
    return (x * cos + pltpu.roll(x, D_NA - HEAD_DIM // 4, 1) * sin_up
            + pltpu.roll(x, HEAD_DIM // 4, 1) * sin_dn)


def _na_attn_kernel(q_ref, k_ref, v_ref, ck_ref, cv_ref, tz_ref, cos_ref, sup_ref, sdn_ref,
                    o_ref, krot_ref):
    r = pl.program_id(1)

    @pl.when(r == 0)
    def _():
        krot_ref[...] = _rope(k_ref[...], cos_ref[...], sup_ref[...], sdn_ref[...]).astype(BF16)

    qrows = pl.ds(pl.multiple_of(r * GRID_W, GRID_W), GRID_W)
    q = _rope(q_ref[...], cos_ref[qrows, :], sup_ref[qrows, :], sdn_ref[qrows, :])
    q = (q * HEAD_DIM ** -0.5).astype(BF16)

    row_start = jnp.clip(r - WIN_H // 2, 0, GRID_H - WIN_H)
    krows = pl.ds(pl.multiple_of(row_start * GRID_W, GRID_W), WIN_H * GRID_W)
    kwin = krot_ref[krows, :]
    vwin = v_ref[krows, :].astype(BF16)
    ck = ck_ref[...].astype(BF16)
    cv = cv_ref[...].astype(BF16)
    dr0 = row_start - r + WIN_H - 1
    nt = (((1,), (1,)), ((), ()))
    for h in range(N_HEADS):
        cols = slice(h * HEAD_DIM, (h + 1) * HEAD_DIM)
        qh = q[:, cols]
        bias = jnp.concatenate([tz_ref[h, dr0 + w] for w in range(WIN_H)], axis=1)
        s_loc = lax.dot_general(qh, kwin[:, cols], nt, preferred_element_type=F32) + bias
        s_ctx = lax.dot_general(qh, ck[:, cols], nt, preferred_element_type=F32)
        m = jnp.maximum(jnp.max(s_loc, axis=-1, keepdims=True),
                        jnp.max(s_ctx, axis=-1, keepdims=True))
        p_loc = jnp.exp(s_loc - m)
        p_ctx = jnp.exp(s_ctx - m)
        den = jnp.sum(p_loc, axis=-1, keepdims=True) + jnp.sum(p_ctx, axis=-1, keepdims=True)
        o = (jnp.dot(p_loc.astype(BF16), vwin[:, cols], preferred_element_type=F32)
             + jnp.dot(p_ctx.astype(BF16), cv[:, cols], preferred_element_type=F32))
        o_ref[:, cols] = (o / den).astype(BF16)


def _na_attn(l, z, cache_k, cache_v, tz, cos, sin_up, sin_dn):
    seq_blk0 = T_PROMPT // DEC_SEQ
    row_blk0 = T_PROMPT // GRID_W
    full = pl.BlockSpec((DEC_SEQ, D_NA), lambda b, r: (0, 0))
    return pl.pallas_call(
        _na_attn_kernel,
        grid=(DEC_BATCH, GRID_H),
        in_specs=[
            pl.BlockSpec((GRID_W, COL_BLK), lambda b, r: (row_blk0 + b * GRID_H + r, CB_Q)),
            pl.BlockSpec((DEC_SEQ, COL_BLK), lambda b, r: (seq_blk0 + b, CB_K)),
            pl.BlockSpec((DEC_SEQ, COL_BLK), lambda b, r: (seq_blk0 + b, CB_V)),
            pl.BlockSpec((None, None, PAST_LEN, D_NA), lambda b, r: (b, l, 0, 0)),
            pl.BlockSpec((None, None, PAST_LEN, D_NA), lambda b, r: (b, l, 0, 0)),
            pl.BlockSpec((None, N_HEADS, 2 * WIN_H - 1, GRID_W, GRID_W),
                         lambda b, r: (l, 0, 0, 0, 0)),
            full, full, full,
        ],
        out_specs=pl.BlockSpec((GRID_W, D_NA), lambda b, r: (b * GRID_H + r, 0)),
        out_shape=jax.ShapeDtypeStruct((T_SAMPLE, D_NA), BF16),
        scratch_shapes=[pltpu.VMEM((DEC_SEQ, D_NA), BF16)],
        compiler_params=_cparams(("arbitrary", "arbitrary")),
    )(z, z, z, cache_k, cache_v, tz, cos, sin_up, sin_dn)


def _merge_kernel(x_ref, mod_ref, g0a, g0b, g1a, g1b, g2a, g2b, yc_ref, ug_ref, ap_ref, as_ref,
                  pw_ref, no_ref, go_ref, wo_ref, lg_ref, lb_ref, o_ref):
    i = pl.program_id(0)
    att = jnp.where(i < N_TB_PROMPT, ap_ref[...], as_ref[...])
    br_c = jnp.dot(yc_ref[...], pw_ref[...], preferred_element_type=F32)
    br_a = jnp.dot(att, no_ref[...], preferred_element_type=F32)
    br_g = jnp.dot(ug_ref[...], go_ref[...], preferred_element_type=F32)
    h = COL_BLK
    for lo, ga, gb, gc in ((0, g0a, g1a, g2a), (h, g0b, g1b, g2b)):
        cols = slice(lo, lo + h)
        o_ref[:, cols] = (_sigmoid(ga[...]) * br_c[:, cols] + _sigmoid(gb[...]) * br_a[:, cols]
                          + _sigmoid(gc[...]) * br_g[:, cols])
    mix = jnp.dot(o_ref[...].astype(BF16), wo_ref[...], preferred_element_type=F32)
    o_ref[...] = _ln(ALPHA * x_ref[...] + mod_ref[2] * mix, lg_ref[...], lb_ref[...])


def _merge(l, x, mods, z, yc, ug, att_p, att_s, conv_pw, na_out, gm_out, w_o, ln_g, ln_b):
    def gz(k):
        return pl.BlockSpec((TB, COL_BLK), lambda i: (i, CB_GZ + k))

    def w(k, n):
        return pl.BlockSpec((None, k, n), lambda i: (l, 0, 0))

    blk512 = pl.BlockSpec((TB, COL_BLK), lambda i: (i, 0))
    return pl.pallas_call(
        _merge_kernel,
        grid=(N_TB,),
        in_specs=[
            pl.BlockSpec((TB, D_MODEL), lambda i: (i, 0)),
            pl.BlockSpec((None, 6, None, 1, D_MODEL),
                         lambda i: (l, 0, _mod_row(i, TB_PER_SAMPLE, N_TB_PROMPT), 0, 0)),
            gz(0), gz(1), gz(2), gz(3), gz(4), gz(5),
            blk512, blk512,
            pl.BlockSpec((TB, D_NA), lambda i: (jnp.minimum(i, N_TB_PROMPT - 1), 0)),
            pl.BlockSpec((TB, D_NA), lambda i: (jnp.maximum(i - N_TB_PROMPT, 0), 0)),
            w(D_CONV, D_MODEL), w(D_NA, D_MODEL), w(D_GM, D_MODEL), w(D_MODEL, D_MODEL),
            w(1, D_MODEL), w(1, D_MODEL),
        ],
        out_specs=pl.BlockSpec((TB, D_MODEL), lambda i: (i, 0)),
        out_shape=jax.ShapeDtypeStruct((T_ALL, D_MODEL), F32),
        compiler_params=_cparams(("arbitrary",)),
    )(x, mods, z, z, z, z, z, z, yc, ug, att_p, att_s, conv_pw, na_out, gm_out, w_o, ln_g, ln_b)


def _route(logits):
    lane = lax.broadcasted_iota(jnp.int32, logits.shape, 1)
    big = jnp.int32(ROUTER_LANES)
    is_g = lane < N_EGROUPS
    gl = jnp.where(is_g, logits, -jnp.inf)
    gmax = jnp.max(gl, axis=-1, keepdims=True)
    gidx = jnp.min(jnp.where(gl == gmax, lane, big), axis=-1, keepdims=True)
    gp = 1.0 / jnp.sum(jnp.where(is_g, jnp.exp(gl - gmax), 0.0), axis=-1, keepdims=True)
    lo = N_EGROUPS + gidx * EXP_PER_GROUP
    el = jnp.where(jnp.logical_and(lane >= lo, lane < lo + EXP_PER_GROUP), logits, -jnp.inf)
    v1 = jnp.max(el, axis=-1, keepdims=True)
    i1 = jnp.min(jnp.where(el == v1, lane, big), axis=-1, keepdims=True)
    el2 = jnp.where(lane == i1, -jnp.inf, el)
    v2 = jnp.max(el2, axis=-1, keepdims=True)
    i2 = jnp.min(jnp.where(el2 == v2, lane, big), axis=-1, keepdims=True)
    e2 = jnp.exp(v2 - v1)
    w1 = gp / (1.0 + e2)
    w2 = gp * e2 / (1.0 + e2)
    return jnp.where(lane == i1, w1, 0.0) + jnp.where(lane == i2, w2, 0.0)


def _moe_kernel(x_ref, mod_ref, rw_ref, rb_ref, w1_ref, w3_ref, w2_ref, lg_ref, lb_ref,
                o_ref, t_ref, gate_ref, acc_ref):
    e = pl.program_id(1)

    @pl.when(e == 0)
    def _():
        t = x_ref[...] * (1.0 + mod_ref[4]) + mod_ref[3]
        t_ref[...] = t.astype(BF16)
        logits = jnp.dot(t, rw_ref[...], preferred_element_type=F32, precision=HIGHEST)
        gate_ref[...] = _route(logits + rb_ref[...])
        acc_ref[...] = jnp.zeros_like(acc_ref)

    t = t_ref[...]
    h1 = jnp.dot(t, w1_ref[...], preferred_element_type=F32)
    h3 = jnp.dot(t, w3_ref[...], preferred_element_type=F32)
    gate = gate_ref[...]
    lane = lax.broadcasted_iota(jnp.int32, gate.shape, 1)
    gcol = jnp.sum(jnp.where(lane == e + N_EGROUPS, gate, 0.0), axis=-1, keepdims=True)
    hid = (h1 * _sigmoid(h1) * h3 * gcol).astype(BF16)
    acc_ref[...] += jnp.dot(hid, w2_ref[...], preferred_element_type=F32)

    @pl.when(e == N_EXPERTS - 1)
    def _():
        o_ref[...] = _ln(ALPHA * x_ref[...] + mod_ref[5] * acc_ref[...], lg_ref[...], lb_ref[...])


def _moe(l, x, mods, router_w, router_b, w1, w3, w2, ln_g, ln_b):
    n_m = T_ALL // TM_MOE
    bps = DEC_SEQ // TM_MOE
    npb = T_PROMPT // TM_MOE
    return pl.pallas_call(
        _moe_kernel,
        grid=(n_m, N_EXPERTS),
        in_specs=[
            pl.BlockSpec((TM_MOE, D_MODEL), lambda i, e: (i, 0)),
            pl.BlockSpec((None, 6, None, 1, D_MODEL),
                         lambda i, e: (l, 0, _mod_row(i, bps, npb), 0, 0)),
            pl.BlockSpec((None, D_MODEL, ROUTER_LANES), lambda i, e: (l, 0, 0)),
            pl.BlockSpec((None, 1, ROUTER_LANES), lambda i, e: (l, 0, 0)),
            pl.BlockSpec((None, None, D_MODEL, D_EXPERT), lambda i, e: (l, e, 0, 0)),
            pl.BlockSpec((None, None, D_MODEL, D_EXPERT), lambda i, e: (l, e, 0, 0)),
            pl.BlockSpec((None, None, D_EXPERT, D_MODEL), lambda i, e: (l, e, 0, 0)),
            pl.BlockSpec((None, 1, D_MODEL), lambda i, e: (l, 0, 0)),
            pl.BlockSpec((None, 1, D_MODEL), lambda i, e: (l, 0, 0)),
        ],
        out_specs=pl.BlockSpec((TM_MOE, D_MODEL), lambda i, e: (i, 0)),
        out_shape=jax.ShapeDtypeStruct((T_ALL, D_MODEL), F32),
        scratch_shapes=[pltpu.VMEM((TM_MOE, D_MODEL), BF16),
                        pltpu.VMEM((TM_MOE, ROUTER_LANES), F32),
                        pltpu.VMEM((TM_MOE, D_MODEL), F32)],
        compiler_params=_cparams(("arbitrary", "arbitrary")),
    )(x, mods, router_w, router_b, w1, w3, w2, ln_g, ln_b)


def _rope_tables():
    t = np.arange(DEC_SEQ)
    pos = np.stack([t // GRID_W, t % GRID_W], axis=1).astype(np.float32)
    quarter = HEAD_DIM // 4
    d = np.arange(HEAD_DIM)
    axis = d // (HEAD_DIM // 2)
    freq = d % quarter
    upper = (d % (HEAD_DIM // 2)) >= quarter
    inv = jnp.asarray(ROPE_BASE, F32) ** (-jnp.arange(0, HEAD_DIM // 2, 2, dtype=F32) / (HEAD_DIM // 2))
    ang = jnp.asarray(pos)[:, axis] * inv[freq][None, :]
    cos = jnp.cos(ang)
    sin = jnp.sin(ang)
    sin_up = jnp.where(upper[None, :], 0.0, -sin)
    sin_dn = jnp.where(upper[None, :], sin, 0.0)
    tile = lambda a: jnp.tile(a, (1, N_HEADS))
    return tile(cos), tile(sin_up), tile(sin_dn)


def _bias_tables(na_rpb):
    qc = np.arange(GRID_W)[:, None]
    kc = np.arange(GRID_W)[None, :]
    start = np.clip(qc - WIN_W // 2, 0, GRID_W - WIN_W)
    valid = (kc >= start) & (kc < start + WIN_W)
    dc = np.clip(kc - qc + WIN_W - 1, 0, 2 * WIN_W - 2)
    return jnp.where(jnp.asarray(valid), na_rpb[:, :, :, dc], NEG_INF)


def kernel(x_prompt, x_sample, cache_na_k, cache_na_v, c, c_ctx, w_ada, b_ada, w_in, conv_dw,
           conv_b, conv_ln_g, conv_ln_b, conv_pw, na_rpb, na_out, gm_ln_g, gm_ln_b, gm_ws, gm_bs,
           gm_out, w_o, ln1_g, ln1_b, rg_w, rg_b, re_w, re_b, moe_w1, moe_w3, moe_w2, ln2_g, ln2_b):
    x = jnp.concatenate([x_prompt.reshape(T_PROMPT, D_MODEL),
                         x_sample.reshape(T_SAMPLE, D_MODEL)], axis=0)

    cond = jnp.zeros((N_COND, D_MODEL), F32).at[0].set(c_ctx).at[1:1 + DEC_BATCH].set(c)
    mods = _ada(cond, w_ada, b_ada)
    mods = mods.reshape(DEPTH, N_COND, 6, 1, D_MODEL).transpose(0, 2, 1, 3, 4)

    bf = lambda a: a.astype(BF16)
    w_in_b, conv_pw_b, na_out_b, gm_out_b, w_o_b = bf(w_in), bf(conv_pw), bf(na_out), bf(gm_out), bf(w_o)
    gm_ws_b, w1_b, w3_b, w2_b = bf(gm_ws), bf(moe_w1), bf(moe_w3), bf(moe_w2)
    vec = lambda a: a.reshape(DEPTH, 1, a.shape[-1])
    gm_bs_t = gm_bs.transpose(0, 2, 1)
    router_w = jnp.concatenate(
        [rg_w, re_w.transpose(0, 2, 1, 3).reshape(DEPTH, D_MODEL, N_EXPERTS)], axis=-1)
    router_w = jnp.pad(router_w, ((0, 0), (0, 0), (0, ROUTER_LANES - N_EGROUPS - N_EXPERTS)))
    router_b = jnp.concatenate([rg_b, re_b.reshape(DEPTH, N_EXPERTS)], axis=-1)
    router_b = jnp.pad(router_b, ((0, 0), (0, ROUTER_LANES - N_EGROUPS - N_EXPERTS)))
    router_b = router_b.reshape(DEPTH, 1, ROUTER_LANES)
    cache_k = cache_na_k.reshape(DEC_BATCH, DEPTH, PAST_LEN, D_NA)
    cache_v = cache_na_v.reshape(DEC_BATCH, DEPTH, PAST_LEN, D_NA)
    tz = _bias_tables(na_rpb)
    cos, sin_up, sin_dn = _rope_tables()

    new_k, new_v = [], []
    for l in range(DEPTH):
        z = _inproj(l, x, mods, w_in_b)
        yc, ug = _branches(l, z, conv_dw, vec(conv_b), vec(conv_ln_g), vec(conv_ln_b),
                           vec(gm_ln_g), vec(gm_ln_b), gm_ws_b, gm_bs_t)
        att_p, k_l, v_l = _ctx_attn(z)
        att_s = _na_attn(l, z, cache_k, cache_v, tz, cos, sin_up, sin_dn)
        x = _merge(l, x, mods, z, yc, ug, att_p, att_s, conv_pw_b, na_out_b, gm_out_b, w_o_b,
                   vec(ln1_g), vec(ln1_b))
        x = _moe(l, x, mods, router_w, router_b, w1_b, w3_b, w2_b, vec(ln2_g), vec(ln2_b))
        new_k.append(k_l.reshape(BATCH, SEQ, N_HEADS, HEAD_DIM))
        new_v.append(v_l.reshape(BATCH, SEQ, N_HEADS, HEAD_DIM))

    y_prompt = x[:T_PROMPT].reshape(BATCH, SEQ, D_MODEL)
    y_sample = x[T_PROMPT:].reshape(DEC_BATCH, DEC_SEQ, D_MODEL)
    return y_prompt, y_sample, jnp.stack(new_k, axis=1), jnp.stack(new_v, axis=1)
```

```python
import functools

import jax
import jax.numpy as jnp
import numpy as np
from jax import lax
from jax.experimental import pallas as pl
from jax.experimental.pallas import tpu as pltpu

F32 = jnp.float32
BF16 = jnp.bfloat16
HIGHEST = lax.Precision.HIGHEST

D_MODEL = 1024
BATCH = 16
SEQ = 256
DEPTH = 4
DEC_BATCH = 2
DEC_SEQ = 1024
PAST_LEN = 256
GRID_W = 64
GRID_H = DEC_SEQ // GRID_W
D_CONV = 512
CONV_WIDTH = 31
CONV_HALF = CONV_WIDTH // 2
HEAD_DIM = 64
N_HEADS = 8
D_NA = 512
WIN_H = 8
WIN_W = 16
ROPE_BASE = 10000.0
D_GM = 512
GM_CHUNK = 128
GM_GROUPS = 4
D_IN = 6656
N_EGROUPS = 4
EXP_PER_GROUP = 8
N_EXPERTS = 32
D_EXPERT = 128
ALPHA = (2 * DEPTH) ** 0.25
LN_EPS = 1e-5
NEG_INF = -1e30

T_PROMPT = BATCH * SEQ
T_SAMPLE = DEC_BATCH * DEC_SEQ
T_ALL = T_PROMPT + T_SAMPLE
N_COND = 8

CB_A, CB_B, CB_Q, CB_K, CB_V, CB_GU, CB_GV, CB_GZ = 0, 1, 2, 3, 4, 5, 6, 7
COL_BLK = 512

TB = 256
N_TB = T_ALL // TB
N_TB_PROMPT = T_PROMPT // TB
TB_PER_SAMPLE = DEC_SEQ // TB
HALO = 16
CONV_ROWS = 32

TM_IN = 1024
TM_MOE = 1024
EXP_STEP = 2
ROUTER_LANES = 128
VMEM_LIMIT = 56 * 1024 * 1024


def _ln(x, g, b):
    mu = jnp.mean(x, axis=-1, keepdims=True)
    xc = x - mu
    var = jnp.mean(xc * xc, axis=-1, keepdims=True)
    return xc * lax.rsqrt(var + LN_EPS) * g + b


def _sigmoid(x):
    return jax.nn.sigmoid(x)


def _gelu(x):
    return jax.nn.gelu(x, approximate=True)


def _cparams(sem):
    return pltpu.CompilerParams(dimension_semantics=sem, vmem_limit_bytes=VMEM_LIMIT)


def _ada_kernel(c_ref, w_ref, b_ref, o_ref):
    c = c_ref[...]
    s = c * _sigmoid(c)
    o_ref[...] = jnp.dot(s, w_ref[...], preferred_element_type=F32, precision=HIGHEST) + b_ref[...]


def _ada(cond, w_ada, b_ada):
    tn = 512
    return pl.pallas_call(
        _ada_kernel,
        name="ada",
        grid=(DEPTH, 6 * D_MODEL // tn),
        in_specs=[
            pl.BlockSpec((N_COND, D_MODEL), lambda l, j: (0, 0)),
            pl.BlockSpec((None, D_MODEL, tn), lambda l, j: (l, 0, j)),
            pl.BlockSpec((None, 1, tn), lambda l, j: (l, 0, j)),
        ],
        out_specs=pl.BlockSpec((None, N_COND, tn), lambda l, j: (l, 0, j)),
        out_shape=jax.ShapeDtypeStruct((DEPTH, N_COND, 6 * D_MODEL), F32),
        compiler_params=_cparams(("arbitrary", "arbitrary")),
    )(cond, w_ada, b_ada.reshape(DEPTH, 1, 6 * D_MODEL))


def _mod_row(i, blocks_per_sample, n_prompt_blocks):
    return jnp.where(i < n_prompt_blocks, 0, 1 + (i - n_prompt_blocks) // blocks_per_sample)


def _inproj_kernel(x_ref, mod_ref, w_ref, z_ref, h_ref):
    @pl.when(pl.program_id(1) == 0)
    def _():
        h_ref[...] = (x_ref[...] * (1.0 + mod_ref[1]) + mod_ref[0]).astype(BF16)

    z_ref[...] = jnp.dot(h_ref[...], w_ref[...].astype(BF16), preferred_element_type=F32)


def _inproj(l, x, mods, w_in):
    n_m = T_ALL // TM_IN
    bps = DEC_SEQ // TM_IN
    npb = T_PROMPT // TM_IN
    return pl.pallas_call(
        _inproj_kernel,
        name="inproj",
        grid=(n_m, D_IN // COL_BLK),
        in_specs=[
            pl.BlockSpec((TM_IN, D_MODEL), lambda i, j: (i, 0)),
            pl.BlockSpec((None, 6, None, 1, D_MODEL),
                         lambda i, j: (l, 0, _mod_row(i, bps, npb), 0, 0)),
            pl.BlockSpec((None, D_MODEL, COL_BLK), lambda i, j: (l, 0, j)),
        ],
        out_specs=pl.BlockSpec((TM_IN, COL_BLK), lambda i, j: (i, j)),
        out_shape=jax.ShapeDtypeStruct((T_ALL, D_IN), F32),
        scratch_shapes=[pltpu.VMEM((TM_IN, D_MODEL), BF16)],
        compiler_params=_cparams(("arbitrary", "arbitrary")),
    )(x, mods, w_in)


def _branch_kernel(ap_ref, ac_ref, an_ref, bp_ref, bc_ref, bn_ref, gu_ref, gv_ref,
                   dw_ref, cb_ref, clg_ref, clb_ref, glg_ref, glb_ref, ws_ref, bst_ref,
                   yc_ref, ug_ref, ypad_ref):
    i = pl.program_id(0)
    j = i - N_TB_PROMPT
    in_sample = i >= N_TB_PROMPT
    has_prev = jnp.logical_and(in_sample, j % TB_PER_SAMPLE != 0)
    has_next = jnp.logical_and(in_sample, j % TB_PER_SAMPLE != TB_PER_SAMPLE - 1)

    yp = ap_ref[...] * _sigmoid(bp_ref[...])
    yn = an_ref[...] * _sigmoid(bn_ref[...])
    ypad_ref[0:HALO, :] = jnp.where(has_prev, yp, 0.0)
    ypad_ref[HALO:HALO + TB, :] = ac_ref[...] * _sigmoid(bc_ref[...])
    ypad_ref[HALO + TB:HALO + TB + HALO, :] = jnp.where(has_next, yn, 0.0)

    off = HALO - CONV_HALF
    for c in range(TB // CONV_ROWS):
        base = c * CONV_ROWS
        acc = jnp.zeros((CONV_ROWS, D_CONV), F32)
        for k in range(CONV_WIDTH):
            acc = acc + ypad_ref[base + off + k:base + off + k + CONV_ROWS, :] * dw_ref[k:k + 1, :]
        y = _ln(acc + cb_ref[...], clg_ref[...], clb_ref[...])
        yc_ref[base:base + CONV_ROWS, :] = (y * _sigmoid(y)).astype(BF16)

    for n in range(TB // GM_CHUNK):
        rows = slice(n * GM_CHUNK, (n + 1) * GM_CHUNK)
        u = _gelu(gu_ref[rows, :])
        v = _ln(_gelu(gv_ref[rows, :]), glg_ref[...], glb_ref[...]).astype(BF16)
        for g in range(GM_GROUPS):
            cols = slice(g * GM_CHUNK, (g + 1) * GM_CHUNK)
            sv = jnp.dot(ws_ref[g], v[:, cols], preferred_element_type=F32) + bst_ref[:, g:g + 1]
            ug_ref[rows, cols] = (u[:, cols] * sv).astype(BF16)


def _branches(l, z, conv_dw, conv_b, conv_ln_g, conv_ln_b, gm_ln_g, gm_ln_b, gm_ws, gm_bs_t):
    halo_per_tb = TB // HALO
    n_halo = T_ALL // HALO

    def cur(cb):
        return pl.BlockSpec((TB, COL_BLK), lambda i: (i, cb))

    def prev(cb):
        return pl.BlockSpec((HALO, COL_BLK), lambda i: (jnp.maximum(i * halo_per_tb - 1, 0), cb))

    def nxt(cb):
        return pl.BlockSpec((HALO, COL_BLK),
                            lambda i: (jnp.minimum((i + 1) * halo_per_tb, n_halo - 1), cb))

    def vec(n):
        return pl.BlockSpec((None, 1, n), lambda i: (l, 0, 0))

    return pl.pallas_call(
        _branch_kernel,
        name="branches",
        grid=(N_TB,),
        in_specs=[
            prev(CB_A), cur(CB_A), nxt(CB_A), prev(CB_B), cur(CB_B), nxt(CB_B),
            cur(CB_GU), cur(CB_GV),
            pl.BlockSpec((None, CONV_WIDTH, D_CONV), lambda i: (l, 0, 0)),
            vec(D_CONV), vec(D_CONV), vec(D_CONV), vec(D_GM), vec(D_GM),
            pl.BlockSpec((None, GM_GROUPS, GM_CHUNK, GM_CHUNK), lambda i: (l, 0, 0, 0)),
            pl.BlockSpec((None, GM_CHUNK, GM_GROUPS), lambda i: (l, 0, 0)),
        ],
        out_specs=[pl.BlockSpec((TB, D_CONV), lambda i: (i, 0)),
                   pl.BlockSpec((TB, D_GM), lambda i: (i, 0))],
        out_shape=[jax.ShapeDtypeStruct((T_ALL, D_CONV), BF16),
                   jax.ShapeDtypeStruct((T_ALL, D_GM), BF16)],
        scratch_shapes=[pltpu.VMEM((TB + 2 * HALO, D_CONV), F32)],
        compiler_params=_cparams(("arbitrary",)),
    )(z, z, z, z, z, z, z, z, conv_dw, conv_b, conv_ln_g, conv_ln_b, gm_ln_g, gm_ln_b,
      gm_ws, gm_bs_t)


def _ctx_attn_kernel(q_ref, k_ref, v_ref, o_ref, ko_ref, vo_ref):
    k = k_ref[...]
    v = v_ref[...]
    ko_ref[...] = k
    vo_ref[...] = v
    q = (q_ref[...] * HEAD_DIM ** -0.5).astype(BF16)
    kb = k.astype(BF16)
    vb = v.astype(BF16)
    for h in range(N_HEADS):
        cols = slice(h * HEAD_DIM, (h + 1) * HEAD_DIM)
        s = lax.dot_general(q[:, cols], kb[:, cols], (((1,), (1,)), ((), ())),
                            preferred_element_type=F32)
        m = jnp.max(s, axis=-1, keepdims=True)
        p = jnp.exp(s - m)
        den = jnp.sum(p, axis=-1, keepdims=True)
        o = jnp.dot(p.astype(BF16), vb[:, cols], preferred_element_type=F32)
        o_ref[:, cols] = (o / den).astype(BF16)


def _ctx_attn(z):
    def col(cb):
        return pl.BlockSpec((SEQ, COL_BLK), lambda b: (b, cb))

    blk = pl.BlockSpec((SEQ, D_NA), lambda b: (b, 0))
    return pl.pallas_call(
        _ctx_attn_kernel,
        name="ctx_attn",
        grid=(BATCH,),
        in_specs=[col(CB_Q), col(CB_K), col(CB_V)],
        out_specs=[blk, blk, blk],
        out_shape=[jax.ShapeDtypeStruct((T_PROMPT, D_NA), BF16),
                   jax.ShapeDtypeStruct((T_PROMPT, D_NA), F32),
                   jax.ShapeDtypeStruct((T_PROMPT, D_NA), F32)],
        compiler_params=_cparams(("arbitrary",)),
    )(z, z, z)


def _rope(x, cos, sin_up, sin_dn):
    return (x * cos + pltpu.roll(x, D_NA - HEAD_DIM // 4, 1) * sin_up
            + pltpu.roll(x, HEAD_DIM // 4, 1) * sin_dn)


def _na_attn_kernel(q_ref, k_ref, v_ref, ck_ref, cv_ref, tz_ref, cos_ref, sup_ref, sdn_ref,
                    o_ref, krot_ref):
    r = pl.program_id(1)

    @pl.when(r == 0)
    def _():
        krot_ref[...] = _rope(k_ref[...], cos_ref[...], sup_ref[...], sdn_ref[...]).astype(BF16)

    qrows = pl.ds(pl.multiple_of(r * GRID_W, GRID_W), GRID_W)
    q = _rope(q_ref[...], cos_ref[qrows, :], sup_ref[qrows, :], sdn_ref[qrows, :])
    q = (q * HEAD_DIM ** -0.5).astype(BF16)

    row_start = jnp.clip(r - WIN_H // 2, 0, GRID_H - WIN_H)
    krows = pl.ds(pl.multiple_of(row_start * GRID_W, GRID_W), WIN_H * GRID_W)
    kwin = krot_ref[krows, :]
    vwin = v_ref[krows, :].astype(BF16)
    ck = ck_ref[...].astype(BF16)
    cv = cv_ref[...].astype(BF16)
    dr0 = row_start - r + WIN_H - 1
    nt = (((1,), (1,)), ((), ()))
    for h in range(N_HEADS):
        cols = slice(h * HEAD_DIM, (h + 1) * HEAD_DIM)
        qh = q[:, cols]
        bias = jnp.concatenate([tz_ref[h, dr0 + w] for w in range(WIN_H)], axis=1)
        s_loc = lax.dot_general(qh, kwin[:, cols], nt, preferred_element_type=F32) + bias
        s_ctx = lax.dot_general(qh, ck[:, cols], nt, preferred_element_type=F32)
        m = jnp.maximum(jnp.max(s_loc, axis=-1, keepdims=True),
                        jnp.max(s_ctx, axis=-1, keepdims=True))
        p_loc = jnp.exp(s_loc - m)
        p_ctx = jnp.exp(s_ctx - m)
        den = jnp.sum(p_loc, axis=-1, keepdims=True) + jnp.sum(p_ctx, axis=-1, keepdims=True)
        o = (jnp.dot(p_loc.astype(BF16), vwin[:, cols], preferred_element_type=F32)
             + jnp.dot(p_ctx.astype(BF16), cv[:, cols], preferred_element_type=F32))
        o_ref[:, cols] = (o / den).astype(BF16)


def _na_attn(l, z, cache_k, cache_v, tz, cos, sin_up, sin_dn):
    seq_blk0 = T_PROMPT // DEC_SEQ
    row_blk0 = T_PROMPT // GRID_W
    full = pl.BlockSpec((DEC_SEQ, D_NA), lambda b, r: (0, 0))
    return pl.pallas_call(
        _na_attn_kernel,
        name="na_attn",
        grid=(DEC_BATCH, GRID_H),
        in_specs=[
            pl.BlockSpec((GRID_W, COL_BLK), lambda b, r: (row_blk0 + b * GRID_H + r, CB_Q)),
            pl.BlockSpec((DEC_SEQ, COL_BLK), lambda b, r: (seq_blk0 + b, CB_K)),
            pl.BlockSpec((DEC_SEQ, COL_BLK), lambda b, r: (seq_blk0 + b, CB_V)),
            pl.BlockSpec((None, None, PAST_LEN, D_NA), lambda b, r: (b, l, 0, 0)),
            pl.BlockSpec((None, None, PAST_LEN, D_NA), lambda b, r: (b, l, 0, 0)),
            pl.BlockSpec((None, N_HEADS, 2 * WIN_H - 1, GRID_W, GRID_W),
                         lambda b, r: (l, 0, 0, 0, 0)),
            full, full, full,
        ],
        out_specs=pl.BlockSpec((GRID_W, D_NA), lambda b, r: (b * GRID_H + r, 0)),
        out_shape=jax.ShapeDtypeStruct((T_SAMPLE, D_NA), BF16),
        scratch_shapes=[pltpu.VMEM((DEC_SEQ, D_NA), BF16)],
        compiler_params=_cparams(("arbitrary", "arbitrary")),
    )(z, z, z, cache_k, cache_v, tz, cos, sin_up, sin_dn)


def _merge_kernel(x_ref, mod_ref, g0a, g0b, g1a, g1b, g2a, g2b, yc_ref, ug_ref, ap_ref, as_ref,
                  pw_ref, no_ref, go_ref, wo_ref, lg_ref, lb_ref, o_ref):
    i = pl.program_id(0)
    att = jnp.where(i < N_TB_PROMPT, ap_ref[...], as_ref[...])
    br_c = jnp.dot(yc_ref[...], pw_ref[...], preferred_element_type=F32)
    br_a = jnp.dot(att, no_ref[...], preferred_element_type=F32)
    br_g = jnp.dot(ug_ref[...], go_ref[...], preferred_element_type=F32)
    h = COL_BLK
    for lo, ga, gb, gc in ((0, g0a, g1a, g2a), (h, g0b, g1b, g2b)):
        cols = slice(lo, lo + h)
        o_ref[:, cols] = (_sigmoid(ga[...]) * br_c[:, cols] + _sigmoid(gb[...]) * br_a[:, cols]
                          + _sigmoid(gc[...]) * br_g[:, cols])
    mix = jnp.dot(o_ref[...].astype(BF16), wo_ref[...], preferred_element_type=F32)
    o_ref[...] = _ln(ALPHA * x_ref[...] + mod_ref[2] * mix, lg_ref[...], lb_ref[...])


def _merge(l, x, mods, z, yc, ug, att_p, att_s, conv_pw, na_out, gm_out, w_o, ln_g, ln_b):
    def gz(k):
        return pl.BlockSpec((TB, COL_BLK), lambda i: (i, CB_GZ + k))

    def w(k, n):
        return pl.BlockSpec((None, k, n), lambda i: (l, 0, 0))

    blk512 = pl.BlockSpec((TB, COL_BLK), lambda i: (i, 0))
    return pl.pallas_call(
        _merge_kernel,
        name="merge",
        grid=(N_TB,),
        in_specs=[
            pl.BlockSpec((TB, D_MODEL), lambda i: (i, 0)),
            pl.BlockSpec((None, 6, None, 1, D_MODEL),
                         lambda i: (l, 0, _mod_row(i, TB_PER_SAMPLE, N_TB_PROMPT), 0, 0)),
            gz(0), gz(1), gz(2), gz(3), gz(4), gz(5),
            blk512, blk512,
            pl.BlockSpec((TB, D_NA), lambda i: (jnp.minimum(i, N_TB_PROMPT - 1), 0)),
            pl.BlockSpec((TB, D_NA), lambda i: (jnp.maximum(i - N_TB_PROMPT, 0), 0)),
            w(D_CONV, D_MODEL), w(D_NA, D_MODEL), w(D_GM, D_MODEL), w(D_MODEL, D_MODEL),
            w(1, D_MODEL), w(1, D_MODEL),
        ],
        out_specs=pl.BlockSpec((TB, D_MODEL), lambda i: (i, 0)),
        out_shape=jax.ShapeDtypeStruct((T_ALL, D_MODEL), F32),
        compiler_params=_cparams(("arbitrary",)),
    )(x, mods, z, z, z, z, z, z, yc, ug, att_p, att_s, conv_pw, na_out, gm_out, w_o, ln_g, ln_b)


def _route(logits):
    lane = lax.broadcasted_iota(jnp.int32, logits.shape, 1)
    big = jnp.int32(ROUTER_LANES)
    is_g = lane < N_EGROUPS
    gl = jnp.where(is_g, logits, -jnp.inf)
    gmax = jnp.max(gl, axis=-1, keepdims=True)
    gidx = jnp.min(jnp.where(gl == gmax, lane, big), axis=-1, keepdims=True)
    gp = 1.0 / jnp.sum(jnp.where(is_g, jnp.exp(gl - gmax), 0.0), axis=-1, keepdims=True)
    lo = N_EGROUPS + gidx * EXP_PER_GROUP
    el = jnp.where(jnp.logical_and(lane >= lo, lane < lo + EXP_PER_GROUP), logits, -jnp.inf)
    v1 = jnp.max(el, axis=-1, keepdims=True)
    i1 = jnp.min(jnp.where(el == v1, lane, big), axis=-1, keepdims=True)
    el2 = jnp.where(lane == i1, -jnp.inf, el)
    v2 = jnp.max(el2, axis=-1, keepdims=True)
    i2 = jnp.min(jnp.where(el2 == v2, lane, big), axis=-1, keepdims=True)
    e2 = jnp.exp(v2 - v1)
    w1 = gp / (1.0 + e2)
    w2 = gp * e2 / (1.0 + e2)
    return jnp.where(lane == i1, w1, 0.0) + jnp.where(lane == i2, w2, 0.0)


def _moe_kernel(x_ref, mod_ref, rw_ref, rb_ref, w1_ref, w3_ref, w2_ref, lg_ref, lb_ref,
                o_ref, t_ref, gate_ref, acc_ref):
    e = pl.program_id(1)

    @pl.when(e == 0)
    def _():
        t = x_ref[...] * (1.0 + mod_ref[4]) + mod_ref[3]
        t_hi = t.astype(BF16)
        t_ref[...] = t_hi
        t_lo = (t - t_hi.astype(F32)).astype(BF16)
        logits = (jnp.dot(t_hi, rw_ref[0], preferred_element_type=F32)
                  + jnp.dot(t_lo, rw_ref[0], preferred_element_type=F32)
                  + jnp.dot(t_hi, rw_ref[1], preferred_element_type=F32))
        gate_ref[...] = _route(logits + rb_ref[...])
        acc_ref[...] = jnp.zeros_like(acc_ref)

    t = t_ref[...]
    w1 = jnp.concatenate([w1_ref[k] for k in range(EXP_STEP)], axis=1).astype(BF16)
    w3 = jnp.concatenate([w3_ref[k] for k in range(EXP_STEP)], axis=1).astype(BF16)
    w2 = w2_ref[...].reshape(EXP_STEP * D_EXPERT, D_MODEL).astype(BF16)
    h1 = jnp.dot(t, w1, preferred_element_type=F32)
    h3 = jnp.dot(t, w3, preferred_element_type=F32)
    gate = gate_ref[...]
    lane = lax.broadcasted_iota(jnp.int32, gate.shape, 1)
    hcol = lax.broadcasted_iota(jnp.int32, h1.shape, 1) // D_EXPERT
    gmul = jnp.zeros(h1.shape, F32)
    for k in range(EXP_STEP):
        gcol = jnp.sum(jnp.where(lane == e * EXP_STEP + k + N_EGROUPS, gate, 0.0),
                       axis=-1, keepdims=True)
        gmul = jnp.where(hcol == k, gcol, gmul)
    hid = (h1 * _sigmoid(h1) * h3 * gmul).astype(BF16)
    acc_ref[...] += jnp.dot(hid, w2, preferred_element_type=F32)

    @pl.when(e == N_EXPERTS // EXP_STEP - 1)
    def _():
        o_ref[...] = _ln(ALPHA * x_ref[...] + mod_ref[5] * acc_ref[...], lg_ref[...], lb_ref[...])


def _moe(l, x, mods, router_w, router_b, w1, w3, w2, ln_g, ln_b):
    n_m = T_ALL // TM_MOE
    bps = DEC_SEQ // TM_MOE
    npb = T_PROMPT // TM_MOE
    return pl.pallas_call(
        _moe_kernel,
        name="moe",
        grid=(n_m, N_EXPERTS // EXP_STEP),
        in_specs=[
            pl.BlockSpec((TM_MOE, D_MODEL), lambda i, e: (i, 0)),
            pl.BlockSpec((None, 6, None, 1, D_MODEL),
                         lambda i, e: (l, 0, _mod_row(i, bps, npb), 0, 0)),
            pl.BlockSpec((None, 2, D_MODEL, ROUTER_LANES), lambda i, e: (l, 0, 0, 0)),
            pl.BlockSpec((None, 1, ROUTER_LANES), lambda i, e: (l, 0, 0)),
            pl.BlockSpec((None, EXP_STEP, D_MODEL, D_EXPERT), lambda i, e: (l, e, 0, 0)),
            pl.BlockSpec((None, EXP_STEP, D_MODEL, D_EXPERT), lambda i, e: (l, e, 0, 0)),
            pl.BlockSpec((None, EXP_STEP, D_EXPERT, D_MODEL), lambda i, e: (l, e, 0, 0)),
            pl.BlockSpec((None, 1, D_MODEL), lambda i, e: (l, 0, 0)),
            pl.BlockSpec((None, 1, D_MODEL), lambda i, e: (l, 0, 0)),
        ],
        out_specs=pl.BlockSpec((TM_MOE, D_MODEL), lambda i, e: (i, 0)),
        out_shape=jax.ShapeDtypeStruct((T_ALL, D_MODEL), F32),
        scratch_shapes=[pltpu.VMEM((TM_MOE, D_MODEL), BF16),
                        pltpu.VMEM((TM_MOE, ROUTER_LANES), F32),
                        pltpu.VMEM((TM_MOE, D_MODEL), F32)],
        compiler_params=_cparams(("arbitrary", "arbitrary")),
    )(x, mods, router_w, router_b, w1, w3, w2, ln_g, ln_b)


def _rope_tables():
    t = np.arange(DEC_SEQ)
    pos = np.stack([t // GRID_W, t % GRID_W], axis=1).astype(np.float32)
    quarter = HEAD_DIM // 4
    d = np.arange(HEAD_DIM)
    axis = d // (HEAD_DIM // 2)
    freq = d % quarter
    upper = (d % (HEAD_DIM // 2)) >= quarter
    inv = jnp.asarray(ROPE_BASE, F32) ** (-jnp.arange(0, HEAD_DIM // 2, 2, dtype=F32) / (HEAD_DIM // 2))
    ang = jnp.asarray(pos)[:, axis] * inv[freq][None, :]
    cos = jnp.cos(ang)
    sin = jnp.sin(ang)
    sin_up = jnp.where(upper[None, :], 0.0, -sin)
    sin_dn = jnp.where(upper[None, :], sin, 0.0)
    tile = lambda a: jnp.tile(a, (1, N_HEADS))
    return tile(cos), tile(sin_up), tile(sin_dn)


def _bias_tables(na_rpb):
    qc = np.arange(GRID_W)[:, None]
    kc = np.arange(GRID_W)[None, :]
    start = np.clip(qc - WIN_W // 2, 0, GRID_W - WIN_W)
    valid = (kc >= start) & (kc < start + WIN_W)
    pad_lo = GRID_W - WIN_W
    span = 2 * GRID_W - 1
    lead = na_rpb.shape[:3]
    rr = jnp.pad(na_rpb, ((0, 0), (0, 0), (0, 0), (pad_lo, span - pad_lo - (2 * WIN_W - 1))))
    flat = jnp.broadcast_to(rr[..., None, :], lead + (GRID_W, span)).reshape(lead + (GRID_W * span,))
    toep = flat[..., GRID_W - 1:GRID_W - 1 + GRID_W * (span - 1)]
    toep = toep.reshape(lead + (GRID_W, span - 1))[..., :GRID_W]
    return jnp.where(jnp.asarray(valid), toep, NEG_INF)


def kernel(x_prompt, x_sample, cache_na_k, cache_na_v, c, c_ctx, w_ada, b_ada, w_in, conv_dw,
           conv_b, conv_ln_g, conv_ln_b, conv_pw, na_rpb, na_out, gm_ln_g, gm_ln_b, gm_ws, gm_bs,
           gm_out, w_o, ln1_g, ln1_b, rg_w, rg_b, re_w, re_b, moe_w1, moe_w3, moe_w2, ln2_g, ln2_b):
    x = jnp.concatenate([x_prompt.reshape(T_PROMPT, D_MODEL),
                         x_sample.reshape(T_SAMPLE, D_MODEL)], axis=0)

    cond = jnp.zeros((N_COND, D_MODEL), F32).at[0].set(c_ctx).at[1:1 + DEC_BATCH].set(c)
    mods = _ada(cond, w_ada, b_ada)
    mods = mods.reshape(DEPTH, N_COND, 6, 1, D_MODEL).transpose(0, 2, 1, 3, 4)

    bf = lambda a: a.astype(BF16)
    conv_pw_b, na_out_b, gm_out_b, w_o_b, gm_ws_b = bf(conv_pw), bf(na_out), bf(gm_out), bf(w_o), bf(gm_ws)
    vec = lambda a: a.reshape(DEPTH, 1, a.shape[-1])
    gm_bs_t = gm_bs.transpose(0, 2, 1)
    router_w = jnp.concatenate(
        [rg_w, re_w.transpose(0, 2, 1, 3).reshape(DEPTH, D_MODEL, N_EXPERTS)], axis=-1)
    router_w = jnp.pad(router_w, ((0, 0), (0, 0), (0, ROUTER_LANES - N_EGROUPS - N_EXPERTS)))
    router_hi = router_w.astype(BF16)
    router_lo = (router_w - router_hi.astype(F32)).astype(BF16)
    router_w = jnp.stack([router_hi, router_lo], axis=1)
    router_b = jnp.concatenate([rg_b, re_b.reshape(DEPTH, N_EXPERTS)], axis=-1)
    router_b = jnp.pad(router_b, ((0, 0), (0, ROUTER_LANES - N_EGROUPS - N_EXPERTS)))
    router_b = router_b.reshape(DEPTH, 1, ROUTER_LANES)
    cache_k = cache_na_k.reshape(DEC_BATCH, DEPTH, PAST_LEN, D_NA)
    cache_v = cache_na_v.reshape(DEC_BATCH, DEPTH, PAST_LEN, D_NA)
    tz = _bias_tables(na_rpb)
    cos, sin_up, sin_dn = _rope_tables()

    new_k, new_v = [], []
    for l in range(DEPTH):
        z = _inproj(l, x, mods, w_in)
        yc, ug = _branches(l, z, conv_dw, vec(conv_b), vec(conv_ln_g), vec(conv_ln_b),
                           vec(gm_ln_g), vec(gm_ln_b), gm_ws_b, gm_bs_t)
        att_p, k_l, v_l = _ctx_attn(z)
        att_s = _na_attn(l, z, cache_k, cache_v, tz, cos, sin_up, sin_dn)
        x = _merge(l, x, mods, z, yc, ug, att_p, att_s, conv_pw_b, na_out_b, gm_out_b, w_o_b,
                   vec(ln1_g), vec(ln1_b))
        x = _moe(l, x, mods, router_w, router_b, moe_w1, moe_w3, moe_w2, vec(ln2_g), vec(ln2_b))
        new_k.append(k_l.reshape(BATCH, SEQ, N_HEADS, HEAD_DIM))
        new_v.append(v_l.reshape(BATCH, SEQ, N_HEADS, HEAD_DIM))

    y_prompt = x[:T_PROMPT].reshape(BATCH, SEQ, D_MODEL)
    y_sample = x[T_PROMPT:].reshape(DEC_BATCH, DEC_SEQ, D_MODEL)
    return y_prompt, y_sample, jnp.stack(new_k, axis=1), jnp.stack(new_v, axis=1)
```

```python
import functools

import jax
import jax.numpy as jnp
import numpy as np
from jax import lax
from jax.experimental import pallas as pl
from jax.experimental.pallas import tpu as pltpu

F32 = jnp.float32
BF16 = jnp.bfloat16
HIGHEST = lax.Precision.HIGHEST

D_MODEL = 1024
BATCH = 16
SEQ = 256
DEPTH = 4
DEC_BATCH = 2
DEC_SEQ = 1024
PAST_LEN = 256
GRID_W = 64
GRID_H = DEC_SEQ // GRID_W
D_CONV = 512
CONV_WIDTH = 31
CONV_HALF = CONV_WIDTH // 2
HEAD_DIM = 64
N_HEADS = 8
D_NA = 512
WIN_H = 8
WIN_W = 16
ROPE_BASE = 10000.0
D_GM = 512
GM_CHUNK = 128
GM_GROUPS = 4
D_IN = 6656
N_EGROUPS = 4
EXP_PER_GROUP = 8
N_EXPERTS = 32
D_EXPERT = 128
ALPHA = (2 * DEPTH) ** 0.25
LN_EPS = 1e-5
NEG_INF = -1e30

T_PROMPT = BATCH * SEQ
T_SAMPLE = DEC_BATCH * DEC_SEQ
T_ALL = T_PROMPT + T_SAMPLE
N_COND = 8

COL_BLK = 512
W_CB_KV = 3
N_CB_KV = 2
N_CB_MAIN = D_IN // COL_BLK - N_CB_KV
CB_A, CB_B, CB_Q, CB_GU, CB_GV, CB_GZ = 0, 1, 2, 3, 4, 5
CB_K, CB_V = 0, 1

TB = 256
N_TB = T_ALL // TB
N_TB_PROMPT = T_PROMPT // TB
TB_PER_SAMPLE = DEC_SEQ // TB
HALO = 16
CONV_ROWS = 32

TM_IN = 1024
TM_MOE = 1024
EXP_STEP = 2
ROUTER_LANES = 128
VMEM_LIMIT = 56 * 1024 * 1024


def _ln(x, g, b):
    mu = jnp.mean(x, axis=-1, keepdims=True)
    xc = x - mu
    var = jnp.mean(xc * xc, axis=-1, keepdims=True)
    return xc * lax.rsqrt(var + LN_EPS) * g + b


def _sigmoid(x):
    return jax.nn.sigmoid(x)


def _gelu(x):
    return jax.nn.gelu(x, approximate=True)


def _cparams(sem):
    return pltpu.CompilerParams(dimension_semantics=sem, vmem_limit_bytes=VMEM_LIMIT)


def _ada_kernel(c_ref, w_ref, b_ref, o_ref):
    c = c_ref[...]
    s = c * _sigmoid(c)
    o_ref[...] = jnp.dot(s, w_ref[...], preferred_element_type=F32, precision=HIGHEST) + b_ref[...]


def _ada(cond, w_ada, b_ada):
    tn = 512
    return pl.pallas_call(
        _ada_kernel,
        name="ada",
        grid=(DEPTH, 6 * D_MODEL // tn),
        in_specs=[
            pl.BlockSpec((N_COND, D_MODEL), lambda l, j: (0, 0)),
            pl.BlockSpec((None, D_MODEL, tn), lambda l, j: (l, 0, j)),
            pl.BlockSpec((None, 1, tn), lambda l, j: (l, 0, j)),
        ],
        out_specs=pl.BlockSpec((None, N_COND, tn), lambda l, j: (l, 0, j)),
        out_shape=jax.ShapeDtypeStruct((DEPTH, N_COND, 6 * D_MODEL), F32),
        compiler_params=_cparams(("arbitrary", "arbitrary")),
    )(cond, w_ada, b_ada.reshape(DEPTH, 1, 6 * D_MODEL))


def _mod_row(i, blocks_per_sample, n_prompt_blocks):
    return jnp.where(i < n_prompt_blocks, 0, 1 + (i - n_prompt_blocks) // blocks_per_sample)


def _modulate_kernel(x_ref, mod_ref, h_ref):
    h_ref[...] = (x_ref[...] * (1.0 + mod_ref[1]) + mod_ref[0]).astype(BF16)


def _modulate(l, x, mods):
    bps = DEC_SEQ // TM_IN
    npb = T_PROMPT // TM_IN
    return pl.pallas_call(
        _modulate_kernel,
        name="modulate",
        grid=(T_ALL // TM_IN,),
        in_specs=[
            pl.BlockSpec((TM_IN, D_MODEL), lambda i: (i, 0)),
            pl.BlockSpec((None, 6, None, 1, D_MODEL),
                         lambda i: (l, 0, _mod_row(i, bps, npb), 0, 0)),
        ],
        out_specs=pl.BlockSpec((TM_IN, D_MODEL), lambda i: (i, 0)),
        out_shape=jax.ShapeDtypeStruct((T_ALL, D_MODEL), BF16),
        compiler_params=_cparams(("arbitrary",)),
    )(x, mods)


def _inproj_kernel(h_ref, w_ref, z_ref):
    rows = pl.ds(pl.multiple_of(pl.program_id(1) * TM_IN, TM_IN), TM_IN)
    z = jnp.dot(h_ref[rows, :], w_ref[...].astype(BF16), preferred_element_type=F32)
    z_ref[...] = z.astype(z_ref.dtype)


def _inproj(l, h, w_in, wcol, n_cols, out_dtype):
    return pl.pallas_call(
        _inproj_kernel,
        name="inproj",
        grid=(n_cols, T_ALL // TM_IN),
        in_specs=[
            pl.BlockSpec((T_ALL, D_MODEL), lambda j, i: (0, 0)),
            pl.BlockSpec((None, D_MODEL, COL_BLK), lambda j, i: (l, 0, wcol(j))),
        ],
        out_specs=pl.BlockSpec((TM_IN, COL_BLK), lambda j, i: (i, j)),
        out_shape=jax.ShapeDtypeStruct((T_ALL, n_cols * COL_BLK), out_dtype),
        compiler_params=_cparams(("arbitrary", "arbitrary")),
    )(h, w_in)


def _branch_kernel(ap_ref, ac_ref, an_ref, bp_ref, bc_ref, bn_ref, gu_ref, gv_ref,
                   dw_ref, cb_ref, clg_ref, clb_ref, glg_ref, glb_ref, ws_ref, bst_ref,
                   yc_ref, ug_ref, ypad_ref):
    i = pl.program_id(0)
    j = i - N_TB_PROMPT
    in_sample = i >= N_TB_PROMPT
    has_prev = jnp.logical_and(in_sample, j % TB_PER_SAMPLE != 0)
    has_next = jnp.logical_and(in_sample, j % TB_PER_SAMPLE != TB_PER_SAMPLE - 1)

    def glu(a_ref, b_ref):
        return a_ref[...].astype(F32) * _sigmoid(b_ref[...].astype(F32))

    ypad_ref[0:HALO, :] = jnp.where(has_prev, glu(ap_ref, bp_ref), 0.0)
    ypad_ref[HALO:HALO + TB, :] = glu(ac_ref, bc_ref)
    ypad_ref[HALO + TB:HALO + TB + HALO, :] = jnp.where(has_next, glu(an_ref, bn_ref), 0.0)

    off = HALO - CONV_HALF
    for c in range(TB // CONV_ROWS):
        base = c * CONV_ROWS
        acc = jnp.zeros((CONV_ROWS, D_CONV), F32)
        for k in range(CONV_WIDTH):
            acc = acc + ypad_ref[base + off + k:base + off + k + CONV_ROWS, :] * dw_ref[k:k + 1, :]
        y = _ln(acc + cb_ref[...], clg_ref[...], clb_ref[...])
        yc_ref[base:base + CONV_ROWS, :] = (y * _sigmoid(y)).astype(BF16)

    for n in range(TB // GM_CHUNK):
        rows = slice(n * GM_CHUNK, (n + 1) * GM_CHUNK)
        u = _gelu(gu_ref[rows, :].astype(F32))
        v = _ln(_gelu(gv_ref[rows, :].astype(F32)), glg_ref[...], glb_ref[...]).astype(BF16)
        for g in range(GM_GROUPS):
            cols = slice(g * GM_CHUNK, (g + 1) * GM_CHUNK)
            sv = jnp.dot(ws_ref[g], v[:, cols], preferred_element_type=F32) + bst_ref[:, g:g + 1]
            ug_ref[rows, cols] = (u[:, cols] * sv).astype(BF16)


def _branches(l, z, conv_dw, conv_b, conv_ln_g, conv_ln_b, gm_ln_g, gm_ln_b, gm_ws, gm_bs_t):
    halo_per_tb = TB // HALO
    n_halo = T_ALL // HALO

    def cur(cb):
        return pl.BlockSpec((TB, COL_BLK), lambda i: (i, cb))

    def prev(cb):
        return pl.BlockSpec((HALO, COL_BLK), lambda i: (jnp.maximum(i * halo_per_tb - 1, 0), cb))

    def nxt(cb):
        return pl.BlockSpec((HALO, COL_BLK),
                            lambda i: (jnp.minimum((i + 1) * halo_per_tb, n_halo - 1), cb))

    def vec(n):
        return pl.BlockSpec((None, 1, n), lambda i: (l, 0, 0))

    return pl.pallas_call(
        _branch_kernel,
        name="branches",
        grid=(N_TB,),
        in_specs=[
            prev(CB_A), cur(CB_A), nxt(CB_A), prev(CB_B), cur(CB_B), nxt(CB_B),
            cur(CB_GU), cur(CB_GV),
            pl.BlockSpec((None, CONV_WIDTH, D_CONV), lambda i: (l, 0, 0)),
            vec(D_CONV), vec(D_CONV), vec(D_CONV), vec(D_GM), vec(D_GM),
            pl.BlockSpec((None, GM_GROUPS, GM_CHUNK, GM_CHUNK), lambda i: (l, 0, 0, 0)),
            pl.BlockSpec((None, GM_CHUNK, GM_GROUPS), lambda i: (l, 0, 0)),
        ],
        out_specs=[pl.BlockSpec((TB, D_CONV), lambda i: (i, 0)),
                   pl.BlockSpec((TB, D_GM), lambda i: (i, 0))],
        out_shape=[jax.ShapeDtypeStruct((T_ALL, D_CONV), BF16),
                   jax.ShapeDtypeStruct((T_ALL, D_GM), BF16)],
        scratch_shapes=[pltpu.VMEM((TB + 2 * HALO, D_CONV), F32)],
        compiler_params=_cparams(("arbitrary",)),
    )(z, z, z, z, z, z, z, z, conv_dw, conv_b, conv_ln_g, conv_ln_b, gm_ln_g, gm_ln_b,
      gm_ws, gm_bs_t)


def _ctx_attn_kernel(q_ref, k_ref, v_ref, kin_ref, vin_ref, o_ref, ko_ref, vo_ref):
    del kin_ref, vin_ref
    kt = k_ref[...].T.reshape(N_HEADS, HEAD_DIM, SEQ)
    vt = v_ref[...].T.reshape(N_HEADS, HEAD_DIM, SEQ)
    ko_ref[...] = kt
    vo_ref[...] = vt
    q = (q_ref[...].astype(F32) * HEAD_DIM ** -0.5).astype(BF16)
    kb = kt.astype(BF16)
    vb = vt.astype(BF16)
    for h in range(N_HEADS):
        cols = slice(h * HEAD_DIM, (h + 1) * HEAD_DIM)
        s = jnp.dot(q[:, cols], kb[h], preferred_element_type=F32)
        m = jnp.max(s, axis=-1, keepdims=True)
        p = jnp.exp(s - m)
        den = jnp.sum(p, axis=-1, keepdims=True)
        o = lax.dot_general(p.astype(BF16), vb[h], (((1,), (1,)), ((), ())),
                            preferred_element_type=F32)
        o_ref[:, cols] = (o / den).astype(BF16)


def _ctx_attn(l, z, zkv, kt_all, vt_all):
    cache_blk = pl.BlockSpec((None, None, N_HEADS, HEAD_DIM, SEQ), lambda b: (b, l, 0, 0, 0))
    cache_shape = jax.ShapeDtypeStruct((BATCH, DEPTH, N_HEADS, HEAD_DIM, SEQ), F32)
    return pl.pallas_call(
        _ctx_attn_kernel,
        name="ctx_attn",
        grid=(BATCH,),
        in_specs=[pl.BlockSpec((SEQ, COL_BLK), lambda b: (b, CB_Q)),
                  pl.BlockSpec((SEQ, COL_BLK), lambda b: (b, CB_K)),
                  pl.BlockSpec((SEQ, COL_BLK), lambda b: (b, CB_V)),
                  pl.BlockSpec(memory_space=pl.ANY), pl.BlockSpec(memory_space=pl.ANY)],
        out_specs=[pl.BlockSpec((SEQ, D_NA), lambda b: (b, 0)), cache_blk, cache_blk],
        out_shape=[jax.ShapeDtypeStruct((T_PROMPT, D_NA), BF16), cache_shape, cache_shape],
        input_output_aliases={3: 1, 4: 2},
        compiler_params=_cparams(("arbitrary",)),
    )(z, zkv, zkv, kt_all, vt_all)


def _rope(x, cos, sin_up, sin_dn):
    return (x * cos + pltpu.roll(x, D_NA - HEAD_DIM // 4, 1) * sin_up
            + pltpu.roll(x, HEAD_DIM // 4, 1) * sin_dn)


def _na_attn_kernel(q_ref, k_ref, v_ref, ck_ref, cv_ref, tz_ref, cos_ref, sup_ref, sdn_ref,
                    o_ref, krot_ref):
    r = pl.program_id(1)

    @pl.when(r == 0)
    def _():
        krot_ref[...] = _rope(k_ref[...], cos_ref[...], sup_ref[...], sdn_ref[...]).astype(BF16)

    qrows = pl.ds(pl.multiple_of(r * GRID_W, GRID_W), GRID_W)
    q = _rope(q_ref[...].astype(F32), cos_ref[qrows, :], sup_ref[qrows, :], sdn_ref[qrows, :])
    q = (q * HEAD_DIM ** -0.5).astype(BF16)

    row_start = jnp.clip(r - WIN_H // 2, 0, GRID_H - WIN_H)
    krows = pl.ds(pl.multiple_of(row_start * GRID_W, GRID_W), WIN_H * GRID_W)
    kwin = krot_ref[krows, :]
    vwin = v_ref[krows, :].astype(BF16)
    ck = ck_ref[...].astype(BF16)
    cv = cv_ref[...].astype(BF16)
    dr0 = row_start - r + WIN_H - 1
    nt = (((1,), (1,)), ((), ()))
    for h in range(N_HEADS):
        cols = slice(h * HEAD_DIM, (h + 1) * HEAD_DIM)
        qh = q[:, cols]
        bias = jnp.concatenate([tz_ref[h, dr0 + w] for w in range(WIN_H)], axis=1)
        s_loc = lax.dot_general(qh, kwin[:, cols], nt, preferred_element_type=F32) + bias
        s_ctx = lax.dot_general(qh, ck[:, cols], nt, preferred_element_type=F32)
        m = jnp.maximum(jnp.max(s_loc, axis=-1, keepdims=True),
                        jnp.max(s_ctx, axis=-1, keepdims=True))
        p_loc = jnp.exp(s_loc - m)
        p_ctx = jnp.exp(s_ctx - m)
        den = jnp.sum(p_loc, axis=-1, keepdims=True) + jnp.sum(p_ctx, axis=-1, keepdims=True)
        o = (jnp.dot(p_loc.astype(BF16), vwin[:, cols], preferred_element_type=F32)
             + jnp.dot(p_ctx.astype(BF16), cv[:, cols], preferred_element_type=F32))
        o_ref[:, cols] = (o / den).astype(BF16)


def _na_attn(l, z, zkv, cache_k, cache_v, tz, cos, sin_up, sin_dn):
    seq_blk0 = T_PROMPT // DEC_SEQ
    row_blk0 = T_PROMPT // GRID_W
    full = pl.BlockSpec((DEC_SEQ, D_NA), lambda b, r: (0, 0))
    return pl.pallas_call(
        _na_attn_kernel,
        name="na_attn",
        grid=(DEC_BATCH, GRID_H),
        in_specs=[
            pl.BlockSpec((GRID_W, COL_BLK), lambda b, r: (row_blk0 + b * GRID_H + r, CB_Q)),
            pl.BlockSpec((DEC_SEQ, COL_BLK), lambda b, r: (seq_blk0 + b, CB_K)),
            pl.BlockSpec((DEC_SEQ, COL_BLK), lambda b, r: (seq_blk0 + b, CB_V)),
            pl.BlockSpec((None, None, PAST_LEN, D_NA), lambda b, r: (b, l, 0, 0)),
            pl.BlockSpec((None, None, PAST_LEN, D_NA), lambda b, r: (b, l, 0, 0)),
            pl.BlockSpec((None, N_HEADS, 2 * WIN_H - 1, GRID_W, GRID_W),
                         lambda b, r: (l, 0, 0, 0, 0)),
            full, full, full,
        ],
        out_specs=pl.BlockSpec((GRID_W, D_NA), lambda b, r: (b * GRID_H + r, 0)),
        out_shape=jax.ShapeDtypeStruct((T_SAMPLE, D_NA), BF16),
        scratch_shapes=[pltpu.VMEM((DEC_SEQ, D_NA), BF16)],
        compiler_params=_cparams(("arbitrary", "arbitrary")),
    )(z, zkv, zkv, cache_k, cache_v, tz, cos, sin_up, sin_dn)


def _merge_kernel(x_ref, mod_ref, g0a, g0b, g1a, g1b, g2a, g2b, yc_ref, ug_ref, ap_ref, as_ref,
                  pw_ref, no_ref, go_ref, wo_ref, lg_ref, lb_ref, o_ref):
    i = pl.program_id(0)
    att = jnp.where(i < N_TB_PROMPT, ap_ref[...], as_ref[...])
    br_c = jnp.dot(yc_ref[...], pw_ref[...], preferred_element_type=F32)
    br_a = jnp.dot(att, no_ref[...], preferred_element_type=F32)
    br_g = jnp.dot(ug_ref[...], go_ref[...], preferred_element_type=F32)
    h = COL_BLK
    for lo, ga, gb, gc in ((0, g0a, g1a, g2a), (h, g0b, g1b, g2b)):
        cols = slice(lo, lo + h)
        o_ref[:, cols] = (_sigmoid(ga[...].astype(F32)) * br_c[:, cols]
                          + _sigmoid(gb[...].astype(F32)) * br_a[:, cols]
                          + _sigmoid(gc[...].astype(F32)) * br_g[:, cols])
    mix = jnp.dot(o_ref[...].astype(BF16), wo_ref[...], preferred_element_type=F32)
    o_ref[...] = _ln(ALPHA * x_ref[...] + mod_ref[2] * mix, lg_ref[...], lb_ref[...])


def _merge(l, x, mods, z, yc, ug, att_p, att_s, conv_pw, na_out, gm_out, w_o, ln_g, ln_b):
    def gz(k):
        return pl.BlockSpec((TB, COL_BLK), lambda i: (i, CB_GZ + k))

    def w(k, n):
        return pl.BlockSpec((None, k, n), lambda i: (l, 0, 0))

    blk512 = pl.BlockSpec((TB, COL_BLK), lambda i: (i, 0))
    return pl.pallas_call(
        _merge_kernel,
        name="merge",
        grid=(N_TB,),
        in_specs=[
            pl.BlockSpec((TB, D_MODEL), lambda i: (i, 0)),
            pl.BlockSpec((None, 6, None, 1, D_MODEL),
                         lambda i: (l, 0, _mod_row(i, TB_PER_SAMPLE, N_TB_PROMPT), 0, 0)),
            gz(0), gz(1), gz(2), gz(3), gz(4), gz(5),
            blk512, blk512,
            pl.BlockSpec((TB, D_NA), lambda i: (jnp.minimum(i, N_TB_PROMPT - 1), 0)),
            pl.BlockSpec((TB, D_NA), lambda i: (jnp.maximum(i - N_TB_PROMPT, 0), 0)),
            w(D_CONV, D_MODEL), w(D_NA, D_MODEL), w(D_GM, D_MODEL), w(D_MODEL, D_MODEL),
            w(1, D_MODEL), w(1, D_MODEL),
        ],
        out_specs=pl.BlockSpec((TB, D_MODEL), lambda i: (i, 0)),
        out_shape=jax.ShapeDtypeStruct((T_ALL, D_MODEL), F32),
        compiler_params=_cparams(("arbitrary",)),
    )(x, mods, z, z, z, z, z, z, yc, ug, att_p, att_s, conv_pw, na_out, gm_out, w_o, ln_g, ln_b)


def _route(logits):
    lane = lax.broadcasted_iota(jnp.int32, logits.shape, 1)
    big = jnp.int32(ROUTER_LANES)
    is_g = lane < N_EGROUPS
    gl = jnp.where(is_g, logits, -jnp.inf)
    gmax = jnp.max(gl, axis=-1, keepdims=True)
    gidx = jnp.min(jnp.where(gl == gmax, lane, big), axis=-1, keepdims=True)
    gp = 1.0 / jnp.sum(jnp.where(is_g, jnp.exp(gl - gmax), 0.0), axis=-1, keepdims=True)
    lo = N_EGROUPS + gidx * EXP_PER_GROUP
    el = jnp.where(jnp.logical_and(lane >= lo, lane < lo + EXP_PER_GROUP), logits, -jnp.inf)
    v1 = jnp.max(el, axis=-1, keepdims=True)
    i1 = jnp.min(jnp.where(el == v1, lane, big), axis=-1, keepdims=True)
    el2 = jnp.where(lane == i1, -jnp.inf, el)
    v2 = jnp.max(el2, axis=-1, keepdims=True)
    i2 = jnp.min(jnp.where(el2 == v2, lane, big), axis=-1, keepdims=True)
    e2 = jnp.exp(v2 - v1)
    w1 = gp / (1.0 + e2)
    w2 = gp * e2 / (1.0 + e2)
    return jnp.where(lane == i1, w1, 0.0) + jnp.where(lane == i2, w2, 0.0)


def _moe_kernel(x_ref, mod_ref, rw_ref, rb_ref, w1_ref, w3_ref, w2_ref, lg_ref, lb_ref,
                o_ref, t_ref, gate_ref, acc_ref):
    e = pl.program_id(1)

    @pl.when(e == 0)
    def _():
        t = x_ref[...] * (1.0 + mod_ref[4]) + mod_ref[3]
        t_hi = t.astype(BF16)
        t_ref[...] = t_hi
        t_lo = (t - t_hi.astype(F32)).astype(BF16)
        logits = (jnp.dot(t_hi, rw_ref[0], preferred_element_type=F32)
                  + jnp.dot(t_lo, rw_ref[0], preferred_element_type=F32)
                  + jnp.dot(t_hi, rw_ref[1], preferred_element_type=F32))
        gate_ref[...] = _route(logits + rb_ref[...])
        acc_ref[...] = jnp.zeros_like(acc_ref)

    t = t_ref[...]
    w1 = jnp.concatenate([w1_ref[k] for k in range(EXP_STEP)], axis=1).astype(BF16)
    w3 = jnp.concatenate([w3_ref[k] for k in range(EXP_STEP)], axis=1).astype(BF16)
    w2 = w2_ref[...].reshape(EXP_STEP * D_EXPERT, D_MODEL).astype(BF16)
    h1 = jnp.dot(t, w1, preferred_element_type=F32)
    h3 = jnp.dot(t, w3, preferred_element_type=F32)
    gate = gate_ref[...]
    lane = lax.broadcasted_iota(jnp.int32, gate.shape, 1)
    hcol = lax.broadcasted_iota(jnp.int32, h1.shape, 1) // D_EXPERT
    gmul = jnp.zeros(h1.shape, F32)
    for k in range(EXP_STEP):
        gcol = jnp.sum(jnp.where(lane == e * EXP_STEP + k + N_EGROUPS, gate, 0.0),
                       axis=-1, keepdims=True)
        gmul = jnp.where(hcol == k, gcol, gmul)
    hid = (h1 * _sigmoid(h1) * h3 * gmul).astype(BF16)
    acc_ref[...] += jnp.dot(hid, w2, preferred_element_type=F32)

    @pl.when(e == N_EXPERTS // EXP_STEP - 1)
    def _():
        o_ref[...] = _ln(ALPHA * x_ref[...] + mod_ref[5] * acc_ref[...], lg_ref[...], lb_ref[...])


def _moe(l, x, mods, router_w, router_b, w1, w3, w2, ln_g, ln_b):
    n_m = T_ALL // TM_MOE
    bps = DEC_SEQ // TM_MOE
    npb = T_PROMPT // TM_MOE
    return pl.pallas_call(
        _moe_kernel,
        name="moe",
        grid=(n_m, N_EXPERTS // EXP_STEP),
        in_specs=[
            pl.BlockSpec((TM_MOE, D_MODEL), lambda i, e: (i, 0)),
            pl.BlockSpec((None, 6, None, 1, D_MODEL),
                         lambda i, e: (l, 0, _mod_row(i, bps, npb), 0, 0)),
            pl.BlockSpec((None, 2, D_MODEL, ROUTER_LANES), lambda i, e: (l, 0, 0, 0)),
            pl.BlockSpec((None, 1, ROUTER_LANES), lambda i, e: (l, 0, 0)),
            pl.BlockSpec((None, EXP_STEP, D_MODEL, D_EXPERT), lambda i, e: (l, e, 0, 0)),
            pl.BlockSpec((None, EXP_STEP, D_MODEL, D_EXPERT), lambda i, e: (l, e, 0, 0)),
            pl.BlockSpec((None, EXP_STEP, D_EXPERT, D_MODEL), lambda i, e: (l, e, 0, 0)),
            pl.BlockSpec((None, 1, D_MODEL), lambda i, e: (l, 0, 0)),
            pl.BlockSpec((None, 1, D_MODEL), lambda i, e: (l, 0, 0)),
        ],
        out_specs=pl.BlockSpec((TM_MOE, D_MODEL), lambda i, e: (i, 0)),
        out_shape=jax.ShapeDtypeStruct((T_ALL, D_MODEL), F32),
        scratch_shapes=[pltpu.VMEM((TM_MOE, D_MODEL), BF16),
                        pltpu.VMEM((TM_MOE, ROUTER_LANES), F32),
                        pltpu.VMEM((TM_MOE, D_MODEL), F32)],
        compiler_params=_cparams(("arbitrary", "arbitrary")),
    )(x, mods, router_w, router_b, w1, w3, w2, ln_g, ln_b)


def _rope_tables():
    t = np.arange(DEC_SEQ)
    pos = np.stack([t // GRID_W, t % GRID_W], axis=1).astype(np.float32)
    quarter = HEAD_DIM // 4
    d = np.arange(HEAD_DIM)
    axis = d // (HEAD_DIM // 2)
    freq = d % quarter
    upper = (d % (HEAD_DIM // 2)) >= quarter
    inv = jnp.asarray(ROPE_BASE, F32) ** (-jnp.arange(0, HEAD_DIM // 2, 2, dtype=F32) / (HEAD_DIM // 2))
    ang = jnp.asarray(pos)[:, axis] * inv[freq][None, :]
    cos = jnp.cos(ang)
    sin = jnp.sin(ang)
    sin_up = jnp.where(upper[None, :], 0.0, -sin)
    sin_dn = jnp.where(upper[None, :], sin, 0.0)
    tile = lambda a: jnp.tile(a, (1, N_HEADS))
    return tile(cos), tile(sin_up), tile(sin_dn)


def _bias_tables(na_rpb):
    qc = np.arange(GRID_W)[:, None]
    kc = np.arange(GRID_W)[None, :]
    start = np.clip(qc - WIN_W // 2, 0, GRID_W - WIN_W)
    valid = (kc >= start) & (kc < start + WIN_W)
    pad_lo = GRID_W - WIN_W
    span = 2 * GRID_W - 1
    lead = na_rpb.shape[:3]
    rr = jnp.pad(na_rpb, ((0, 0), (0, 0), (0, 0), (pad_lo, span - pad_lo - (2 * WIN_W - 1))))
    flat = jnp.broadcast_to(rr[..., None, :], lead + (GRID_W, span)).reshape(lead + (GRID_W * span,))
    toep = flat[..., GRID_W - 1:GRID_W - 1 + GRID_W * (span - 1)]
    toep = toep.reshape(lead + (GRID_W, span - 1))[..., :GRID_W]
    return jnp.where(jnp.asarray(valid), toep, NEG_INF)


def kernel(x_prompt, x_sample, cache_na_k, cache_na_v, c, c_ctx, w_ada, b_ada, w_in, conv_dw,
           conv_b, conv_ln_g, conv_ln_b, conv_pw, na_rpb, na_out, gm_ln_g, gm_ln_b, gm_ws, gm_bs,
           gm_out, w_o, ln1_g, ln1_b, rg_w, rg_b, re_w, re_b, moe_w1, moe_w3, moe_w2, ln2_g, ln2_b):
    x = jnp.concatenate([x_prompt.reshape(T_PROMPT, D_MODEL),
                         x_sample.reshape(T_SAMPLE, D_MODEL)], axis=0)

    cond = jnp.zeros((N_COND, D_MODEL), F32).at[0].set(c_ctx).at[1:1 + DEC_BATCH].set(c)
    mods = _ada(cond, w_ada, b_ada)
    mods = mods.reshape(DEPTH, N_COND, 6, 1, D_MODEL).transpose(0, 2, 1, 3, 4)

    bf = lambda a: a.astype(BF16)
    conv_pw_b, na_out_b, gm_out_b, w_o_b, gm_ws_b = bf(conv_pw), bf(na_out), bf(gm_out), bf(w_o), bf(gm_ws)
    vec = lambda a: a.reshape(DEPTH, 1, a.shape[-1])
    gm_bs_t = gm_bs.transpose(0, 2, 1)
    router_w = jnp.concatenate(
        [rg_w, re_w.transpose(0, 2, 1, 3).reshape(DEPTH, D_MODEL, N_EXPERTS)], axis=-1)
    router_w = jnp.pad(router_w, ((0, 0), (0, 0), (0, ROUTER_LANES - N_EGROUPS - N_EXPERTS)))
    router_hi = router_w.astype(BF16)
    router_lo = (router_w - router_hi.astype(F32)).astype(BF16)
    router_w = jnp.stack([router_hi, router_lo], axis=1)
    router_b = jnp.concatenate([rg_b, re_b.reshape(DEPTH, N_EXPERTS)], axis=-1)
    router_b = jnp.pad(router_b, ((0, 0), (0, ROUTER_LANES - N_EGROUPS - N_EXPERTS)))
    router_b = router_b.reshape(DEPTH, 1, ROUTER_LANES)
    cache_k = cache_na_k.reshape(DEC_BATCH, DEPTH, PAST_LEN, D_NA)
    cache_v = cache_na_v.reshape(DEC_BATCH, DEPTH, PAST_LEN, D_NA)
    tz = _bias_tables(na_rpb)
    cos, sin_up, sin_dn = _rope_tables()

    kt_all = jnp.zeros((BATCH, DEPTH, N_HEADS, HEAD_DIM, SEQ), F32)
    vt_all = jnp.zeros((BATCH, DEPTH, N_HEADS, HEAD_DIM, SEQ), F32)
    for l in range(DEPTH):
        h = _modulate(l, x, mods)
        z = _inproj(l, h, w_in, lambda j: jnp.where(j < W_CB_KV, j, j + N_CB_KV), N_CB_MAIN, BF16)
        zkv = _inproj(l, h, w_in, lambda j: j + W_CB_KV, N_CB_KV, F32)
        yc, ug = _branches(l, z, conv_dw, vec(conv_b), vec(conv_ln_g), vec(conv_ln_b),
                           vec(gm_ln_g), vec(gm_ln_b), gm_ws_b, gm_bs_t)
        att_p, kt_all, vt_all = _ctx_attn(l, z, zkv, kt_all, vt_all)
        att_s = _na_attn(l, z, zkv, cache_k, cache_v, tz, cos, sin_up, sin_dn)
        x = _merge(l, x, mods, z, yc, ug, att_p, att_s, conv_pw_b, na_out_b, gm_out_b, w_o_b,
                   vec(ln1_g), vec(ln1_b))
        x = _moe(l, x, mods, router_w, router_b, moe_w1, moe_w3, moe_w2, vec(ln2_g), vec(ln2_b))

    y_prompt = x[:T_PROMPT].reshape(BATCH, SEQ, D_MODEL)
    y_sample = x[T_PROMPT:].reshape(DEC_BATCH, DEC_SEQ, D_MODEL)
    return (y_prompt, y_sample, kt_all.transpose(0, 1, 4, 2, 3), vt_all.transpose(0, 1, 4, 2, 3))
```

```python
import functools

import jax
import jax.numpy as jnp
import numpy as np
from jax import lax
from jax.experimental import pallas as pl
from jax.experimental.pallas import tpu as pltpu

F32 = jnp.float32
BF16 = jnp.bfloat16
HIGHEST = lax.Precision.HIGHEST

D_MODEL = 1024
BATCH = 16
SEQ = 256
DEPTH = 4
DEC_BATCH = 2
DEC_SEQ = 1024
PAST_LEN = 256
GRID_W = 64
GRID_H = DEC_SEQ // GRID_W
D_CONV = 512
CONV_WIDTH = 31
CONV_HALF = CONV_WIDTH // 2
HEAD_DIM = 64
HEAD_PAIR = 2 * HEAD_DIM
N_HEADS = 8
D_NA = 512
WIN_H = 8
WIN_W = 16
ROPE_BASE = 10000.0
D_GM = 512
GM_CHUNK = 128
GM_GROUPS = 4
D_IN = 6656
N_EGROUPS = 4
EXP_PER_GROUP = 8
N_EXPERTS = 32
D_EXPERT = 128
ALPHA = (2 * DEPTH) ** 0.25
LN_EPS = 1e-5
NEG_INF = -1e30

T_PROMPT = BATCH * SEQ
T_SAMPLE = DEC_BATCH * DEC_SEQ
T_ALL = T_PROMPT + T_SAMPLE
N_COND = 8

COL_BLK = 512
W_CB_KV = 3
N_CB_KV = 2
N_CB_MAIN = D_IN // COL_BLK - N_CB_KV
CB_A, CB_B, CB_Q, CB_GU, CB_GV, CB_GZ = 0, 1, 2, 3, 4, 5
CB_K, CB_V = 0, 1

TB = 256
N_TB = T_ALL // TB
N_TB_PROMPT = T_PROMPT // TB
TB_PER_SAMPLE = DEC_SEQ // TB
HALO = 16
CONV_ROWS = 32
SUBLANES = 8
SHIFT_ROWS = TB + 2 * HALO - SUBLANES

TM_IN = 1024
TM_MOE = 1024
EXP_STEP = 2
ROUTER_LANES = 128
VMEM_LIMIT = 56 * 1024 * 1024


def _ln(x, g, b):
    mu = jnp.mean(x, axis=-1, keepdims=True)
    xc = x - mu
    var = jnp.mean(xc * xc, axis=-1, keepdims=True)
    return xc * lax.rsqrt(var + LN_EPS) * g + b


def _sigmoid(x):
    return jax.nn.sigmoid(x)


def _gelu(x):
    return jax.nn.gelu(x, approximate=True)


_NT = (((1,), (1,)), ((), ()))


def _cparams(sem):
    return pltpu.CompilerParams(dimension_semantics=sem, vmem_limit_bytes=VMEM_LIMIT)


def _ada_kernel(c_ref, w_ref, b_ref, o_ref):
    c = c_ref[...]
    s = c * _sigmoid(c)
    w = w_ref[...]
    s_hi, w_hi = s.astype(BF16), w.astype(BF16)
    s_lo = (s - s_hi.astype(F32)).astype(BF16)
    w_lo = (w - w_hi.astype(F32)).astype(BF16)
    o_ref[...] = (jnp.dot(s_hi, w_hi, preferred_element_type=F32)
                  + jnp.dot(s_lo, w_hi, preferred_element_type=F32)
                  + jnp.dot(s_hi, w_lo, preferred_element_type=F32)) + b_ref[...]


def _ada(cond, w_ada, b_ada):
    tn = 1024
    return pl.pallas_call(
        _ada_kernel,
        name="ada",
        grid=(DEPTH, 6 * D_MODEL // tn),
        in_specs=[
            pl.BlockSpec((N_COND, D_MODEL), lambda l, j: (0, 0)),
            pl.BlockSpec((None, D_MODEL, tn), lambda l, j: (l, 0, j)),
            pl.BlockSpec((None, 1, tn), lambda l, j: (l, 0, j)),
        ],
        out_specs=pl.BlockSpec((None, N_COND, tn), lambda l, j: (l, 0, j)),
        out_shape=jax.ShapeDtypeStruct((DEPTH, N_COND, 6 * D_MODEL), F32),
        compiler_params=_cparams(("arbitrary", "arbitrary")),
    )(cond, w_ada, b_ada.reshape(DEPTH, 1, 6 * D_MODEL))


def _mod_row(i, blocks_per_sample, n_prompt_blocks):
    return jnp.where(i < n_prompt_blocks, 0, 1 + (i - n_prompt_blocks) // blocks_per_sample)


def _modulate_kernel(x_ref, mod_ref, h_ref):
    h_ref[...] = (x_ref[...] * (1.0 + mod_ref[1]) + mod_ref[0]).astype(BF16)


def _modulate(l, x, mods):
    bps = DEC_SEQ // TM_IN
    npb = T_PROMPT // TM_IN
    return pl.pallas_call(
        _modulate_kernel,
        name="modulate",
        grid=(T_ALL // TM_IN,),
        in_specs=[
            pl.BlockSpec((TM_IN, D_MODEL), lambda i: (i, 0)),
            pl.BlockSpec((None, 6, None, 1, D_MODEL),
                         lambda i: (l, 0, _mod_row(i, bps, npb), 0, 0)),
        ],
        out_specs=pl.BlockSpec((TM_IN, D_MODEL), lambda i: (i, 0)),
        out_shape=jax.ShapeDtypeStruct((T_ALL, D_MODEL), BF16),
        compiler_params=_cparams(("arbitrary",)),
    )(x, mods)


def _inproj_kernel(h_ref, w_ref, z_ref):
    rows = pl.ds(pl.multiple_of(pl.program_id(1) * TM_IN, TM_IN), TM_IN)
    z = jnp.dot(h_ref[rows, :], w_ref[...].astype(BF16), preferred_element_type=F32)
    z_ref[...] = z.astype(z_ref.dtype)


def _inproj(l, h, w_in, wcol, n_cols, out_dtype):
    return pl.pallas_call(
        _inproj_kernel,
        name="inproj",
        grid=(n_cols, T_ALL // TM_IN),
        in_specs=[
            pl.BlockSpec((T_ALL, D_MODEL), lambda j, i: (0, 0)),
            pl.BlockSpec((None, D_MODEL, COL_BLK), lambda j, i: (l, 0, wcol(j))),
        ],
        out_specs=pl.BlockSpec((TM_IN, COL_BLK), lambda j, i: (i, j)),
        out_shape=jax.ShapeDtypeStruct((T_ALL, n_cols * COL_BLK), out_dtype),
        compiler_params=_cparams(("arbitrary", "arbitrary")),
    )(h, w_in)


def _branch_kernel(ap_ref, ac_ref, an_ref, bp_ref, bc_ref, bn_ref, gu_ref, gv_ref,
                   dw_ref, cb_ref, clg_ref, clb_ref, glg_ref, glb_ref, ws_ref, bst_ref,
                   yc_ref, ug_ref, ypad_ref, ysh_ref):
    i = pl.program_id(0)
    j = i - N_TB_PROMPT
    in_sample = i >= N_TB_PROMPT
    has_prev = jnp.logical_and(in_sample, j % TB_PER_SAMPLE != 0)
    has_next = jnp.logical_and(in_sample, j % TB_PER_SAMPLE != TB_PER_SAMPLE - 1)

    def glu(a_ref, b_ref):
        return a_ref[...].astype(F32) * _sigmoid(b_ref[...].astype(F32))

    ypad_ref[0:HALO, :] = jnp.where(has_prev, glu(ap_ref, bp_ref), 0.0)
    ypad_ref[HALO:HALO + TB, :] = glu(ac_ref, bc_ref)
    ypad_ref[HALO + TB:HALO + TB + HALO, :] = jnp.where(has_next, glu(an_ref, bn_ref), 0.0)

    for b in range(SUBLANES):
        ysh_ref[b] = ypad_ref[b:b + SHIFT_ROWS, :]

    off = HALO - CONV_HALF
    for c in range(TB // CONV_ROWS):
        base = c * CONV_ROWS
        acc = jnp.zeros((CONV_ROWS, D_CONV), F32)
        for k in range(CONV_WIDTH):
            tile, phase = divmod(off + k, SUBLANES)
            start = base + tile * SUBLANES
            acc = acc + ysh_ref[phase, start:start + CONV_ROWS, :] * dw_ref[k:k + 1, :]
        y = _ln(acc + cb_ref[...], clg_ref[...], clb_ref[...])
        yc_ref[base:base + CONV_ROWS, :] = (y * _sigmoid(y)).astype(BF16)

    for n in range(TB // GM_CHUNK):
        rows = slice(n * GM_CHUNK, (n + 1) * GM_CHUNK)
        u = _gelu(gu_ref[rows, :].astype(F32))
        v = _ln(_gelu(gv_ref[rows, :].astype(F32)), glg_ref[...], glb_ref[...]).astype(BF16)
        for g in range(GM_GROUPS):
            cols = slice(g * GM_CHUNK, (g + 1) * GM_CHUNK)
            sv = jnp.dot(ws_ref[g], v[:, cols], preferred_element_type=F32) + bst_ref[:, g:g + 1]
            ug_ref[rows, cols] = (u[:, cols] * sv).astype(BF16)


def _branches(l, z, conv_dw, conv_b, conv_ln_g, conv_ln_b, gm_ln_g, gm_ln_b, gm_ws, gm_bs_t):
    halo_per_tb = TB // HALO
    n_halo = T_ALL // HALO

    def cur(cb):
        return pl.BlockSpec((TB, COL_BLK), lambda i: (i, cb))

    def prev(cb):
        return pl.BlockSpec((HALO, COL_BLK), lambda i: (jnp.maximum(i * halo_per_tb - 1, 0), cb))

    def nxt(cb):
        return pl.BlockSpec((HALO, COL_BLK),
                            lambda i: (jnp.minimum((i + 1) * halo_per_tb, n_halo - 1), cb))

    def vec(n):
        return pl.BlockSpec((None, 1, n), lambda i: (l, 0, 0))

    return pl.pallas_call(
        _branch_kernel,
        name="branches",
        grid=(N_TB,),
        in_specs=[
            prev(CB_A), cur(CB_A), nxt(CB_A), prev(CB_B), cur(CB_B), nxt(CB_B),
            cur(CB_GU), cur(CB_GV),
            pl.BlockSpec((None, CONV_WIDTH, D_CONV), lambda i: (l, 0, 0)),
            vec(D_CONV), vec(D_CONV), vec(D_CONV), vec(D_GM), vec(D_GM),
            pl.BlockSpec((None, GM_GROUPS, GM_CHUNK, GM_CHUNK), lambda i: (l, 0, 0, 0)),
            pl.BlockSpec((None, GM_CHUNK, GM_GROUPS), lambda i: (l, 0, 0)),
        ],
        out_specs=[pl.BlockSpec((TB, D_CONV), lambda i: (i, 0)),
                   pl.BlockSpec((TB, D_GM), lambda i: (i, 0))],
        out_shape=[jax.ShapeDtypeStruct((T_ALL, D_CONV), BF16),
                   jax.ShapeDtypeStruct((T_ALL, D_GM), BF16)],
        scratch_shapes=[pltpu.VMEM((TB + 2 * HALO, D_CONV), F32),
                        pltpu.VMEM((SUBLANES, SHIFT_ROWS, D_CONV), F32)],
        compiler_params=_cparams(("arbitrary",)),
    )(z, z, z, z, z, z, z, z, conv_dw, conv_b, conv_ln_g, conv_ln_b, gm_ln_g, gm_ln_b,
      gm_ws, gm_bs_t)


def _ctx_attn_kernel(q_ref, k_ref, v_ref, kin_ref, vin_ref, o_ref, ko_ref, vo_ref):
    del kin_ref, vin_ref
    kt = k_ref[...].T
    vt = v_ref[...].T
    ko_ref[...] = kt.reshape(N_HEADS, HEAD_DIM, SEQ)
    vo_ref[...] = vt.reshape(N_HEADS, HEAD_DIM, SEQ)
    q = (q_ref[...].astype(F32) * HEAD_DIM ** -0.5).astype(BF16)
    kb = kt.astype(BF16)
    vb = vt.astype(BF16)
    lower = lax.broadcasted_iota(jnp.int32, (SEQ, HEAD_PAIR), 1) < HEAD_DIM
    for hp in range(N_HEADS // 2):
        grp = slice(hp * HEAD_PAIR, (hp + 1) * HEAD_PAIR)
        q2, k2, v2 = q[:, grp], kb[grp, :], vb[grp, :]
        outs = []
        for mask in (lower, jnp.logical_not(lower)):
            qh = jnp.where(mask, q2, jnp.zeros_like(q2))
            s = jnp.dot(qh, k2, preferred_element_type=F32)
            m = jnp.max(s, axis=-1, keepdims=True)
            p = jnp.exp(s - m)
            den = jnp.sum(p, axis=-1, keepdims=True)
            o = lax.dot_general(p.astype(BF16), v2, _NT, preferred_element_type=F32)
            outs.append(o / den)
        o_ref[:, grp] = jnp.where(lower, outs[0], outs[1]).astype(BF16)


def _ctx_attn(l, z, zkv, kt_all, vt_all):
    cache_blk = pl.BlockSpec((None, None, N_HEADS, HEAD_DIM, SEQ), lambda b: (b, l, 0, 0, 0))
    cache_shape = jax.ShapeDtypeStruct((BATCH, DEPTH, N_HEADS, HEAD_DIM, SEQ), F32)
    return pl.pallas_call(
        _ctx_attn_kernel,
        name="ctx_attn",
        grid=(BATCH,),
        in_specs=[pl.BlockSpec((SEQ, COL_BLK), lambda b: (b, CB_Q)),
                  pl.BlockSpec((SEQ, COL_BLK), lambda b: (b, CB_K)),
                  pl.BlockSpec((SEQ, COL_BLK), lambda b: (b, CB_V)),
                  pl.BlockSpec(memory_space=pl.ANY), pl.BlockSpec(memory_space=pl.ANY)],
        out_specs=[pl.BlockSpec((SEQ, D_NA), lambda b: (b, 0)), cache_blk, cache_blk],
        out_shape=[jax.ShapeDtypeStruct((T_PROMPT, D_NA), BF16), cache_shape, cache_shape],
        input_output_aliases={3: 1, 4: 2},
        compiler_params=_cparams(("arbitrary",)),
    )(z, zkv, zkv, kt_all, vt_all)


def _rope(x, cos, sin_up, sin_dn):
    return (x * cos + pltpu.roll(x, D_NA - HEAD_DIM // 4, 1) * sin_up
            + pltpu.roll(x, HEAD_DIM // 4, 1) * sin_dn)


def _na_attn_kernel(q_ref, k_ref, v_ref, ck_ref, cv_ref, tz_ref, cos_ref, sup_ref, sdn_ref,
                    o_ref, krot_ref, vb_ref, ckb_ref, cvb_ref):
    r = pl.program_id(1)

    @pl.when(r == 0)
    def _():
        krot_ref[...] = _rope(k_ref[...], cos_ref[...], sup_ref[...], sdn_ref[...]).astype(BF16)
        vb_ref[...] = v_ref[...].astype(BF16)
        ckb_ref[...] = ck_ref[...].astype(BF16)
        cvb_ref[...] = cv_ref[...].astype(BF16)

    qrows = pl.ds(pl.multiple_of(r * GRID_W, GRID_W), GRID_W)
    q = _rope(q_ref[...].astype(F32), cos_ref[qrows, :], sup_ref[qrows, :], sdn_ref[qrows, :])
    q = (q * HEAD_DIM ** -0.5).astype(BF16)

    row_start = jnp.clip(r - WIN_H // 2, 0, GRID_H - WIN_H)
    krows = pl.ds(pl.multiple_of(row_start * GRID_W, GRID_W), WIN_H * GRID_W)
    kwin = krot_ref[krows, :]
    vwin = vb_ref[krows, :]
    dr0 = row_start - r + WIN_H - 1
    lower = lax.broadcasted_iota(jnp.int32, (GRID_W, HEAD_PAIR), 1) < HEAD_DIM
    heads = range(N_HEADS)
    grp = [slice(h // 2 * HEAD_PAIR, (h // 2 + 1) * HEAD_PAIR) for h in heads]
    mask = [lower if h % 2 == 0 else jnp.logical_not(lower) for h in heads]
    qh = [jnp.where(mask[h], q[:, grp[h]], jnp.zeros((GRID_W, HEAD_PAIR), BF16)) for h in heads]
    s_loc = [lax.dot_general(qh[h], kwin[:, grp[h]], _NT, preferred_element_type=F32)
             + jnp.concatenate([tz_ref[h, dr0 + w] for w in range(0, WIN_H, 2)], axis=1)
             for h in heads]
    s_ctx = [lax.dot_general(qh[h], ckb_ref[:, grp[h]], _NT, preferred_element_type=F32)
             for h in heads]
    m = [jnp.maximum(jnp.max(s_loc[h], axis=-1, keepdims=True),
                     jnp.max(s_ctx[h], axis=-1, keepdims=True)) for h in heads]
    p_loc = [jnp.exp(s_loc[h] - m[h]) for h in heads]
    p_ctx = [jnp.exp(s_ctx[h] - m[h]) for h in heads]
    den = [jnp.sum(p_loc[h], axis=-1, keepdims=True) + jnp.sum(p_ctx[h], axis=-1, keepdims=True)
           for h in heads]
    o = [(jnp.dot(p_loc[h].astype(BF16), vwin[:, grp[h]], preferred_element_type=F32)
          + jnp.dot(p_ctx[h].astype(BF16), cvb_ref[:, grp[h]], preferred_element_type=F32)) / den[h]
         for h in heads]
    for h in range(0, N_HEADS, 2):
        o_ref[:, grp[h]] = jnp.where(lower, o[h], o[h + 1]).astype(BF16)


def _na_attn(l, z, zkv, cache_k, cache_v, tz, cos, sin_up, sin_dn):
    seq_blk0 = T_PROMPT // DEC_SEQ
    row_blk0 = T_PROMPT // GRID_W
    full = pl.BlockSpec((DEC_SEQ, D_NA), lambda b, r: (0, 0))
    return pl.pallas_call(
        _na_attn_kernel,
        name="na_attn",
        grid=(DEC_BATCH, GRID_H),
        in_specs=[
            pl.BlockSpec((GRID_W, COL_BLK), lambda b, r: (row_blk0 + b * GRID_H + r, CB_Q)),
            pl.BlockSpec((DEC_SEQ, COL_BLK), lambda b, r: (seq_blk0 + b, CB_K)),
            pl.BlockSpec((DEC_SEQ, COL_BLK), lambda b, r: (seq_blk0 + b, CB_V)),
            pl.BlockSpec((None, None, PAST_LEN, D_NA), lambda b, r: (b, l, 0, 0)),
            pl.BlockSpec((None, None, PAST_LEN, D_NA), lambda b, r: (b, l, 0, 0)),
            pl.BlockSpec((None, N_HEADS, 2 * WIN_H - 2, GRID_W, 2 * GRID_W),
                         lambda b, r: (l, 0, 0, 0, 0)),
            full, full, full,
        ],
        out_specs=pl.BlockSpec((GRID_W, D_NA), lambda b, r: (b * GRID_H + r, 0)),
        out_shape=jax.ShapeDtypeStruct((T_SAMPLE, D_NA), BF16),
        scratch_shapes=[pltpu.VMEM((DEC_SEQ, D_NA), BF16), pltpu.VMEM((DEC_SEQ, D_NA), BF16),
                        pltpu.VMEM((PAST_LEN, D_NA), BF16), pltpu.VMEM((PAST_LEN, D_NA), BF16)],
        compiler_params=_cparams(("arbitrary", "arbitrary")),
    )(z, zkv, zkv, cache_k, cache_v, tz, cos, sin_up, sin_dn)


def _merge_kernel(x_ref, mod_ref, g0a, g0b, g1a, g1b, g2a, g2b, yc_ref, ug_ref, ap_ref, as_ref,
                  pw_ref, no_ref, go_ref, wo_ref, lg_ref, lb_ref, o_ref):
    i = pl.program_id(0)
    att = jnp.where(i < N_TB_PROMPT, ap_ref[...], as_ref[...])
    br_c = jnp.dot(yc_ref[...], pw_ref[...], preferred_element_type=F32)
    br_a = jnp.dot(att, no_ref[...], preferred_element_type=F32)
    br_g = jnp.dot(ug_ref[...], go_ref[...], preferred_element_type=F32)
    h = COL_BLK
    for lo, ga, gb, gc in ((0, g0a, g1a, g2a), (h, g0b, g1b, g2b)):
        cols = slice(lo, lo + h)
        o_ref[:, cols] = (_sigmoid(ga[...].astype(F32)) * br_c[:, cols]
                          + _sigmoid(gb[...].astype(F32)) * br_a[:, cols]
                          + _sigmoid(gc[...].astype(F32)) * br_g[:, cols])
    mix = jnp.dot(o_ref[...].astype(BF16), wo_ref[...], preferred_element_type=F32)
    o_ref[...] = _ln(ALPHA * x_ref[...] + mod_ref[2] * mix, lg_ref[...], lb_ref[...])


def _merge(l, x, mods, z, yc, ug, att_p, att_s, conv_pw, na_out, gm_out, w_o, ln_g, ln_b):
    def gz(k):
        return pl.BlockSpec((TB, COL_BLK), lambda i: (i, CB_GZ + k))

    def w(k, n):
        return pl.BlockSpec((None, k, n), lambda i: (l, 0, 0))

    blk512 = pl.BlockSpec((TB, COL_BLK), lambda i: (i, 0))
    return pl.pallas_call(
        _merge_kernel,
        name="merge",
        grid=(N_TB,),
        in_specs=[
            pl.BlockSpec((TB, D_MODEL), lambda i: (i, 0)),
            pl.BlockSpec((None, 6, None, 1, D_MODEL),
                         lambda i: (l, 0, _mod_row(i, TB_PER_SAMPLE, N_TB_PROMPT), 0, 0)),
            gz(0), gz(1), gz(2), gz(3), gz(4), gz(5),
            blk512, blk512,
            pl.BlockSpec((TB, D_NA), lambda i: (jnp.minimum(i, N_TB_PROMPT - 1), 0)),
            pl.BlockSpec((TB, D_NA), lambda i: (jnp.maximum(i - N_TB_PROMPT, 0), 0)),
            w(D_CONV, D_MODEL), w(D_NA, D_MODEL), w(D_GM, D_MODEL), w(D_MODEL, D_MODEL),
            w(1, D_MODEL), w(1, D_MODEL),
        ],
        out_specs=pl.BlockSpec((TB, D_MODEL), lambda i: (i, 0)),
        out_shape=jax.ShapeDtypeStruct((T_ALL, D_MODEL), F32),
        compiler_params=_cparams(("arbitrary",)),
    )(x, mods, z, z, z, z, z, z, yc, ug, att_p, att_s, conv_pw, na_out, gm_out, w_o, ln_g, ln_b)


def _route(logits):
    lane = lax.broadcasted_iota(jnp.int32, logits.shape, 1)
    big = jnp.int32(ROUTER_LANES)
    is_g = lane < N_EGROUPS
    gl = jnp.where(is_g, logits, -jnp.inf)
    gmax = jnp.max(gl, axis=-1, keepdims=True)
    gidx = jnp.min(jnp.where(gl == gmax, lane, big), axis=-1, keepdims=True)
    gp = 1.0 / jnp.sum(jnp.where(is_g, jnp.exp(gl - gmax), 0.0), axis=-1, keepdims=True)
    lo = N_EGROUPS + gidx * EXP_PER_GROUP
    el = jnp.where(jnp.logical_and(lane >= lo, lane < lo + EXP_PER_GROUP), logits, -jnp.inf)
    v1 = jnp.max(el, axis=-1, keepdims=True)
    i1 = jnp.min(jnp.where(el == v1, lane, big), axis=-1, keepdims=True)
    el2 = jnp.where(lane == i1, -jnp.inf, el)
    v2 = jnp.max(el2, axis=-1, keepdims=True)
    i2 = jnp.min(jnp.where(el2 == v2, lane, big), axis=-1, keepdims=True)
    e2 = jnp.exp(v2 - v1)
    w1 = gp / (1.0 + e2)
    w2 = gp * e2 / (1.0 + e2)
    return jnp.where(lane == i1, w1, 0.0) + jnp.where(lane == i2, w2, 0.0)


def _moe_kernel(x_ref, mod_ref, rw_ref, rb_ref, w1_ref, w3_ref, w2_ref, lg_ref, lb_ref,
                o_ref, t_ref, gate_ref, acc_ref):
    e = pl.program_id(1)

    @pl.when(e == 0)
    def _():
        t = x_ref[...] * (1.0 + mod_ref[4]) + mod_ref[3]
        t_hi = t.astype(BF16)
        t_ref[...] = t_hi
        t_lo = (t - t_hi.astype(F32)).astype(BF16)
        logits = (jnp.dot(t_hi, rw_ref[0], preferred_element_type=F32)
                  + jnp.dot(t_lo, rw_ref[0], preferred_element_type=F32)
                  + jnp.dot(t_hi, rw_ref[1], preferred_element_type=F32))
        gate_ref[...] = _route(logits + rb_ref[...])
        acc_ref[...] = jnp.zeros_like(acc_ref)

    t = t_ref[...]
    w1 = jnp.concatenate([w1_ref[k] for k in range(EXP_STEP)], axis=1).astype(BF16)
    w3 = jnp.concatenate([w3_ref[k] for k in range(EXP_STEP)], axis=1).astype(BF16)
    w2 = w2_ref[...].reshape(EXP_STEP * D_EXPERT, D_MODEL).astype(BF16)
    h1 = jnp.dot(t, w1, preferred_element_type=F32)
    h3 = jnp.dot(t, w3, preferred_element_type=F32)
    gate = gate_ref[...]
    lane = lax.broadcasted_iota(jnp.int32, gate.shape, 1)
    hcol = lax.broadcasted_iota(jnp.int32, h1.shape, 1) // D_EXPERT
    gmul = jnp.zeros(h1.shape, F32)
    for k in range(EXP_STEP):
        gcol = jnp.sum(jnp.where(lane == e * EXP_STEP + k + N_EGROUPS, gate, 0.0),
                       axis=-1, keepdims=True)
        gmul = jnp.where(hcol == k, gcol, gmul)
    hid = (h1 * _sigmoid(h1) * h3 * gmul).astype(BF16)
    acc_ref[...] += jnp.dot(hid, w2, preferred_element_type=F32)

    @pl.when(e == N_EXPERTS // EXP_STEP - 1)
    def _():
        o_ref[...] = _ln(ALPHA * x_ref[...] + mod_ref[5] * acc_ref[...], lg_ref[...], lb_ref[...])


def _moe(l, x, mods, router_w, router_b, w1, w3, w2, ln_g, ln_b):
    n_m = T_ALL // TM_MOE
    bps = DEC_SEQ // TM_MOE
    npb = T_PROMPT // TM_MOE
    return pl.pallas_call(
        _moe_kernel,
        name="moe",
        grid=(n_m, N_EXPERTS // EXP_STEP),
        in_specs=[
            pl.BlockSpec((TM_MOE, D_MODEL), lambda i, e: (i, 0)),
            pl.BlockSpec((None, 6, None, 1, D_MODEL),
                         lambda i, e: (l, 0, _mod_row(i, bps, npb), 0, 0)),
            pl.BlockSpec((None, 2, D_MODEL, ROUTER_LANES), lambda i, e: (l, 0, 0, 0)),
            pl.BlockSpec((None, 1, ROUTER_LANES), lambda i, e: (l, 0, 0)),
            pl.BlockSpec((None, EXP_STEP, D_MODEL, D_EXPERT), lambda i, e: (l, e, 0, 0)),
            pl.BlockSpec((None, EXP_STEP, D_MODEL, D_EXPERT), lambda i, e: (l, e, 0, 0)),
            pl.BlockSpec((None, EXP_STEP, D_EXPERT, D_MODEL), lambda i, e: (l, e, 0, 0)),
            pl.BlockSpec((None, 1, D_MODEL), lambda i, e: (l, 0, 0)),
            pl.BlockSpec((None, 1, D_MODEL), lambda i, e: (l, 0, 0)),
        ],
        out_specs=pl.BlockSpec((TM_MOE, D_MODEL), lambda i, e: (i, 0)),
        out_shape=jax.ShapeDtypeStruct((T_ALL, D_MODEL), F32),
        scratch_shapes=[pltpu.VMEM((TM_MOE, D_MODEL), BF16),
                        pltpu.VMEM((TM_MOE, ROUTER_LANES), F32),
                        pltpu.VMEM((TM_MOE, D_MODEL), F32)],
        compiler_params=_cparams(("arbitrary", "arbitrary")),
    )(x, mods, router_w, router_b, w1, w3, w2, ln_g, ln_b)


def _rope_tables():
    t = np.arange(DEC_SEQ)
    pos = np.stack([t // GRID_W, t % GRID_W], axis=1).astype(np.float32)
    quarter = HEAD_DIM // 4
    d = np.arange(HEAD_DIM)
    axis = d // (HEAD_DIM // 2)
    freq = d % quarter
    upper = (d % (HEAD_DIM // 2)) >= quarter
    inv = jnp.asarray(ROPE_BASE, F32) ** (-jnp.arange(0, HEAD_DIM // 2, 2, dtype=F32) / (HEAD_DIM // 2))
    ang = jnp.asarray(pos)[:, axis] * inv[freq][None, :]
    cos = jnp.cos(ang)
    sin = jnp.sin(ang)
    sin_up = jnp.where(upper[None, :], 0.0, -sin)
    sin_dn = jnp.where(upper[None, :], sin, 0.0)
    tile = lambda a: jnp.tile(a, (1, N_HEADS))
    return tile(cos), tile(sin_up), tile(sin_dn)


def _bias_tables(na_rpb):
    qc = np.arange(GRID_W)[:, None]
    kc = np.arange(GRID_W)[None, :]
    start = np.clip(qc - WIN_W // 2, 0, GRID_W - WIN_W)
    valid = (kc >= start) & (kc < start + WIN_W)
    dc = kc - qc + WIN_W - 1
    sel = (dc[None] == np.arange(2 * WIN_W - 1)[:, None, None]) & valid[None]
    toep = jnp.einsum('lhrd,dqk->lhrqk', na_rpb, jnp.asarray(sel, F32), precision=HIGHEST)
    tz = jnp.where(jnp.asarray(valid), toep, NEG_INF)
    return jnp.concatenate([tz[:, :, :-1], tz[:, :, 1:]], axis=-1)


def kernel(x_prompt, x_sample, cache_na_k, cache_na_v, c, c_ctx, w_ada, b_ada, w_in, conv_dw,
           conv_b, conv_ln_g, conv_ln_b, conv_pw, na_rpb, na_out, gm_ln_g, gm_ln_b, gm_ws, gm_bs,
           gm_out, w_o, ln1_g, ln1_b, rg_w, rg_b, re_w, re_b, moe_w1, moe_w3, moe_w2, ln2_g, ln2_b):
    x = jnp.concatenate([x_prompt.reshape(T_PROMPT, D_MODEL),
                         x_sample.reshape(T_SAMPLE, D_MODEL)], axis=0)

    cond = jnp.zeros((N_COND, D_MODEL), F32).at[0].set(c_ctx).at[1:1 + DEC_BATCH].set(c)
    mods = _ada(cond, w_ada, b_ada)
    mods = mods.reshape(DEPTH, N_COND, 6, 1, D_MODEL).transpose(0, 2, 1, 3, 4)

    bf = lambda a: a.astype(BF16)
    conv_pw_b, na_out_b, gm_out_b, w_o_b, gm_ws_b = bf(conv_pw), bf(na_out), bf(gm_out), bf(w_o), bf(gm_ws)
    vec = lambda a: a.reshape(DEPTH, 1, a.shape[-1])
    gm_bs_t = gm_bs.transpose(0, 2, 1)
    router_w = jnp.concatenate(
        [rg_w, re_w.transpose(0, 2, 1, 3).reshape(DEPTH, D_MODEL, N_EXPERTS)], axis=-1)
    router_w = jnp.pad(router_w, ((0, 0), (0, 0), (0, ROUTER_LANES - N_EGROUPS - N_EXPERTS)))
    router_hi = router_w.astype(BF16)
    router_lo = (router_w - router_hi.astype(F32)).astype(BF16)
    router_w = jnp.stack([router_hi, router_lo], axis=1)
    router_b = jnp.concatenate([rg_b, re_b.reshape(DEPTH, N_EXPERTS)], axis=-1)
    router_b = jnp.pad(router_b, ((0, 0), (0, ROUTER_LANES - N_EGROUPS - N_EXPERTS)))
    router_b = router_b.reshape(DEPTH, 1, ROUTER_LANES)
    cache_k = cache_na_k.reshape(DEC_BATCH, DEPTH, PAST_LEN, D_NA)
    cache_v = cache_na_v.reshape(DEC_BATCH, DEPTH, PAST_LEN, D_NA)
    tz = _bias_tables(na_rpb)
    cos, sin_up, sin_dn = _rope_tables()

    kt_all = jnp.zeros((BATCH, DEPTH, N_HEADS, HEAD_DIM, SEQ), F32)
    vt_all = jnp.zeros((BATCH, DEPTH, N_HEADS, HEAD_DIM, SEQ), F32)
    for l in range(DEPTH):
        h = _modulate(l, x, mods)
        z = _inproj(l, h, w_in, lambda j: jnp.where(j < W_CB_KV, j, j + N_CB_KV), N_CB_MAIN, BF16)
        zkv = _inproj(l, h, w_in, lambda j: j + W_CB_KV, N_CB_KV, F32)
        yc, ug = _branches(l, z, conv_dw, vec(conv_b), vec(conv_ln_g), vec(conv_ln_b),
                           vec(gm_ln_g), vec(gm_ln_b), gm_ws_b, gm_bs_t)
        att_p, kt_all, vt_all = _ctx_attn(l, z, zkv, kt_all, vt_all)
        att_s = _na_attn(l, z, zkv, cache_k, cache_v, tz, cos, sin_up, sin_dn)
        x = _merge(l, x, mods, z, yc, ug, att_p, att_s, conv_pw_b, na_out_b, gm_out_b, w_o_b,
                   vec(ln1_g), vec(ln1_b))
        x = _moe(l, x, mods, router_w, router_b, moe_w1, moe_w3, moe_w2, vec(ln2_g), vec(ln2_b))

    y_prompt = x[:T_PROMPT].reshape(BATCH, SEQ, D_MODEL)
    y_sample = x[T_PROMPT:].reshape(DEC_BATCH, DEC_SEQ, D_MODEL)
    return (y_prompt, y_sample, kt_all.transpose(0, 1, 4, 2, 3), vt_all.transpose(0, 1, 4, 2, 3))
```

```python
import functools

import jax
import jax.numpy as jnp
import numpy as np
from jax import lax
from jax.experimental import pallas as pl
from jax.experimental.pallas import tpu as pltpu

F32 = jnp.float32
BF16 = jnp.bfloat16
HIGHEST = lax.Precision.HIGHEST

D_MODEL = 1024
BATCH = 16
SEQ = 256
DEPTH = 4
DEC_BATCH = 2
DEC_SEQ = 1024
PAST_LEN = 256
GRID_W = 64
GRID_H = DEC_SEQ // GRID_W
D_CONV = 512
CONV_WIDTH = 31
CONV_HALF = CONV_WIDTH // 2
HEAD_DIM = 64
HEAD_PAIR = 2 * HEAD_DIM
N_HEADS = 8
D_NA = 512
WIN_H = 8
WIN_W = 16
ROPE_BASE = 10000.0
D_GM = 512
GM_CHUNK = 128
GM_GROUPS = 4
D_IN = 6656
N_EGROUPS = 4
EXP_PER_GROUP = 8
N_EXPERTS = 32
D_EXPERT = 128
ALPHA = (2 * DEPTH) ** 0.25
LN_EPS = 1e-5
NEG_INF = -1e30

T_PROMPT = BATCH * SEQ
T_SAMPLE = DEC_BATCH * DEC_SEQ
T_ALL = T_PROMPT + T_SAMPLE
N_COND = 8

COL_BLK = 512
W_CB_KV = 3
N_CB_KV = 2
N_CB_MAIN = D_IN // COL_BLK - N_CB_KV
CB_A, CB_B, CB_Q, CB_GU, CB_GV, CB_GZ = 0, 1, 2, 3, 4, 5
CB_K, CB_V = 0, 1

TB = 256
N_TB = T_ALL // TB
N_TB_PROMPT = T_PROMPT // TB
TB_PER_SAMPLE = DEC_SEQ // TB
TBM = 512
N_TBM = T_ALL // TBM
N_TBM_PROMPT = T_PROMPT // TBM
TBM_PER_SAMPLE = DEC_SEQ // TBM
HALO = 16
CONV_ROWS = 32
SUBLANES = 8
SHIFT_ROWS = TB + 2 * HALO - SUBLANES

TM_IN = 2048
TM_MOE = 1024
EXP_STEP = 4
ROUTER_LANES = 128
VMEM_LIMIT = 56 * 1024 * 1024


def _ln(x, g, b):
    mu = jnp.mean(x, axis=-1, keepdims=True)
    xc = x - mu
    var = jnp.mean(xc * xc, axis=-1, keepdims=True)
    return xc * lax.rsqrt(var + LN_EPS) * g + b


def _sigmoid(x):
    return jax.nn.sigmoid(x)


def _gelu(x):
    return jax.nn.gelu(x, approximate=True)


_NT = (((1,), (1,)), ((), ()))


def _cparams(sem):
    return pltpu.CompilerParams(dimension_semantics=sem, vmem_limit_bytes=VMEM_LIMIT)


def _ada_kernel(c_ref, w_ref, b_ref, o_ref):
    c = c_ref[...]
    s = c * _sigmoid(c)
    w = w_ref[...]
    s_hi, w_hi = s.astype(BF16), w.astype(BF16)
    s_lo = (s - s_hi.astype(F32)).astype(BF16)
    w_lo = (w - w_hi.astype(F32)).astype(BF16)
    o_ref[...] = (jnp.dot(s_hi, w_hi, preferred_element_type=F32)
                  + jnp.dot(s_lo, w_hi, preferred_element_type=F32)
                  + jnp.dot(s_hi, w_lo, preferred_element_type=F32)) + b_ref[...]


def _ada(cond, w_ada, b_ada):
    tn = 2048
    return pl.pallas_call(
        _ada_kernel,
        name="ada",
        grid=(DEPTH, 6 * D_MODEL // tn),
        in_specs=[
            pl.BlockSpec((N_COND, D_MODEL), lambda l, j: (0, 0)),
            pl.BlockSpec((None, D_MODEL, tn), lambda l, j: (l, 0, j)),
            pl.BlockSpec((None, 1, tn), lambda l, j: (l, 0, j)),
        ],
        out_specs=pl.BlockSpec((None, N_COND, tn), lambda l, j: (l, 0, j)),
        out_shape=jax.ShapeDtypeStruct((DEPTH, N_COND, 6 * D_MODEL), F32),
        compiler_params=_cparams(("arbitrary", "arbitrary")),
    )(cond, w_ada, b_ada.reshape(DEPTH, 1, 6 * D_MODEL))


def _mod_row(i, blocks_per_sample, n_prompt_blocks):
    return jnp.where(i < n_prompt_blocks, 0, 1 + (i - n_prompt_blocks) // blocks_per_sample)


def _modulate_kernel(x_ref, mod_ref, h_ref):
    h_ref[...] = (x_ref[...] * (1.0 + mod_ref[1]) + mod_ref[0]).astype(BF16)


def _modulate(l, x, mods):
    bps = DEC_SEQ // TM_MOE
    npb = T_PROMPT // TM_MOE
    return pl.pallas_call(
        _modulate_kernel,
        name="modulate",
        grid=(T_ALL // TM_MOE,),
        in_specs=[
            pl.BlockSpec((TM_MOE, D_MODEL), lambda i: (i, 0)),
            pl.BlockSpec((None, 6, None, 1, D_MODEL),
                         lambda i: (l, 0, _mod_row(i, bps, npb), 0, 0)),
        ],
        out_specs=pl.BlockSpec((TM_MOE, D_MODEL), lambda i: (i, 0)),
        out_shape=jax.ShapeDtypeStruct((T_ALL, D_MODEL), BF16),
        compiler_params=_cparams(("arbitrary",)),
    )(x, mods)


def _inproj_kernel(h_ref, w_ref, z_ref):
    rows = pl.ds(pl.multiple_of(pl.program_id(1) * TM_IN, TM_IN), TM_IN)
    z = jnp.dot(h_ref[rows, :], w_ref[...].astype(BF16), preferred_element_type=F32)
    z_ref[...] = z.astype(z_ref.dtype)


def _inproj(l, h, w_in, wcol, n_cols, out_dtype):
    return pl.pallas_call(
        _inproj_kernel,
        name="inproj",
        grid=(n_cols, T_ALL // TM_IN),
        in_specs=[
            pl.BlockSpec((T_ALL, D_MODEL), lambda j, i: (0, 0)),
            pl.BlockSpec((None, D_MODEL, COL_BLK), lambda j, i: (l, 0, wcol(j))),
        ],
        out_specs=pl.BlockSpec((TM_IN, COL_BLK), lambda j, i: (i, j)),
        out_shape=jax.ShapeDtypeStruct((T_ALL, n_cols * COL_BLK), out_dtype),
        compiler_params=_cparams(("arbitrary", "arbitrary")),
    )(h, w_in)


def _branch_kernel(ap_ref, ac_ref, an_ref, bp_ref, bc_ref, bn_ref, gu_ref, gv_ref,
                   dw_ref, cb_ref, clg_ref, clb_ref, glg_ref, glb_ref, ws_ref, bst_ref,
                   yc_ref, ug_ref, ypad_ref, ysh_ref):
    i = pl.program_id(0)
    j = i - N_TB_PROMPT
    in_sample = i >= N_TB_PROMPT
    has_prev = jnp.logical_and(in_sample, j % TB_PER_SAMPLE != 0)
    has_next = jnp.logical_and(in_sample, j % TB_PER_SAMPLE != TB_PER_SAMPLE - 1)

    def glu(a_ref, b_ref):
        return a_ref[...].astype(F32) * _sigmoid(b_ref[...].astype(F32))

    ypad_ref[0:HALO, :] = jnp.where(has_prev, glu(ap_ref, bp_ref), 0.0)
    ypad_ref[HALO:HALO + TB, :] = glu(ac_ref, bc_ref)
    ypad_ref[HALO + TB:HALO + TB + HALO, :] = jnp.where(has_next, glu(an_ref, bn_ref), 0.0)

    for b in range(SUBLANES):
        ysh_ref[b] = ypad_ref[b:b + SHIFT_ROWS, :]

    off = HALO - CONV_HALF
    for c in range(TB // CONV_ROWS):
        base = c * CONV_ROWS
        acc = jnp.zeros((CONV_ROWS, D_CONV), F32)
        for k in range(CONV_WIDTH):
            tile, phase = divmod(off + k, SUBLANES)
            start = base + tile * SUBLANES
            acc = acc + ysh_ref[phase, start:start + CONV_ROWS, :] * dw_ref[k:k + 1, :]
        y = _ln(acc + cb_ref[...], clg_ref[...], clb_ref[...])
        yc_ref[base:base + CONV_ROWS, :] = (y * _sigmoid(y)).astype(BF16)

    for n in range(TB // GM_CHUNK):
        rows = slice(n * GM_CHUNK, (n + 1) * GM_CHUNK)
        u = _gelu(gu_ref[rows, :].astype(F32))
        v = _ln(_gelu(gv_ref[rows, :].astype(F32)), glg_ref[...], glb_ref[...]).astype(BF16)
        for g in range(GM_GROUPS):
            cols = slice(g * GM_CHUNK, (g + 1) * GM_CHUNK)
            sv = jnp.dot(ws_ref[g], v[:, cols], preferred_element_type=F32) + bst_ref[:, g:g + 1]
            ug_ref[rows, cols] = (u[:, cols] * sv).astype(BF16)


def _branches(l, z, conv_dw, conv_b, conv_ln_g, conv_ln_b, gm_ln_g, gm_ln_b, gm_ws, gm_bs_t):
    halo_per_tb = TB // HALO
    n_halo = T_ALL // HALO

    def cur(cb):
        return pl.BlockSpec((TB, COL_BLK), lambda i: (i, cb))

    def prev(cb):
        return pl.BlockSpec((HALO, COL_BLK), lambda i: (jnp.maximum(i * halo_per_tb - 1, 0), cb))

    def nxt(cb):
        return pl.BlockSpec((HALO, COL_BLK),
                            lambda i: (jnp.minimum((i + 1) * halo_per_tb, n_halo - 1), cb))

    def vec(n):
        return pl.BlockSpec((None, 1, n), lambda i: (l, 0, 0))

    return pl.pallas_call(
        _branch_kernel,
        name="branches",
        grid=(N_TB,),
        in_specs=[
            prev(CB_A), cur(CB_A), nxt(CB_A), prev(CB_B), cur(CB_B), nxt(CB_B),
            cur(CB_GU), cur(CB_GV),
            pl.BlockSpec((None, CONV_WIDTH, D_CONV), lambda i: (l, 0, 0)),
            vec(D_CONV), vec(D_CONV), vec(D_CONV), vec(D_GM), vec(D_GM),
            pl.BlockSpec((None, GM_GROUPS, GM_CHUNK, GM_CHUNK), lambda i: (l, 0, 0, 0)),
            pl.BlockSpec((None, GM_CHUNK, GM_GROUPS), lambda i: (l, 0, 0)),
        ],
        out_specs=[pl.BlockSpec((TB, D_CONV), lambda i: (i, 0)),
                   pl.BlockSpec((TB, D_GM), lambda i: (i, 0))],
        out_shape=[jax.ShapeDtypeStruct((T_ALL, D_CONV), BF16),
                   jax.ShapeDtypeStruct((T_ALL, D_GM), BF16)],
        scratch_shapes=[pltpu.VMEM((TB + 2 * HALO, D_CONV), F32),
                        pltpu.VMEM((SUBLANES, SHIFT_ROWS, D_CONV), F32)],
        compiler_params=_cparams(("arbitrary",)),
    )(z, z, z, z, z, z, z, z, conv_dw, conv_b, conv_ln_g, conv_ln_b, gm_ln_g, gm_ln_b,
      gm_ws, gm_bs_t)


def _ctx_attn_kernel(q_ref, k_ref, v_ref, kin_ref, vin_ref, o_ref, ko_ref, vo_ref):
    del kin_ref, vin_ref
    kt = k_ref[...].T
    vt = v_ref[...].T
    ko_ref[...] = kt.reshape(N_HEADS, HEAD_DIM, SEQ)
    vo_ref[...] = vt.reshape(N_HEADS, HEAD_DIM, SEQ)
    q = (q_ref[...].astype(F32) * HEAD_DIM ** -0.5).astype(BF16)
    kb = kt.astype(BF16)
    vb = vt.astype(BF16)
    lower = lax.broadcasted_iota(jnp.int32, (SEQ, HEAD_PAIR), 1) < HEAD_DIM
    for hp in range(N_HEADS // 2):
        grp = slice(hp * HEAD_PAIR, (hp + 1) * HEAD_PAIR)
        q2, k2, v2 = q[:, grp], kb[grp, :], vb[grp, :]
        outs = []
        for mask in (lower, jnp.logical_not(lower)):
            qh = jnp.where(mask, q2, jnp.zeros_like(q2))
            s = jnp.dot(qh, k2, preferred_element_type=F32)
            m = jnp.max(s, axis=-1, keepdims=True)
            p = jnp.exp(s - m)
            den = jnp.sum(p, axis=-1, keepdims=True)
            o = lax.dot_general(p.astype(BF16), v2, _NT, preferred_element_type=F32)
            outs.append(o / den)
        o_ref[:, grp] = jnp.where(lower, outs[0], outs[1]).astype(BF16)


def _ctx_attn(l, z, zkv, kt_all, vt_all):
    cache_blk = pl.BlockSpec((None, None, N_HEADS, HEAD_DIM, SEQ), lambda b: (b, l, 0, 0, 0))
    cache_shape = jax.ShapeDtypeStruct((BATCH, DEPTH, N_HEADS, HEAD_DIM, SEQ), F32)
    return pl.pallas_call(
        _ctx_attn_kernel,
        name="ctx_attn",
        grid=(BATCH,),
        in_specs=[pl.BlockSpec((SEQ, COL_BLK), lambda b: (b, CB_Q)),
                  pl.BlockSpec((SEQ, COL_BLK), lambda b: (b, CB_K)),
                  pl.BlockSpec((SEQ, COL_BLK), lambda b: (b, CB_V)),
                  pl.BlockSpec(memory_space=pl.ANY), pl.BlockSpec(memory_space=pl.ANY)],
        out_specs=[pl.BlockSpec((SEQ, D_NA), lambda b: (b, 0)), cache_blk, cache_blk],
        out_shape=[jax.ShapeDtypeStruct((T_PROMPT, D_NA), BF16), cache_shape, cache_shape],
        input_output_aliases={3: 1, 4: 2},
        compiler_params=_cparams(("arbitrary",)),
    )(z, zkv, zkv, kt_all, vt_all)


def _rope(x, cos, sin_up, sin_dn):
    return (x * cos + pltpu.roll(x, D_NA - HEAD_DIM // 4, 1) * sin_up
            + pltpu.roll(x, HEAD_DIM // 4, 1) * sin_dn)


def _na_attn_kernel(q_ref, k_ref, v_ref, ck_ref, cv_ref, tz_ref, cos_ref, sup_ref, sdn_ref,
                    o_ref, krot_ref, vb_ref, ckb_ref, cvb_ref):
    r = pl.program_id(1)

    @pl.when(r == 0)
    def _():
        krot_ref[...] = _rope(k_ref[...], cos_ref[...], sup_ref[...], sdn_ref[...]).astype(BF16)
        vb_ref[...] = v_ref[...].astype(BF16)
        ckb_ref[...] = ck_ref[...].astype(BF16)
        cvb_ref[...] = cv_ref[...].astype(BF16)

    qrows = pl.ds(pl.multiple_of(r * GRID_W, GRID_W), GRID_W)
    q = _rope(q_ref[...].astype(F32), cos_ref[qrows, :], sup_ref[qrows, :], sdn_ref[qrows, :])
    q = (q * HEAD_DIM ** -0.5).astype(BF16)

    row_start = jnp.clip(r - WIN_H // 2, 0, GRID_H - WIN_H)
    krows = pl.ds(pl.multiple_of(row_start * GRID_W, GRID_W), WIN_H * GRID_W)
    kwin = krot_ref[krows, :]
    vwin = vb_ref[krows, :]
    dr0 = row_start - r + WIN_H - 1
    lower = lax.broadcasted_iota(jnp.int32, (GRID_W, HEAD_PAIR), 1) < HEAD_DIM
    heads = range(N_HEADS)
    grp = [slice(h // 2 * HEAD_PAIR, (h // 2 + 1) * HEAD_PAIR) for h in heads]
    mask = [lower if h % 2 == 0 else jnp.logical_not(lower) for h in heads]
    qh = [jnp.where(mask[h], q[:, grp[h]], jnp.zeros((GRID_W, HEAD_PAIR), BF16)) for h in heads]
    s_loc = [lax.dot_general(qh[h], kwin[:, grp[h]], _NT, preferred_element_type=F32)
             + jnp.concatenate([tz_ref[h, dr0 + w] for w in range(0, WIN_H, 2)], axis=1)
             for h in heads]
    s_ctx = [lax.dot_general(qh[h], ckb_ref[:, grp[h]], _NT, preferred_element_type=F32)
             for h in heads]
    m = [jnp.maximum(jnp.max(s_loc[h], axis=-1, keepdims=True),
                     jnp.max(s_ctx[h], axis=-1, keepdims=True)) for h in heads]
    p_loc = [jnp.exp(s_loc[h] - m[h]) for h in heads]
    p_ctx = [jnp.exp(s_ctx[h] - m[h]) for h in heads]
    den = [jnp.sum(p_loc[h], axis=-1, keepdims=True) + jnp.sum(p_ctx[h], axis=-1, keepdims=True)
           for h in heads]
    o = [(jnp.dot(p_loc[h].astype(BF16), vwin[:, grp[h]], preferred_element_type=F32)
          + jnp.dot(p_ctx[h].astype(BF16), cvb_ref[:, grp[h]], preferred_element_type=F32)) / den[h]
         for h in heads]
    for h in range(0, N_HEADS, 2):
        o_ref[:, grp[h]] = jnp.where(lower, o[h], o[h + 1]).astype(BF16)


def _na_attn(l, z, zkv, cache_k, cache_v, tz, cos, sin_up, sin_dn):
    seq_blk0 = T_PROMPT // DEC_SEQ
    row_blk0 = T_PROMPT // GRID_W
    full = pl.BlockSpec((DEC_SEQ, D_NA), lambda b, r: (0, 0))
    return pl.pallas_call(
        _na_attn_kernel,
        name="na_attn",
        grid=(DEC_BATCH, GRID_H),
        in_specs=[
            pl.BlockSpec((GRID_W, COL_BLK), lambda b, r: (row_blk0 + b * GRID_H + r, CB_Q)),
            pl.BlockSpec((DEC_SEQ, COL_BLK), lambda b, r: (seq_blk0 + b, CB_K)),
            pl.BlockSpec((DEC_SEQ, COL_BLK), lambda b, r: (seq_blk0 + b, CB_V)),
            pl.BlockSpec((None, None, PAST_LEN, D_NA), lambda b, r: (b, l, 0, 0)),
            pl.BlockSpec((None, None, PAST_LEN, D_NA), lambda b, r: (b, l, 0, 0)),
            pl.BlockSpec((None, N_HEADS, 2 * WIN_H - 2, GRID_W, 2 * GRID_W),
                         lambda b, r: (l, 0, 0, 0, 0)),
            full, full, full,
        ],
        out_specs=pl.BlockSpec((GRID_W, D_NA), lambda b, r: (b * GRID_H + r, 0)),
        out_shape=jax.ShapeDtypeStruct((T_SAMPLE, D_NA), BF16),
        scratch_shapes=[pltpu.VMEM((DEC_SEQ, D_NA), BF16), pltpu.VMEM((DEC_SEQ, D_NA), BF16),
                        pltpu.VMEM((PAST_LEN, D_NA), BF16), pltpu.VMEM((PAST_LEN, D_NA), BF16)],
        compiler_params=_cparams(("arbitrary", "arbitrary")),
    )(z, zkv, zkv, cache_k, cache_v, tz, cos, sin_up, sin_dn)


def _merge_kernel(x_ref, mod_ref, g0a, g0b, g1a, g1b, g2a, g2b, yc_ref, ug_ref, ap_ref, as_ref,
                  pw_ref, no_ref, go_ref, wo_ref, lg_ref, lb_ref, o_ref):
    i = pl.program_id(0)
    att = jnp.where(i < N_TBM_PROMPT, ap_ref[...], as_ref[...])
    br_c = jnp.dot(yc_ref[...], pw_ref[...], preferred_element_type=F32)
    br_a = jnp.dot(att, no_ref[...], preferred_element_type=F32)
    br_g = jnp.dot(ug_ref[...], go_ref[...], preferred_element_type=F32)
    h = COL_BLK
    for lo, ga, gb, gc in ((0, g0a, g1a, g2a), (h, g0b, g1b, g2b)):
        cols = slice(lo, lo + h)
        o_ref[:, cols] = (_sigmoid(ga[...].astype(F32)) * br_c[:, cols]
                          + _sigmoid(gb[...].astype(F32)) * br_a[:, cols]
                          + _sigmoid(gc[...].astype(F32)) * br_g[:, cols])
    mix = jnp.dot(o_ref[...].astype(BF16), wo_ref[...], preferred_element_type=F32)
    o_ref[...] = _ln(ALPHA * x_ref[...] + mod_ref[2] * mix, lg_ref[...], lb_ref[...])


def _merge(l, x, mods, z, yc, ug, att_p, att_s, conv_pw, na_out, gm_out, w_o, ln_g, ln_b):
    def gz(k):
        return pl.BlockSpec((TBM, COL_BLK), lambda i: (i, CB_GZ + k))

    def w(k, n):
        return pl.BlockSpec((None, k, n), lambda i: (l, 0, 0))

    blk512 = pl.BlockSpec((TBM, COL_BLK), lambda i: (i, 0))
    return pl.pallas_call(
        _merge_kernel,
        name="merge",
        grid=(N_TBM,),
        in_specs=[
            pl.BlockSpec((TBM, D_MODEL), lambda i: (i, 0)),
            pl.BlockSpec((None, 6, None, 1, D_MODEL),
                         lambda i: (l, 0, _mod_row(i, TBM_PER_SAMPLE, N_TBM_PROMPT), 0, 0)),
            gz(0), gz(1), gz(2), gz(3), gz(4), gz(5),
            blk512, blk512,
            pl.BlockSpec((TBM, D_NA), lambda i: (jnp.minimum(i, N_TBM_PROMPT - 1), 0)),
            pl.BlockSpec((TBM, D_NA), lambda i: (jnp.maximum(i - N_TBM_PROMPT, 0), 0)),
            w(D_CONV, D_MODEL), w(D_NA, D_MODEL), w(D_GM, D_MODEL), w(D_MODEL, D_MODEL),
            w(1, D_MODEL), w(1, D_MODEL),
        ],
        out_specs=pl.BlockSpec((TBM, D_MODEL), lambda i: (i, 0)),
        out_shape=jax.ShapeDtypeStruct((T_ALL, D_MODEL), F32),
        compiler_params=_cparams(("arbitrary",)),
    )(x, mods, z, z, z, z, z, z, yc, ug, att_p, att_s, conv_pw, na_out, gm_out, w_o, ln_g, ln_b)


def _route(logits):
    lane = lax.broadcasted_iota(jnp.int32, logits.shape, 1)
    big = jnp.int32(ROUTER_LANES)
    is_g = lane < N_EGROUPS
    gl = jnp.where(is_g, logits, -jnp.inf)
    gmax = jnp.max(gl, axis=-1, keepdims=True)
    gidx = jnp.min(jnp.where(gl == gmax, lane, big), axis=-1, keepdims=True)
    gp = 1.0 / jnp.sum(jnp.where(is_g, jnp.exp(gl - gmax), 0.0), axis=-1, keepdims=True)
    lo = N_EGROUPS + gidx * EXP_PER_GROUP
    el = jnp.where(jnp.logical_and(lane >= lo, lane < lo + EXP_PER_GROUP), logits, -jnp.inf)
    v1 = jnp.max(el, axis=-1, keepdims=True)
    i1 = jnp.min(jnp.where(el == v1, lane, big), axis=-1, keepdims=True)
    el2 = jnp.where(lane == i1, -jnp.inf, el)
    v2 = jnp.max(el2, axis=-1, keepdims=True)
    i2 = jnp.min(jnp.where(el2 == v2, lane, big), axis=-1, keepdims=True)
    e2 = jnp.exp(v2 - v1)
    w1 = gp / (1.0 + e2)
    w2 = gp * e2 / (1.0 + e2)
    return jnp.where(lane == i1, w1, 0.0) + jnp.where(lane == i2, w2, 0.0)


def _moe_kernel(x_ref, mod_ref, modn_ref, rw_ref, rb_ref, w1_ref, w3_ref, w2_ref, lg_ref, lb_ref,
                o_ref, *rest):
    hn_ref = rest[0] if len(rest) == 4 else None
    t_ref, gate_ref, acc_ref = rest[-3:]
    e = pl.program_id(1)

    @pl.when(e == 0)
    def _():
        t = x_ref[...] * (1.0 + mod_ref[4]) + mod_ref[3]
        t_hi = t.astype(BF16)
        t_ref[...] = t_hi
        t_lo = (t - t_hi.astype(F32)).astype(BF16)
        logits = (jnp.dot(t_hi, rw_ref[0], preferred_element_type=F32)
                  + jnp.dot(t_lo, rw_ref[0], preferred_element_type=F32)
                  + jnp.dot(t_hi, rw_ref[1], preferred_element_type=F32))
        gate_ref[...] = _route(logits + rb_ref[...])
        acc_ref[...] = jnp.zeros_like(acc_ref)

    t = t_ref[...]
    w1 = jnp.concatenate([w1_ref[k] for k in range(EXP_STEP)], axis=1).astype(BF16)
    w3 = jnp.concatenate([w3_ref[k] for k in range(EXP_STEP)], axis=1).astype(BF16)
    w2 = w2_ref[...].reshape(EXP_STEP * D_EXPERT, D_MODEL).astype(BF16)
    h1 = jnp.dot(t, w1, preferred_element_type=F32)
    h3 = jnp.dot(t, w3, preferred_element_type=F32)
    gate = gate_ref[...]
    lane = lax.broadcasted_iota(jnp.int32, gate.shape, 1)
    hcol = lax.broadcasted_iota(jnp.int32, h1.shape, 1) // D_EXPERT
    gmul = jnp.zeros(h1.shape, F32)
    for k in range(EXP_STEP):
        gcol = jnp.sum(jnp.where(lane == e * EXP_STEP + k + N_EGROUPS, gate, 0.0),
                       axis=-1, keepdims=True)
        gmul = jnp.where(hcol == k, gcol, gmul)
    hid = (h1 * _sigmoid(h1) * h3 * gmul).astype(BF16)
    acc_ref[...] += jnp.dot(hid, w2, preferred_element_type=F32)

    @pl.when(e == N_EXPERTS // EXP_STEP - 1)
    def _():
        y = _ln(ALPHA * x_ref[...] + mod_ref[5] * acc_ref[...], lg_ref[...], lb_ref[...])
        o_ref[...] = y
        if hn_ref is not None:
            hn_ref[...] = (y * (1.0 + modn_ref[1]) + modn_ref[0]).astype(BF16)


def _moe(l, x, mods, router_w, router_b, w1, w3, w2, ln_g, ln_b):
    n_m = T_ALL // TM_MOE
    bps = DEC_SEQ // TM_MOE
    npb = T_PROMPT // TM_MOE
    has_next = l + 1 < DEPTH
    l_next = l + 1 if has_next else l
    tok_blk = pl.BlockSpec((TM_MOE, D_MODEL), lambda i, e: (i, 0))
    out_shape = [jax.ShapeDtypeStruct((T_ALL, D_MODEL), F32)]
    if has_next:
        out_shape.append(jax.ShapeDtypeStruct((T_ALL, D_MODEL), BF16))
    return pl.pallas_call(
        _moe_kernel,
        name="moe",
        grid=(n_m, N_EXPERTS // EXP_STEP),
        in_specs=[
            tok_blk,
            pl.BlockSpec((None, 6, None, 1, D_MODEL),
                         lambda i, e: (l, 0, _mod_row(i, bps, npb), 0, 0)),
            pl.BlockSpec((None, 6, None, 1, D_MODEL),
                         lambda i, e: (l_next, 0, _mod_row(i, bps, npb), 0, 0)),
            pl.BlockSpec((None, 2, D_MODEL, ROUTER_LANES), lambda i, e: (l, 0, 0, 0)),
            pl.BlockSpec((None, 1, ROUTER_LANES), lambda i, e: (l, 0, 0)),
            pl.BlockSpec((None, EXP_STEP, D_MODEL, D_EXPERT), lambda i, e: (l, e, 0, 0)),
            pl.BlockSpec((None, EXP_STEP, D_MODEL, D_EXPERT), lambda i, e: (l, e, 0, 0)),
            pl.BlockSpec((None, EXP_STEP, D_EXPERT, D_MODEL), lambda i, e: (l, e, 0, 0)),
            pl.BlockSpec((None, 1, D_MODEL), lambda i, e: (l, 0, 0)),
            pl.BlockSpec((None, 1, D_MODEL), lambda i, e: (l, 0, 0)),
        ],
        out_specs=[tok_blk] * len(out_shape),
        out_shape=out_shape,
        scratch_shapes=[pltpu.VMEM((TM_MOE, D_MODEL), BF16),
                        pltpu.VMEM((TM_MOE, ROUTER_LANES), F32),
                        pltpu.VMEM((TM_MOE, D_MODEL), F32)],
        compiler_params=_cparams(("arbitrary", "arbitrary")),
    )(x, mods, mods, router_w, router_b, w1, w3, w2, ln_g, ln_b)


def _rope_tables():
    t = np.arange(DEC_SEQ)
    pos = np.stack([t // GRID_W, t % GRID_W], axis=1).astype(np.float32)
    quarter = HEAD_DIM // 4
    d = np.arange(HEAD_DIM)
    axis = d // (HEAD_DIM // 2)
    freq = d % quarter
    upper = (d % (HEAD_DIM // 2)) >= quarter
    inv = jnp.asarray(ROPE_BASE, F32) ** (-jnp.arange(0, HEAD_DIM // 2, 2, dtype=F32) / (HEAD_DIM // 2))
    ang = jnp.asarray(pos)[:, axis] * inv[freq][None, :]
    cos = jnp.cos(ang)
    sin = jnp.sin(ang)
    sin_up = jnp.where(upper[None, :], 0.0, -sin)
    sin_dn = jnp.where(upper[None, :], sin, 0.0)
    tile = lambda a: jnp.tile(a, (1, N_HEADS))
    return tile(cos), tile(sin_up), tile(sin_dn)


def _bias_tables(na_rpb):
    qc = np.arange(GRID_W)[:, None]
    kc = np.arange(GRID_W)[None, :]
    start = np.clip(qc - WIN_W // 2, 0, GRID_W - WIN_W)
    valid = (kc >= start) & (kc < start + WIN_W)
    n_dc = 2 * WIN_W - 1
    dc = kc - qc + WIN_W - 1
    hit = ((dc[None] == np.arange(n_dc)[:, None, None]) & valid[None]).astype(np.float32)
    sel = np.zeros((2 * n_dc + 1, GRID_W, 2 * GRID_W), np.float32)
    sel[:n_dc, :, :GRID_W] = hit
    sel[n_dc:2 * n_dc, :, GRID_W:] = hit
    sel[2 * n_dc] = np.where(np.concatenate([valid, valid], axis=1), 0.0, NEG_INF)
    ones = jnp.ones(na_rpb.shape[:2] + (2 * WIN_H - 2, 1), F32)
    rows = jnp.concatenate([na_rpb[:, :, :-1], na_rpb[:, :, 1:], ones], axis=-1)
    return jnp.einsum('lhrd,dqk->lhrqk', rows, jnp.asarray(sel), precision=HIGHEST)


def kernel(x_prompt, x_sample, cache_na_k, cache_na_v, c, c_ctx, w_ada, b_ada, w_in, conv_dw,
           conv_b, conv_ln_g, conv_ln_b, conv_pw, na_rpb, na_out, gm_ln_g, gm_ln_b, gm_ws, gm_bs,
           gm_out, w_o, ln1_g, ln1_b, rg_w, rg_b, re_w, re_b, moe_w1, moe_w3, moe_w2, ln2_g, ln2_b):
    x = jnp.concatenate([x_prompt.reshape(T_PROMPT, D_MODEL),
                         x_sample.reshape(T_SAMPLE, D_MODEL)], axis=0)

    cond = jnp.zeros((N_COND, D_MODEL), F32).at[0].set(c_ctx).at[1:1 + DEC_BATCH].set(c)
    mods = _ada(cond, w_ada, b_ada)
    mods = mods.reshape(DEPTH, N_COND, 6, 1, D_MODEL).transpose(0, 2, 1, 3, 4)

    bf = lambda a: a.astype(BF16)
    conv_pw_b, na_out_b, gm_out_b, w_o_b, gm_ws_b = bf(conv_pw), bf(na_out), bf(gm_out), bf(w_o), bf(gm_ws)
    vec = lambda a: a.reshape(DEPTH, 1, a.shape[-1])
    gm_bs_t = gm_bs.transpose(0, 2, 1)
    router_w = jnp.concatenate(
        [rg_w, re_w.transpose(0, 2, 1, 3).reshape(DEPTH, D_MODEL, N_EXPERTS)], axis=-1)
    router_w = jnp.pad(router_w, ((0, 0), (0, 0), (0, ROUTER_LANES - N_EGROUPS - N_EXPERTS)))
    router_hi = router_w.astype(BF16)
    router_lo = (router_w - router_hi.astype(F32)).astype(BF16)
    router_w = jnp.stack([router_hi, router_lo], axis=1)
    router_b = jnp.concatenate([rg_b, re_b.reshape(DEPTH, N_EXPERTS)], axis=-1)
    router_b = jnp.pad(router_b, ((0, 0), (0, ROUTER_LANES - N_EGROUPS - N_EXPERTS)))
    router_b = router_b.reshape(DEPTH, 1, ROUTER_LANES)
    cache_k = cache_na_k.reshape(DEC_BATCH, DEPTH, PAST_LEN, D_NA)
    cache_v = cache_na_v.reshape(DEC_BATCH, DEPTH, PAST_LEN, D_NA)
    tz = _bias_tables(na_rpb)
    cos, sin_up, sin_dn = _rope_tables()

    kt_all = jnp.zeros((BATCH, DEPTH, N_HEADS, HEAD_DIM, SEQ), F32)
    vt_all = jnp.zeros((BATCH, DEPTH, N_HEADS, HEAD_DIM, SEQ), F32)
    h = _modulate(0, x, mods)
    for l in range(DEPTH):
        z = _inproj(l, h, w_in, lambda j: jnp.where(j < W_CB_KV, j, j + N_CB_KV), N_CB_MAIN, BF16)
        zkv = _inproj(l, h, w_in, lambda j: j + W_CB_KV, N_CB_KV, F32)
        yc, ug = _branches(l, z, conv_dw, vec(conv_b), vec(conv_ln_g), vec(conv_ln_b),
                           vec(gm_ln_g), vec(gm_ln_b), gm_ws_b, gm_bs_t)
        att_p, kt_all, vt_all = _ctx_attn(l, z, zkv, kt_all, vt_all)
        att_s = _na_attn(l, z, zkv, cache_k, cache_v, tz, cos, sin_up, sin_dn)
        x = _merge(l, x, mods, z, yc, ug, att_p, att_s, conv_pw_b, na_out_b, gm_out_b, w_o_b,
                   vec(ln1_g), vec(ln1_b))
        outs = _moe(l, x, mods, router_w, router_b, moe_w1, moe_w3, moe_w2, vec(ln2_g), vec(ln2_b))
        x, h = outs[0], outs[-1]

    y_prompt = x[:T_PROMPT].reshape(BATCH, SEQ, D_MODEL)
    y_sample = x[T_PROMPT:].reshape(DEC_BATCH, DEC_SEQ, D_MODEL)
    return (y_prompt, y_sample, kt_all.transpose(0, 1, 4, 2, 3), vt_all.transpose(0, 1, 4, 2, 3))
```

```python
import functools

import jax
import jax.numpy as jnp
import numpy as np
from jax import lax
from jax.experimental import pallas as pl
from jax.experimental.pallas import tpu as pltpu

F32 = jnp.float32
BF16 = jnp.bfloat16
HIGHEST = lax.Precision.HIGHEST

D_MODEL = 1024
BATCH = 16
SEQ = 256
DEPTH = 4
DEC_BATCH = 2
DEC_SEQ = 1024
PAST_LEN = 256
GRID_W = 64
GRID_H = DEC_SEQ // GRID_W
D_CONV = 512
CONV_WIDTH = 31
CONV_HALF = CONV_WIDTH // 2
HEAD_DIM = 64
HEAD_PAIR = 2 * HEAD_DIM
N_HEADS = 8
D_NA = 512
WIN_H = 8
WIN_W = 16
ROPE_BASE = 10000.0
D_GM = 512
GM_CHUNK = 128
GM_GROUPS = 4
D_IN = 6656
N_EGROUPS = 4
EXP_PER_GROUP = 8
N_EXPERTS = 32
D_EXPERT = 128
ALPHA = (2 * DEPTH) ** 0.25
LN_EPS = 1e-5
NEG_INF = -1e30

T_PROMPT = BATCH * SEQ
T_SAMPLE = DEC_BATCH * DEC_SEQ
T_ALL = T_PROMPT + T_SAMPLE
N_COND = 8

COL_BLK = 512
W_CB_KV = 3
N_CB_KV = 2
N_CB_MAIN = D_IN // COL_BLK - N_CB_KV
CB_A, CB_B, CB_Q, CB_GU, CB_GV, CB_GZ = 0, 1, 2, 3, 4, 5
CB_K, CB_V = 0, 1

TB = 256
N_TB = T_ALL // TB
N_TB_PROMPT = T_PROMPT // TB
TB_PER_SAMPLE = DEC_SEQ // TB
TBM = 512
N_TBM = T_ALL // TBM
N_TBM_PROMPT = T_PROMPT // TBM
TBM_PER_SAMPLE = DEC_SEQ // TBM
HALO = 16
CONV_ROWS = 32
SUBLANES = 8
SHIFT_ROWS = TB + 2 * HALO - SUBLANES

NA_ROWS = 2
NA_TQ = NA_ROWS * GRID_W
TM_IN = 2048
TM_MOE = 1024
EXP_STEP = 4
EXP_SUB = 4
ROUTER_LANES = 128
VMEM_LIMIT = 56 * 1024 * 1024


def _ln(x, g, b):
    mu = jnp.mean(x, axis=-1, keepdims=True)
    xc = x - mu
    var = jnp.mean(xc * xc, axis=-1, keepdims=True)
    return xc * lax.rsqrt(var + LN_EPS) * g + b


def _sigmoid(x):
    return jax.nn.sigmoid(x)


def _gelu(x):
    return jax.nn.gelu(x, approximate=True)


_NT = (((1,), (1,)), ((), ()))


def _cparams(sem):
    return pltpu.CompilerParams(dimension_semantics=sem, vmem_limit_bytes=VMEM_LIMIT)


def _ada_kernel(c_ref, w_ref, b_ref, o_ref):
    c = c_ref[...]
    s = c * _sigmoid(c)
    w = w_ref[...]
    s_hi, w_hi = s.astype(BF16), w.astype(BF16)
    s_lo = (s - s_hi.astype(F32)).astype(BF16)
    w_lo = (w - w_hi.astype(F32)).astype(BF16)
    o_ref[...] = (jnp.dot(s_hi, w_hi, preferred_element_type=F32)
                  + jnp.dot(s_lo, w_hi, preferred_element_type=F32)
                  + jnp.dot(s_hi, w_lo, preferred_element_type=F32)) + b_ref[...]


def _ada(cond, w_ada, b_ada):
    tn = 2048
    return pl.pallas_call(
        _ada_kernel,
        name="ada",
        grid=(DEPTH, 6 * D_MODEL // tn),
        in_specs=[
            pl.BlockSpec((N_COND, D_MODEL), lambda l, j: (0, 0)),
            pl.BlockSpec((None, D_MODEL, tn), lambda l, j: (l, 0, j)),
            pl.BlockSpec((None, 1, tn), lambda l, j: (l, 0, j)),
        ],
        out_specs=pl.BlockSpec((None, N_COND, tn), lambda l, j: (l, 0, j)),
        out_shape=jax.ShapeDtypeStruct((DEPTH, N_COND, 6 * D_MODEL), F32),
        compiler_params=_cparams(("arbitrary", "arbitrary")),
    )(cond, w_ada, b_ada.reshape(DEPTH, 1, 6 * D_MODEL))


def _mod_row(i, blocks_per_sample, n_prompt_blocks):
    return jnp.where(i < n_prompt_blocks, 0, 1 + (i - n_prompt_blocks) // blocks_per_sample)


def _modulate_kernel(x_ref, mod_ref, h_ref):
    h_ref[...] = (x_ref[...] * (1.0 + mod_ref[1]) + mod_ref[0]).astype(BF16)


def _modulate(l, x, mods):
    bps = DEC_SEQ // TM_MOE
    npb = T_PROMPT // TM_MOE
    return pl.pallas_call(
        _modulate_kernel,
        name="modulate",
        grid=(T_ALL // TM_MOE,),
        in_specs=[
            pl.BlockSpec((TM_MOE, D_MODEL), lambda i: (i, 0)),
            pl.BlockSpec((None, 6, None, 1, D_MODEL),
                         lambda i: (l, 0, _mod_row(i, bps, npb), 0, 0)),
        ],
        out_specs=pl.BlockSpec((TM_MOE, D_MODEL), lambda i: (i, 0)),
        out_shape=jax.ShapeDtypeStruct((T_ALL, D_MODEL), BF16),
        compiler_params=_cparams(("arbitrary",)),
    )(x, mods)


def _inproj_kernel(h_ref, w_ref, z_ref):
    rows = pl.ds(pl.multiple_of(pl.program_id(1) * TM_IN, TM_IN), TM_IN)
    z = jnp.dot(h_ref[rows, :], w_ref[...].astype(BF16), preferred_element_type=F32)
    z_ref[...] = z.astype(z_ref.dtype)


def _inproj(l, h, w_in, wcol, n_cols, out_dtype):
    return pl.pallas_call(
        _inproj_kernel,
        name="inproj",
        grid=(n_cols, T_ALL // TM_IN),
        in_specs=[
            pl.BlockSpec((T_ALL, D_MODEL), lambda j, i: (0, 0)),
            pl.BlockSpec((None, D_MODEL, COL_BLK), lambda j, i: (l, 0, wcol(j))),
        ],
        out_specs=pl.BlockSpec((TM_IN, COL_BLK), lambda j, i: (i, j)),
        out_shape=jax.ShapeDtypeStruct((T_ALL, n_cols * COL_BLK), out_dtype),
        compiler_params=_cparams(("arbitrary", "arbitrary")),
    )(h, w_in)


def _branch_kernel(ap_ref, ac_ref, an_ref, bp_ref, bc_ref, bn_ref, gu_ref, gv_ref,
                   dw_ref, cb_ref, clg_ref, clb_ref, glg_ref, glb_ref, ws_ref, bst_ref,
                   yc_ref, ug_ref, ypad_ref, ysh_ref):
    i = pl.program_id(0)
    j = i - N_TB_PROMPT
    in_sample = i >= N_TB_PROMPT
    has_prev = jnp.logical_and(in_sample, j % TB_PER_SAMPLE != 0)
    has_next = jnp.logical_and(in_sample, j % TB_PER_SAMPLE != TB_PER_SAMPLE - 1)

    def glu(a_ref, b_ref):
        return a_ref[...].astype(F32) * _sigmoid(b_ref[...].astype(F32))

    ypad_ref[0:HALO, :] = jnp.where(has_prev, glu(ap_ref, bp_ref), 0.0)
    ypad_ref[HALO:HALO + TB, :] = glu(ac_ref, bc_ref)
    ypad_ref[HALO + TB:HALO + TB + HALO, :] = jnp.where(has_next, glu(an_ref, bn_ref), 0.0)

    for b in range(SUBLANES):
        ysh_ref[b] = ypad_ref[b:b + SHIFT_ROWS, :]

    off = HALO - CONV_HALF
    for c in range(TB // CONV_ROWS):
        base = c * CONV_ROWS
        acc = jnp.zeros((CONV_ROWS, D_CONV), F32)
        for k in range(CONV_WIDTH):
            tile, phase = divmod(off + k, SUBLANES)
            start = base + tile * SUBLANES
            acc = acc + ysh_ref[phase, start:start + CONV_ROWS, :] * dw_ref[k:k + 1, :]
        y = _ln(acc + cb_ref[...], clg_ref[...], clb_ref[...])
        yc_ref[base:base + CONV_ROWS, :] = (y * _sigmoid(y)).astype(BF16)

    for n in range(TB // GM_CHUNK):
        rows = slice(n * GM_CHUNK, (n + 1) * GM_CHUNK)
        u = _gelu(gu_ref[rows, :].astype(F32))
        v = _ln(_gelu(gv_ref[rows, :].astype(F32)), glg_ref[...], glb_ref[...]).astype(BF16)
        for g in range(GM_GROUPS):
            cols = slice(g * GM_CHUNK, (g + 1) * GM_CHUNK)
            sv = jnp.dot(ws_ref[g], v[:, cols], preferred_element_type=F32) + bst_ref[:, g:g + 1]
            ug_ref[rows, cols] = (u[:, cols] * sv).astype(BF16)


def _branches(l, z, conv_dw, conv_b, conv_ln_g, conv_ln_b, gm_ln_g, gm_ln_b, gm_ws, gm_bs_t):
    halo_per_tb = TB // HALO
    n_halo = T_ALL // HALO

    def cur(cb):
        return pl.BlockSpec((TB, COL_BLK), lambda i: (i, cb))

    def prev(cb):
        return pl.BlockSpec((HALO, COL_BLK), lambda i: (jnp.maximum(i * halo_per_tb - 1, 0), cb))

    def nxt(cb):
        return pl.BlockSpec((HALO, COL_BLK),
                            lambda i: (jnp.minimum((i + 1) * halo_per_tb, n_halo - 1), cb))

    def vec(n):
        return pl.BlockSpec((None, 1, n), lambda i: (l, 0, 0))

    return pl.pallas_call(
        _branch_kernel,
        name="branches",
        grid=(N_TB,),
        in_specs=[
            prev(CB_A), cur(CB_A), nxt(CB_A), prev(CB_B), cur(CB_B), nxt(CB_B),
            cur(CB_GU), cur(CB_GV),
            pl.BlockSpec((None, CONV_WIDTH, D_CONV), lambda i: (l, 0, 0)),
            vec(D_CONV), vec(D_CONV), vec(D_CONV), vec(D_GM), vec(D_GM),
            pl.BlockSpec((None, GM_GROUPS, GM_CHUNK, GM_CHUNK), lambda i: (l, 0, 0, 0)),
            pl.BlockSpec((None, GM_CHUNK, GM_GROUPS), lambda i: (l, 0, 0)),
        ],
        out_specs=[pl.BlockSpec((TB, D_CONV), lambda i: (i, 0)),
                   pl.BlockSpec((TB, D_GM), lambda i: (i, 0))],
        out_shape=[jax.ShapeDtypeStruct((T_ALL, D_CONV), BF16),
                   jax.ShapeDtypeStruct((T_ALL, D_GM), BF16)],
        scratch_shapes=[pltpu.VMEM((TB + 2 * HALO, D_CONV), F32),
                        pltpu.VMEM((SUBLANES, SHIFT_ROWS, D_CONV), F32)],
        compiler_params=_cparams(("arbitrary",)),
    )(z, z, z, z, z, z, z, z, conv_dw, conv_b, conv_ln_g, conv_ln_b, gm_ln_g, gm_ln_b,
      gm_ws, gm_bs_t)


def _ctx_attn_kernel(first, q_ref, k_ref, v_ref, *rest):
    o_ref, ko_ref, vo_ref = rest[-3:]
    kt = k_ref[...].T
    vt = v_ref[...].T
    if first:
        ko_ref[0] = kt.reshape(N_HEADS, HEAD_DIM, SEQ)
        vo_ref[0] = vt.reshape(N_HEADS, HEAD_DIM, SEQ)
        ko_ref[1:] = jnp.zeros((DEPTH - 1, N_HEADS, HEAD_DIM, SEQ), F32)
        vo_ref[1:] = jnp.zeros((DEPTH - 1, N_HEADS, HEAD_DIM, SEQ), F32)
    else:
        ko_ref[...] = kt.reshape(N_HEADS, HEAD_DIM, SEQ)
        vo_ref[...] = vt.reshape(N_HEADS, HEAD_DIM, SEQ)
    q = (q_ref[...].astype(F32) * HEAD_DIM ** -0.5).astype(BF16)
    kb = kt.astype(BF16)
    vb = vt.astype(BF16)
    lower = lax.broadcasted_iota(jnp.int32, (SEQ, HEAD_PAIR), 1) < HEAD_DIM
    upper = jnp.logical_not(lower)
    heads = range(N_HEADS)
    grp = [slice(h // 2 * HEAD_PAIR, (h // 2 + 1) * HEAD_PAIR) for h in heads]
    qh = [jnp.where(lower if h % 2 == 0 else upper, q[:, grp[h]],
                    jnp.zeros((SEQ, HEAD_PAIR), BF16)) for h in heads]
    s = [jnp.dot(qh[h], kb[grp[h], :], preferred_element_type=F32) for h in heads]
    m = [jnp.max(s[h], axis=-1, keepdims=True) for h in heads]
    p = [jnp.exp(s[h] - m[h]) for h in heads]
    den = [jnp.sum(p[h], axis=-1, keepdims=True) for h in heads]
    o = [lax.dot_general(p[h].astype(BF16), vb[grp[h], :], _NT, preferred_element_type=F32) / den[h]
         for h in heads]
    for h in range(0, N_HEADS, 2):
        o_ref[:, grp[h]] = jnp.where(lower, o[h], o[h + 1]).astype(BF16)


def _ctx_attn(l, z, zkv, kt_all=None, vt_all=None):
    first = kt_all is None
    if first:
        cache_blk = pl.BlockSpec((None, DEPTH, N_HEADS, HEAD_DIM, SEQ), lambda b: (b, 0, 0, 0, 0))
        carried, carried_specs, aliases = (), [], {}
    else:
        cache_blk = pl.BlockSpec((None, None, N_HEADS, HEAD_DIM, SEQ), lambda b: (b, l, 0, 0, 0))
        carried = (kt_all, vt_all)
        carried_specs = [pl.BlockSpec(memory_space=pl.ANY), pl.BlockSpec(memory_space=pl.ANY)]
        aliases = {3: 1, 4: 2}
    cache_shape = jax.ShapeDtypeStruct((BATCH, DEPTH, N_HEADS, HEAD_DIM, SEQ), F32)
    return pl.pallas_call(
        functools.partial(_ctx_attn_kernel, first),
        name="ctx_attn",
        grid=(BATCH,),
        in_specs=[pl.BlockSpec((SEQ, COL_BLK), lambda b: (b, CB_Q)),
                  pl.BlockSpec((SEQ, COL_BLK), lambda b: (b, CB_K)),
                  pl.BlockSpec((SEQ, COL_BLK), lambda b: (b, CB_V))] + carried_specs,
        out_specs=[pl.BlockSpec((SEQ, D_NA), lambda b: (b, 0)), cache_blk, cache_blk],
        out_shape=[jax.ShapeDtypeStruct((T_PROMPT, D_NA), BF16), cache_shape, cache_shape],
        input_output_aliases=aliases,
        compiler_params=_cparams(("arbitrary",)),
    )(z, zkv, zkv, *carried)


def _rope(x, cos, sin_up, sin_dn):
    return (x * cos + pltpu.roll(x, D_NA - HEAD_DIM // 4, 1) * sin_up
            + pltpu.roll(x, HEAD_DIM // 4, 1) * sin_dn)


def _na_attn_kernel(q_ref, k_ref, v_ref, ck_ref, cv_ref, tz_ref, cos_ref, sup_ref, sdn_ref,
                    o_ref, krot_ref, vb_ref, ckb_ref, cvb_ref):
    step = pl.program_id(1)

    @pl.when(step == 0)
    def _():
        krot_ref[...] = _rope(k_ref[...], cos_ref[...], sup_ref[...], sdn_ref[...]).astype(BF16)
        vb_ref[...] = v_ref[...].astype(BF16)
        ckb_ref[...] = ck_ref[...].astype(BF16)
        cvb_ref[...] = cv_ref[...].astype(BF16)

    qrows = pl.ds(pl.multiple_of(step * NA_TQ, NA_TQ), NA_TQ)
    q = _rope(q_ref[...].astype(F32), cos_ref[qrows, :], sup_ref[qrows, :], sdn_ref[qrows, :])
    q = (q * HEAD_DIM ** -0.5).astype(BF16)

    lower = lax.broadcasted_iota(jnp.int32, (GRID_W, HEAD_PAIR), 1) < HEAD_DIM
    upper = jnp.logical_not(lower)
    kwin, vwin, dr0 = [], [], []
    for j in range(NA_ROWS):
        r = step * NA_ROWS + j
        row_start = jnp.clip(r - WIN_H // 2, 0, GRID_H - WIN_H)
        krows = pl.ds(pl.multiple_of(row_start * GRID_W, GRID_W), WIN_H * GRID_W)
        kwin.append(krot_ref[krows, :])
        vwin.append(vb_ref[krows, :])
        dr0.append(row_start - r + WIN_H - 1)

    units = [(j, h) for j in range(NA_ROWS) for h in range(N_HEADS)]
    grp = [slice(h // 2 * HEAD_PAIR, (h // 2 + 1) * HEAD_PAIR) for _, h in units]
    qh = [jnp.where(lower if h % 2 == 0 else upper, q[j * GRID_W:(j + 1) * GRID_W, grp[u]],
                    jnp.zeros((GRID_W, HEAD_PAIR), BF16)) for u, (j, h) in enumerate(units)]
    s_loc = [lax.dot_general(qh[u], kwin[j][:, grp[u]], _NT, preferred_element_type=F32)
             + jnp.concatenate([tz_ref[h, dr0[j] + w] for w in range(0, WIN_H, 2)], axis=1)
             for u, (j, h) in enumerate(units)]
    s_ctx = [lax.dot_general(qh[u], ckb_ref[:, grp[u]], _NT, preferred_element_type=F32)
             for u in range(len(units))]
    m = [jnp.maximum(jnp.max(s_loc[u], axis=-1, keepdims=True),
                     jnp.max(s_ctx[u], axis=-1, keepdims=True)) for u in range(len(units))]
    p_loc = [jnp.exp(s_loc[u] - m[u]) for u in range(len(units))]
    p_ctx = [jnp.exp(s_ctx[u] - m[u]) for u in range(len(units))]
    den = [jnp.sum(p_loc[u], axis=-1, keepdims=True) + jnp.sum(p_ctx[u], axis=-1, keepdims=True)
           for u in range(len(units))]
    o = [(jnp.dot(p_loc[u].astype(BF16), vwin[j][:, grp[u]], preferred_element_type=F32)
          + jnp.dot(p_ctx[u].astype(BF16), cvb_ref[:, grp[u]], preferred_element_type=F32)) / den[u]
         for u, (j, h) in enumerate(units)]
    for u, (j, h) in enumerate(units):
        if h % 2 == 0:
            o_ref[j * GRID_W:(j + 1) * GRID_W, grp[u]] = jnp.where(lower, o[u], o[u + 1]).astype(BF16)


def _na_attn(l, z, zkv, cache_k, cache_v, tz, cos, sin_up, sin_dn):
    seq_blk0 = T_PROMPT // DEC_SEQ
    row_blk0 = T_PROMPT // NA_TQ
    steps = GRID_H // NA_ROWS
    full = pl.BlockSpec((DEC_SEQ, D_NA), lambda b, r: (0, 0))
    return pl.pallas_call(
        _na_attn_kernel,
        name="na_attn",
        grid=(DEC_BATCH, steps),
        in_specs=[
            pl.BlockSpec((NA_TQ, COL_BLK), lambda b, r: (row_blk0 + b * steps + r, CB_Q)),
            pl.BlockSpec((DEC_SEQ, COL_BLK), lambda b, r: (seq_blk0 + b, CB_K)),
            pl.BlockSpec((DEC_SEQ, COL_BLK), lambda b, r: (seq_blk0 + b, CB_V)),
            pl.BlockSpec((None, None, PAST_LEN, D_NA), lambda b, r: (b, l, 0, 0)),
            pl.BlockSpec((None, None, PAST_LEN, D_NA), lambda b, r: (b, l, 0, 0)),
            pl.BlockSpec((None, N_HEADS, 2 * WIN_H - 2, GRID_W, 2 * GRID_W),
                         lambda b, r: (l, 0, 0, 0, 0)),
            full, full, full,
        ],
        out_specs=pl.BlockSpec((NA_TQ, D_NA), lambda b, r: (b * steps + r, 0)),
        out_shape=jax.ShapeDtypeStruct((T_SAMPLE, D_NA), BF16),
        scratch_shapes=[pltpu.VMEM((DEC_SEQ, D_NA), BF16), pltpu.VMEM((DEC_SEQ, D_NA), BF16),
                        pltpu.VMEM((PAST_LEN, D_NA), BF16), pltpu.VMEM((PAST_LEN, D_NA), BF16)],
        compiler_params=_cparams(("arbitrary", "arbitrary")),
    )(z, zkv, zkv, cache_k, cache_v, tz, cos, sin_up, sin_dn)


def _merge_kernel(x_ref, mod_ref, g0a, g0b, g1a, g1b, g2a, g2b, yc_ref, ug_ref, ap_ref, as_ref,
                  pw_ref, no_ref, go_ref, wo_ref, lg_ref, lb_ref, o_ref,
                  pwb_ref, nob_ref, gob_ref, wob_ref):
    i = pl.program_id(0)

    @pl.when(i == 0)
    def _():
        pwb_ref[...] = pw_ref[...].astype(BF16)
        nob_ref[...] = no_ref[...].astype(BF16)
        gob_ref[...] = go_ref[...].astype(BF16)
        wob_ref[...] = wo_ref[...].astype(BF16)

    att = jnp.where(i < N_TBM_PROMPT, ap_ref[...], as_ref[...])
    br_c = jnp.dot(yc_ref[...], pwb_ref[...], preferred_element_type=F32)
    br_a = jnp.dot(att, nob_ref[...], preferred_element_type=F32)
    br_g = jnp.dot(ug_ref[...], gob_ref[...], preferred_element_type=F32)
    h = COL_BLK
    for lo, ga, gb, gc in ((0, g0a, g1a, g2a), (h, g0b, g1b, g2b)):
        cols = slice(lo, lo + h)
        o_ref[:, cols] = (_sigmoid(ga[...].astype(F32)) * br_c[:, cols]
                          + _sigmoid(gb[...].astype(F32)) * br_a[:, cols]
                          + _sigmoid(gc[...].astype(F32)) * br_g[:, cols])
    mix = jnp.dot(o_ref[...].astype(BF16), wob_ref[...], preferred_element_type=F32)
    o_ref[...] = _ln(ALPHA * x_ref[...] + mod_ref[2] * mix, lg_ref[...], lb_ref[...])


def _merge(l, x, mods, z, yc, ug, att_p, att_s, conv_pw, na_out, gm_out, w_o, ln_g, ln_b):
    def gz(k):
        return pl.BlockSpec((TBM, COL_BLK), lambda i: (i, CB_GZ + k))

    def w(k, n):
        return pl.BlockSpec((None, k, n), lambda i: (l, 0, 0))

    blk512 = pl.BlockSpec((TBM, COL_BLK), lambda i: (i, 0))
    return pl.pallas_call(
        _merge_kernel,
        name="merge",
        grid=(N_TBM,),
        in_specs=[
            pl.BlockSpec((TBM, D_MODEL), lambda i: (i, 0)),
            pl.BlockSpec((None, 6, None, 1, D_MODEL),
                         lambda i: (l, 0, _mod_row(i, TBM_PER_SAMPLE, N_TBM_PROMPT), 0, 0)),
            gz(0), gz(1), gz(2), gz(3), gz(4), gz(5),
            blk512, blk512,
            pl.BlockSpec((TBM, D_NA), lambda i: (jnp.minimum(i, N_TBM_PROMPT - 1), 0)),
            pl.BlockSpec((TBM, D_NA), lambda i: (jnp.maximum(i - N_TBM_PROMPT, 0), 0)),
            w(D_CONV, D_MODEL), w(D_NA, D_MODEL), w(D_GM, D_MODEL), w(D_MODEL, D_MODEL),
            w(1, D_MODEL), w(1, D_MODEL),
        ],
        out_specs=pl.BlockSpec((TBM, D_MODEL), lambda i: (i, 0)),
        out_shape=jax.ShapeDtypeStruct((T_ALL, D_MODEL), F32),
        scratch_shapes=[pltpu.VMEM((D_CONV, D_MODEL), BF16), pltpu.VMEM((D_NA, D_MODEL), BF16),
                        pltpu.VMEM((D_GM, D_MODEL), BF16), pltpu.VMEM((D_MODEL, D_MODEL), BF16)],
        compiler_params=_cparams(("arbitrary",)),
    )(x, mods, z, z, z, z, z, z, yc, ug, att_p, att_s, conv_pw, na_out, gm_out, w_o, ln_g, ln_b)


def _route(logits):
    lane = lax.broadcasted_iota(jnp.int32, logits.shape, 1)
    big = jnp.int32(ROUTER_LANES)
    is_g = lane < N_EGROUPS
    gl = jnp.where(is_g, logits, -jnp.inf)
    gmax = jnp.max(gl, axis=-1, keepdims=True)
    gidx = jnp.min(jnp.where(gl == gmax, lane, big), axis=-1, keepdims=True)
    gp = 1.0 / jnp.sum(jnp.where(is_g, jnp.exp(gl - gmax), 0.0), axis=-1, keepdims=True)
    lo = N_EGROUPS + gidx * EXP_PER_GROUP
    el = jnp.where(jnp.logical_and(lane >= lo, lane < lo + EXP_PER_GROUP), logits, -jnp.inf)
    v1 = jnp.max(el, axis=-1, keepdims=True)
    i1 = jnp.min(jnp.where(el == v1, lane, big), axis=-1, keepdims=True)
    el2 = jnp.where(lane == i1, -jnp.inf, el)
    v2 = jnp.max(el2, axis=-1, keepdims=True)
    i2 = jnp.min(jnp.where(el2 == v2, lane, big), axis=-1, keepdims=True)
    e2 = jnp.exp(v2 - v1)
    w1 = gp / (1.0 + e2)
    w2 = gp * e2 / (1.0 + e2)
    return jnp.where(lane == i1, w1, 0.0) + jnp.where(lane == i2, w2, 0.0)


def _split_bf16(a):
    hi = a.astype(BF16)
    return hi, (a - hi.astype(F32)).astype(BF16)


def _moe_kernel(x_ref, mod_ref, modn_ref, rw_ref, rb_ref, w1_ref, w3_ref, w2_ref, lg_ref, lb_ref,
                o_ref, *rest):
    n_scratch = 3
    hn_ref = rest[0] if len(rest) == n_scratch + 1 else None
    t_ref, gate_ref, acc_ref = rest[-n_scratch:]
    e = pl.program_id(1)

    @pl.when(e == 0)
    def _():
        t = x_ref[...] * (1.0 + mod_ref[4]) + mod_ref[3]
        t_hi, t_lo = _split_bf16(t)
        t_ref[...] = t_hi
        logits = (jnp.dot(t_hi, rw_ref[0], preferred_element_type=F32)
                  + jnp.dot(t_lo, rw_ref[0], preferred_element_type=F32)
                  + jnp.dot(t_hi, rw_ref[1], preferred_element_type=F32))
        gate_ref[...] = _route(logits + rb_ref[...])
        acc_ref[...] = jnp.zeros_like(acc_ref)

    t = t_ref[...]
    gate = gate_ref[...]
    lane = lax.broadcasted_iota(jnp.int32, gate.shape, 1)
    hcol = lax.broadcasted_iota(jnp.int32, (t.shape[0], EXP_SUB * D_EXPERT), 1) // D_EXPERT
    for k0 in range(0, EXP_STEP, EXP_SUB):
        ks = range(k0, k0 + EXP_SUB)
        w1 = jnp.concatenate([w1_ref[k] for k in ks], axis=1).astype(BF16)
        w3 = jnp.concatenate([w3_ref[k] for k in ks], axis=1).astype(BF16)
        w2 = w2_ref[k0:k0 + EXP_SUB].reshape(EXP_SUB * D_EXPERT, D_MODEL).astype(BF16)
        h1 = jnp.dot(t, w1, preferred_element_type=F32)
        h3 = jnp.dot(t, w3, preferred_element_type=F32)
        gmul = jnp.zeros(h1.shape, F32)
        for k in ks:
            gcol = jnp.sum(jnp.where(lane == e * EXP_STEP + k + N_EGROUPS, gate, 0.0),
                           axis=-1, keepdims=True)
            gmul = jnp.where(hcol == k - k0, gcol, gmul)
        hid = (h1 * _sigmoid(h1) * h3 * gmul).astype(BF16)
        acc_ref[...] += jnp.dot(hid, w2, preferred_element_type=F32)

    @pl.when(e == N_EXPERTS // EXP_STEP - 1)
    def _():
        y = _ln(ALPHA * x_ref[...] + mod_ref[5] * acc_ref[...], lg_ref[...], lb_ref[...])
        o_ref[...] = y
        if hn_ref is not None:
            hn_ref[...] = (y * (1.0 + modn_ref[1]) + modn_ref[0]).astype(BF16)


def _moe(l, x, mods, router_w, router_b, w1, w3, w2, ln_g, ln_b):
    n_m = T_ALL // TM_MOE
    bps = DEC_SEQ // TM_MOE
    npb = T_PROMPT // TM_MOE
    has_next = l + 1 < DEPTH
    l_next = l + 1 if has_next else l
    tok_blk = pl.BlockSpec((TM_MOE, D_MODEL), lambda i, e: (i, 0))
    out_shape = [jax.ShapeDtypeStruct((T_ALL, D_MODEL), F32)]
    if has_next:
        out_shape.append(jax.ShapeDtypeStruct((T_ALL, D_MODEL), BF16))
    return pl.pallas_call(
        _moe_kernel,
        name="moe",
        grid=(n_m, N_EXPERTS // EXP_STEP),
        in_specs=[
            tok_blk,
            pl.BlockSpec((None, 6, None, 1, D_MODEL),
                         lambda i, e: (l, 0, _mod_row(i, bps, npb), 0, 0)),
            pl.BlockSpec((None, 6, None, 1, D_MODEL),
                         lambda i, e: (l_next, 0, _mod_row(i, bps, npb), 0, 0)),
            pl.BlockSpec((None, 2, D_MODEL, ROUTER_LANES), lambda i, e: (l, 0, 0, 0)),
            pl.BlockSpec((None, 1, ROUTER_LANES), lambda i, e: (l, 0, 0)),
            pl.BlockSpec((None, EXP_STEP, D_MODEL, D_EXPERT), lambda i, e: (l, e, 0, 0)),
            pl.BlockSpec((None, EXP_STEP, D_MODEL, D_EXPERT), lambda i, e: (l, e, 0, 0)),
            pl.BlockSpec((None, EXP_STEP, D_EXPERT, D_MODEL), lambda i, e: (l, e, 0, 0)),
            pl.BlockSpec((None, 1, D_MODEL), lambda i, e: (l, 0, 0)),
            pl.BlockSpec((None, 1, D_MODEL), lambda i, e: (l, 0, 0)),
        ],
        out_specs=[tok_blk] * len(out_shape),
        out_shape=out_shape,
        scratch_shapes=[pltpu.VMEM((TM_MOE, D_MODEL), BF16),
                        pltpu.VMEM((TM_MOE, ROUTER_LANES), F32),
                        pltpu.VMEM((TM_MOE, D_MODEL), F32)],
        compiler_params=_cparams(("arbitrary", "arbitrary")),
    )(x, mods, mods, router_w, router_b, w1, w3, w2, ln_g, ln_b)


def _rope_tables():
    t = np.arange(DEC_SEQ)
    pos = np.stack([t // GRID_W, t % GRID_W], axis=1).astype(np.float32)
    quarter = HEAD_DIM // 4
    d = np.arange(HEAD_DIM)
    axis = d // (HEAD_DIM // 2)
    freq = d % quarter
    upper = (d % (HEAD_DIM // 2)) >= quarter
    inv = jnp.asarray(ROPE_BASE, F32) ** (-jnp.arange(0, HEAD_DIM // 2, 2, dtype=F32) / (HEAD_DIM // 2))
    ang = jnp.asarray(pos)[:, axis] * inv[freq][None, :]
    cos = jnp.cos(ang)
    sin = jnp.sin(ang)
    sin_up = jnp.where(upper[None, :], 0.0, -sin)
    sin_dn = jnp.where(upper[None, :], sin, 0.0)
    tile = lambda a: jnp.tile(a, (1, N_HEADS))
    return tile(cos), tile(sin_up), tile(sin_dn)


def _bias_tables(na_rpb):
    qc = np.arange(GRID_W)[:, None]
    kc = np.arange(GRID_W)[None, :]
    start = np.clip(qc - WIN_W // 2, 0, GRID_W - WIN_W)
    valid = (kc >= start) & (kc < start + WIN_W)
    n_dc = 2 * WIN_W - 1
    dc = kc - qc + WIN_W - 1
    hit = ((dc[None] == np.arange(n_dc)[:, None, None]) & valid[None]).astype(np.float32)
    sel = np.zeros((2 * n_dc + 1, GRID_W, 2 * GRID_W), np.float32)
    sel[:n_dc, :, :GRID_W] = hit
    sel[n_dc:2 * n_dc, :, GRID_W:] = hit
    sel[2 * n_dc] = np.where(np.concatenate([valid, valid], axis=1), 0.0, NEG_INF)
    ones = jnp.ones(na_rpb.shape[:2] + (2 * WIN_H - 2, 1), F32)
    rows = jnp.concatenate([na_rpb[:, :, :-1], na_rpb[:, :, 1:], ones], axis=-1)
    return jnp.einsum('lhrd,dqk->lhrqk', rows, jnp.asarray(sel), precision=HIGHEST)


def kernel(x_prompt, x_sample, cache_na_k, cache_na_v, c, c_ctx, w_ada, b_ada, w_in, conv_dw,
           conv_b, conv_ln_g, conv_ln_b, conv_pw, na_rpb, na_out, gm_ln_g, gm_ln_b, gm_ws, gm_bs,
           gm_out, w_o, ln1_g, ln1_b, rg_w, rg_b, re_w, re_b, moe_w1, moe_w3, moe_w2, ln2_g, ln2_b):
    x = jnp.concatenate([x_prompt.reshape(T_PROMPT, D_MODEL),
                         x_sample.reshape(T_SAMPLE, D_MODEL)], axis=0)

    cond = jnp.zeros((N_COND, D_MODEL), F32).at[0].set(c_ctx).at[1:1 + DEC_BATCH].set(c)
    mods = _ada(cond, w_ada, b_ada)
    mods = mods.reshape(DEPTH, N_COND, 6, 1, D_MODEL).transpose(0, 2, 1, 3, 4)

    gm_ws_b = gm_ws.astype(BF16)
    vec = lambda a: a.reshape(DEPTH, 1, a.shape[-1])
    gm_bs_t = gm_bs.transpose(0, 2, 1)
    router_w = jnp.concatenate(
        [rg_w, re_w.transpose(0, 2, 1, 3).reshape(DEPTH, D_MODEL, N_EXPERTS)], axis=-1)
    router_w = jnp.pad(router_w, ((0, 0), (0, 0), (0, ROUTER_LANES - N_EGROUPS - N_EXPERTS)))
    router_hi = router_w.astype(BF16)
    router_lo = (router_w - router_hi.astype(F32)).astype(BF16)
    router_w = jnp.stack([router_hi, router_lo], axis=1)
    router_b = jnp.concatenate([rg_b, re_b.reshape(DEPTH, N_EXPERTS)], axis=-1)
    router_b = jnp.pad(router_b, ((0, 0), (0, ROUTER_LANES - N_EGROUPS - N_EXPERTS)))
    router_b = router_b.reshape(DEPTH, 1, ROUTER_LANES)
    cache_k = cache_na_k.reshape(DEC_BATCH, DEPTH, PAST_LEN, D_NA)
    cache_v = cache_na_v.reshape(DEC_BATCH, DEPTH, PAST_LEN, D_NA)
    tz = _bias_tables(na_rpb)
    cos, sin_up, sin_dn = _rope_tables()

    kt_all = vt_all = None
    h = _modulate(0, x, mods)
    for l in range(DEPTH):
        z = _inproj(l, h, w_in, lambda j: jnp.where(j < W_CB_KV, j, j + N_CB_KV), N_CB_MAIN, BF16)
        zkv = _inproj(l, h, w_in, lambda j: j + W_CB_KV, N_CB_KV, F32)
        yc, ug = _branches(l, z, conv_dw, vec(conv_b), vec(conv_ln_g), vec(conv_ln_b),
                           vec(gm_ln_g), vec(gm_ln_b), gm_ws_b, gm_bs_t)
        att_p, kt_all, vt_all = _ctx_attn(l, z, zkv, kt_all, vt_all)
        att_s = _na_attn(l, z, zkv, cache_k, cache_v, tz, cos, sin_up, sin_dn)
        x = _merge(l, x, mods, z, yc, ug, att_p, att_s, conv_pw, na_out, gm_out, w_o,
                   vec(ln1_g), vec(ln1_b))
        outs = _moe(l, x, mods, router_w, router_b, moe_w1, moe_w3, moe_w2, vec(ln2_g), vec(ln2_b))
        x, h = outs[0], outs[-1]

    y_prompt = x[:T_PROMPT].reshape(BATCH, SEQ, D_MODEL)
    y_sample = x[T_PROMPT:].reshape(DEC_BATCH, DEC_SEQ, D_MODEL)
    return (y_prompt, y_sample, kt_all.transpose(0, 1, 4, 2, 3), vt_all.transpose(0, 1, 4, 2, 3))
```

```python
import functools

import jax
import jax.numpy as jnp
import numpy as np
from jax import lax
from jax.experimental import pallas as pl
from jax.experimental.pallas import tpu as pltpu

F32 = jnp.float32
BF16 = jnp.bfloat16
HIGHEST = lax.Precision.HIGHEST

D_MODEL = 1024
BATCH = 16
SEQ = 256
DEPTH = 4
DEC_BATCH = 2
DEC_SEQ = 1024
PAST_LEN = 256
GRID_W = 64
GRID_H = DEC_SEQ // GRID_W
D_CONV = 512
CONV_WIDTH = 31
CONV_HALF = CONV_WIDTH // 2
HEAD_DIM = 64
HEAD_PAIR = 2 * HEAD_DIM
N_HEADS = 8
D_NA = 512
WIN_H = 8
WIN_W = 16
ROPE_BASE = 10000.0
D_GM = 512
GM_CHUNK = 128
GM_GROUPS = 4
D_IN = 6656
N_EGROUPS = 4
EXP_PER_GROUP = 8
N_EXPERTS = 32
D_EXPERT = 128
ALPHA = (2 * DEPTH) ** 0.25
LN_EPS = 1e-5
NEG_INF = -1e30

T_PROMPT = BATCH * SEQ
T_SAMPLE = DEC_BATCH * DEC_SEQ
T_ALL = T_PROMPT + T_SAMPLE
N_COND = 8

COL_BLK = 512
W_CB_KV = 3
N_CB_KV = 2
N_CB_MAIN = D_IN // COL_BLK - N_CB_KV
CB_A, CB_B, CB_Q, CB_GU, CB_GV, CB_GZ = 0, 1, 2, 3, 4, 5
CB_K, CB_V = 0, 1

TB = 256
N_TB = T_ALL // TB
N_TB_PROMPT = T_PROMPT // TB
TB_PER_SAMPLE = DEC_SEQ // TB
TBM = 512
N_TBM = T_ALL // TBM
N_TBM_PROMPT = T_PROMPT // TBM
TBM_PER_SAMPLE = DEC_SEQ // TBM
HALO = 16
CONV_ROWS = 32
SUBLANES = 8
SHIFT_ROWS = TB + 2 * HALO - SUBLANES

NA_ROWS = 4
NA_TQ = NA_ROWS * GRID_W
TM_IN = 2048
TM_MOE = 1024
EXP_STEP = 4
EXP_SUB = 4
ROUTER_LANES = 128
VMEM_LIMIT = 56 * 1024 * 1024


def _ln(x, g, b):
    mu = jnp.mean(x, axis=-1, keepdims=True)
    xc = x - mu
    var = jnp.mean(xc * xc, axis=-1, keepdims=True)
    return xc * lax.rsqrt(var + LN_EPS) * g + b


def _sigmoid(x):
    return jax.nn.sigmoid(x)


def _gelu(x):
    return jax.nn.gelu(x, approximate=True)


_NT = (((1,), (1,)), ((), ()))


def _cparams(sem):
    return pltpu.CompilerParams(dimension_semantics=sem, vmem_limit_bytes=VMEM_LIMIT)


def _ada_kernel(c_ref, w_ref, b_ref, o_ref):
    c = c_ref[...]
    s = c * _sigmoid(c)
    w = w_ref[...]
    s_hi, w_hi = s.astype(BF16), w.astype(BF16)
    s_lo = (s - s_hi.astype(F32)).astype(BF16)
    w_lo = (w - w_hi.astype(F32)).astype(BF16)
    o_ref[...] = (jnp.dot(s_hi, w_hi, preferred_element_type=F32)
                  + jnp.dot(s_lo, w_hi, preferred_element_type=F32)
                  + jnp.dot(s_hi, w_lo, preferred_element_type=F32)) + b_ref[...]


def _ada(cond, w_ada, b_ada):
    tn = 2048
    return pl.pallas_call(
        _ada_kernel,
        name="ada",
        grid=(DEPTH, 6 * D_MODEL // tn),
        in_specs=[
            pl.BlockSpec((N_COND, D_MODEL), lambda l, j: (0, 0)),
            pl.BlockSpec((None, D_MODEL, tn), lambda l, j: (l, 0, j)),
            pl.BlockSpec((None, 1, tn), lambda l, j: (l, 0, j)),
        ],
        out_specs=pl.BlockSpec((None, N_COND, tn), lambda l, j: (l, 0, j)),
        out_shape=jax.ShapeDtypeStruct((DEPTH, N_COND, 6 * D_MODEL), F32),
        compiler_params=_cparams(("arbitrary", "arbitrary")),
    )(cond, w_ada, b_ada.reshape(DEPTH, 1, 6 * D_MODEL))


def _mod_row(i, blocks_per_sample, n_prompt_blocks):
    return jnp.where(i < n_prompt_blocks, 0, 1 + (i - n_prompt_blocks) // blocks_per_sample)


def _gather_modulate_kernel(xp_ref, xs_ref, mod_ref, x_ref, h_ref):
    x = jnp.where(pl.program_id(0) < T_PROMPT // TM_MOE, xp_ref[...], xs_ref[...])
    x_ref[...] = x
    h_ref[...] = (x * (1.0 + mod_ref[1]) + mod_ref[0]).astype(BF16)


def _gather_modulate(x_prompt, x_sample, mods):
    bps = DEC_SEQ // TM_MOE
    npb = T_PROMPT // TM_MOE
    tok_blk = pl.BlockSpec((TM_MOE, D_MODEL), lambda i: (i, 0))
    return pl.pallas_call(
        _gather_modulate_kernel,
        name="modulate",
        grid=(T_ALL // TM_MOE,),
        in_specs=[
            pl.BlockSpec((TM_MOE, D_MODEL), lambda i: (jnp.minimum(i, npb - 1), 0)),
            pl.BlockSpec((TM_MOE, D_MODEL), lambda i: (jnp.maximum(i - npb, 0), 0)),
            pl.BlockSpec((None, 6, None, 1, D_MODEL),
                         lambda i: (0, 0, _mod_row(i, bps, npb), 0, 0)),
        ],
        out_specs=[tok_blk, tok_blk],
        out_shape=[jax.ShapeDtypeStruct((T_ALL, D_MODEL), F32),
                   jax.ShapeDtypeStruct((T_ALL, D_MODEL), BF16)],
        compiler_params=_cparams(("arbitrary",)),
    )(x_prompt.reshape(T_PROMPT, D_MODEL), x_sample.reshape(T_SAMPLE, D_MODEL), mods)


def _inproj_kernel(h_ref, w_ref, z_ref):
    rows = pl.ds(pl.multiple_of(pl.program_id(1) * TM_IN, TM_IN), TM_IN)
    z = jnp.dot(h_ref[rows, :], w_ref[...].astype(BF16), preferred_element_type=F32)
    z_ref[...] = z.astype(z_ref.dtype)


def _inproj(l, h, w_in, wcol, n_cols, out_dtype):
    return pl.pallas_call(
        _inproj_kernel,
        name="inproj",
        grid=(n_cols, T_ALL // TM_IN),
        in_specs=[
            pl.BlockSpec((T_ALL, D_MODEL), lambda j, i: (0, 0)),
            pl.BlockSpec((None, D_MODEL, COL_BLK), lambda j, i: (l, 0, wcol(j))),
        ],
        out_specs=pl.BlockSpec((TM_IN, COL_BLK), lambda j, i: (i, j)),
        out_shape=jax.ShapeDtypeStruct((T_ALL, n_cols * COL_BLK), out_dtype),
        compiler_params=_cparams(("arbitrary", "arbitrary")),
    )(h, w_in)


def _branch_kernel(ap_ref, ac_ref, an_ref, bp_ref, bc_ref, bn_ref, gu_ref, gv_ref,
                   dw_ref, cb_ref, clg_ref, clb_ref, glg_ref, glb_ref, ws_ref, bst_ref,
                   yc_ref, ug_ref, ypad_ref, ysh_ref):
    i = pl.program_id(0)
    j = i - N_TB_PROMPT
    in_sample = i >= N_TB_PROMPT
    has_prev = jnp.logical_and(in_sample, j % TB_PER_SAMPLE != 0)
    has_next = jnp.logical_and(in_sample, j % TB_PER_SAMPLE != TB_PER_SAMPLE - 1)

    def glu(a_ref, b_ref):
        return a_ref[...].astype(F32) * _sigmoid(b_ref[...].astype(F32))

    ypad_ref[0:HALO, :] = jnp.where(has_prev, glu(ap_ref, bp_ref), 0.0)
    ypad_ref[HALO:HALO + TB, :] = glu(ac_ref, bc_ref)
    ypad_ref[HALO + TB:HALO + TB + HALO, :] = jnp.where(has_next, glu(an_ref, bn_ref), 0.0)

    for b in range(SUBLANES):
        ysh_ref[b] = ypad_ref[b:b + SHIFT_ROWS, :]

    off = HALO - CONV_HALF
    for c in range(TB // CONV_ROWS):
        base = c * CONV_ROWS
        acc = jnp.zeros((CONV_ROWS, D_CONV), F32)
        for k in range(CONV_WIDTH):
            tile, phase = divmod(off + k, SUBLANES)
            start = base + tile * SUBLANES
            acc = acc + ysh_ref[phase, start:start + CONV_ROWS, :] * dw_ref[k:k + 1, :]
        y = _ln(acc + cb_ref[...], clg_ref[...], clb_ref[...])
        yc_ref[base:base + CONV_ROWS, :] = (y * _sigmoid(y)).astype(BF16)

    for n in range(TB // GM_CHUNK):
        rows = slice(n * GM_CHUNK, (n + 1) * GM_CHUNK)
        u = _gelu(gu_ref[rows, :].astype(F32))
        v = _ln(_gelu(gv_ref[rows, :].astype(F32)), glg_ref[...], glb_ref[...]).astype(BF16)
        for g in range(GM_GROUPS):
            cols = slice(g * GM_CHUNK, (g + 1) * GM_CHUNK)
            sv = jnp.dot(ws_ref[g], v[:, cols], preferred_element_type=F32) + bst_ref[:, g:g + 1]
            ug_ref[rows, cols] = (u[:, cols] * sv).astype(BF16)


def _branches(l, z, conv_dw, conv_b, conv_ln_g, conv_ln_b, gm_ln_g, gm_ln_b, gm_ws, gm_bs_t):
    halo_per_tb = TB // HALO
    n_halo = T_ALL // HALO

    def cur(cb):
        return pl.BlockSpec((TB, COL_BLK), lambda i: (i, cb))

    def prev(cb):
        return pl.BlockSpec((HALO, COL_BLK), lambda i: (jnp.maximum(i * halo_per_tb - 1, 0), cb))

    def nxt(cb):
        return pl.BlockSpec((HALO, COL_BLK),
                            lambda i: (jnp.minimum((i + 1) * halo_per_tb, n_halo - 1), cb))

    def vec(n):
        return pl.BlockSpec((None, 1, n), lambda i: (l, 0, 0))

    return pl.pallas_call(
        _branch_kernel,
        name="branches",
        grid=(N_TB,),
        in_specs=[
            prev(CB_A), cur(CB_A), nxt(CB_A), prev(CB_B), cur(CB_B), nxt(CB_B),
            cur(CB_GU), cur(CB_GV),
            pl.BlockSpec((None, CONV_WIDTH, D_CONV), lambda i: (l, 0, 0)),
            vec(D_CONV), vec(D_CONV), vec(D_CONV), vec(D_GM), vec(D_GM),
            pl.BlockSpec((None, GM_GROUPS, GM_CHUNK, GM_CHUNK), lambda i: (l, 0, 0, 0)),
            pl.BlockSpec((None, GM_CHUNK, GM_GROUPS), lambda i: (l, 0, 0)),
        ],
        out_specs=[pl.BlockSpec((TB, D_CONV), lambda i: (i, 0)),
                   pl.BlockSpec((TB, D_GM), lambda i: (i, 0))],
        out_shape=[jax.ShapeDtypeStruct((T_ALL, D_CONV), BF16),
                   jax.ShapeDtypeStruct((T_ALL, D_GM), BF16)],
        scratch_shapes=[pltpu.VMEM((TB + 2 * HALO, D_CONV), F32),
                        pltpu.VMEM((SUBLANES, SHIFT_ROWS, D_CONV), F32)],
        compiler_params=_cparams(("arbitrary",)),
    )(z, z, z, z, z, z, z, z, conv_dw, conv_b, conv_ln_g, conv_ln_b, gm_ln_g, gm_ln_b,
      gm_ws, gm_bs_t)


def _ctx_attn_kernel(first, q_ref, k_ref, v_ref, *rest):
    o_ref, ko_ref, vo_ref = rest[-3:]
    kt = k_ref[...].T
    vt = v_ref[...].T
    if first:
        ko_ref[0] = kt.reshape(N_HEADS, HEAD_DIM, SEQ)
        vo_ref[0] = vt.reshape(N_HEADS, HEAD_DIM, SEQ)
        ko_ref[1:] = jnp.zeros((DEPTH - 1, N_HEADS, HEAD_DIM, SEQ), F32)
        vo_ref[1:] = jnp.zeros((DEPTH - 1, N_HEADS, HEAD_DIM, SEQ), F32)
    else:
        ko_ref[...] = kt.reshape(N_HEADS, HEAD_DIM, SEQ)
        vo_ref[...] = vt.reshape(N_HEADS, HEAD_DIM, SEQ)
    q = (q_ref[...].astype(F32) * HEAD_DIM ** -0.5).astype(BF16)
    kb = kt.astype(BF16)
    vb = vt.astype(BF16)
    lower = lax.broadcasted_iota(jnp.int32, (SEQ, HEAD_PAIR), 1) < HEAD_DIM
    upper = jnp.logical_not(lower)
    heads = range(N_HEADS)
    grp = [slice(h // 2 * HEAD_PAIR, (h // 2 + 1) * HEAD_PAIR) for h in heads]
    qh = [jnp.where(lower if h % 2 == 0 else upper, q[:, grp[h]],
                    jnp.zeros((SEQ, HEAD_PAIR), BF16)) for h in heads]
    s = [jnp.dot(qh[h], kb[grp[h], :], preferred_element_type=F32) for h in heads]
    m = [jnp.max(s[h], axis=-1, keepdims=True) for h in heads]
    p = [jnp.exp(s[h] - m[h]) for h in heads]
    den = [jnp.sum(p[h], axis=-1, keepdims=True) for h in heads]
    o = [lax.dot_general(p[h].astype(BF16), vb[grp[h], :], _NT, preferred_element_type=F32) / den[h]
         for h in heads]
    for h in range(0, N_HEADS, 2):
        o_ref[:, grp[h]] = jnp.where(lower, o[h], o[h + 1]).astype(BF16)


def _ctx_attn(l, z, zkv, kt_all=None, vt_all=None):
    first = kt_all is None
    if first:
        cache_blk = pl.BlockSpec((None, DEPTH, N_HEADS, HEAD_DIM, SEQ), lambda b: (b, 0, 0, 0, 0))
        carried, carried_specs, aliases = (), [], {}
    else:
        cache_blk = pl.BlockSpec((None, None, N_HEADS, HEAD_DIM, SEQ), lambda b: (b, l, 0, 0, 0))
        carried = (kt_all, vt_all)
        carried_specs = [pl.BlockSpec(memory_space=pl.ANY), pl.BlockSpec(memory_space=pl.ANY)]
        aliases = {3: 1, 4: 2}
    cache_shape = jax.ShapeDtypeStruct((BATCH, DEPTH, N_HEADS, HEAD_DIM, SEQ), F32)
    return pl.pallas_call(
        functools.partial(_ctx_attn_kernel, first),
        name="ctx_attn",
        grid=(BATCH,),
        in_specs=[pl.BlockSpec((SEQ, COL_BLK), lambda b: (b, CB_Q)),
                  pl.BlockSpec((SEQ, COL_BLK), lambda b: (b, CB_K)),
                  pl.BlockSpec((SEQ, COL_BLK), lambda b: (b, CB_V))] + carried_specs,
        out_specs=[pl.BlockSpec((SEQ, D_NA), lambda b: (b, 0)), cache_blk, cache_blk],
        out_shape=[jax.ShapeDtypeStruct((T_PROMPT, D_NA), BF16), cache_shape, cache_shape],
        input_output_aliases=aliases,
        compiler_params=_cparams(("arbitrary",)),
    )(z, zkv, zkv, *carried)


def _rope(x, cos, sin_up, sin_dn):
    return (x * cos + pltpu.roll(x, D_NA - HEAD_DIM // 4, 1) * sin_up
            + pltpu.roll(x, HEAD_DIM // 4, 1) * sin_dn)


def _na_attn_kernel(q_ref, k_ref, v_ref, ck_ref, cv_ref, tz_ref, cos_ref, sup_ref, sdn_ref,
                    o_ref, krot_ref, vb_ref, ckb_ref, cvb_ref):
    step = pl.program_id(1)

    @pl.when(step == 0)
    def _():
        krot_ref[...] = _rope(k_ref[...], cos_ref[...], sup_ref[...], sdn_ref[...]).astype(BF16)
        vb_ref[...] = v_ref[...].astype(BF16)
        ckb_ref[...] = ck_ref[...].astype(BF16)
        cvb_ref[...] = cv_ref[...].astype(BF16)

    qrows = pl.ds(pl.multiple_of(step * NA_TQ, NA_TQ), NA_TQ)
    q = _rope(q_ref[...].astype(F32), cos_ref[qrows, :], sup_ref[qrows, :], sdn_ref[qrows, :])
    q = (q * HEAD_DIM ** -0.5).astype(BF16)

    lower = lax.broadcasted_iota(jnp.int32, (GRID_W, HEAD_PAIR), 1) < HEAD_DIM
    upper = jnp.logical_not(lower)
    kwin, vwin, dr0 = [], [], []
    for j in range(NA_ROWS):
        r = step * NA_ROWS + j
        row_start = jnp.clip(r - WIN_H // 2, 0, GRID_H - WIN_H)
        krows = pl.ds(pl.multiple_of(row_start * GRID_W, GRID_W), WIN_H * GRID_W)
        kwin.append(krot_ref[krows, :])
        vwin.append(vb_ref[krows, :])
        dr0.append(row_start - r + WIN_H - 1)

    units = [(j, h) for j in range(NA_ROWS) for h in range(N_HEADS)]
    grp = [slice(h // 2 * HEAD_PAIR, (h // 2 + 1) * HEAD_PAIR) for _, h in units]
    qh = [jnp.where(lower if h % 2 == 0 else upper, q[j * GRID_W:(j + 1) * GRID_W, grp[u]],
                    jnp.zeros((GRID_W, HEAD_PAIR), BF16)) for u, (j, h) in enumerate(units)]
    s_loc = [lax.dot_general(qh[u], kwin[j][:, grp[u]], _NT, preferred_element_type=F32)
             + jnp.concatenate([tz_ref[h, dr0[j] + w] for w in range(0, WIN_H, 2)], axis=1)
             for u, (j, h) in enumerate(units)]
    s_ctx = [lax.dot_general(qh[u], ckb_ref[:, grp[u]], _NT, preferred_element_type=F32)
             for u in range(len(units))]
    m = [jnp.maximum(jnp.max(s_loc[u], axis=-1, keepdims=True),
                     jnp.max(s_ctx[u], axis=-1, keepdims=True)) for u in range(len(units))]
    p_loc = [jnp.exp(s_loc[u] - m[u]) for u in range(len(units))]
    p_ctx = [jnp.exp(s_ctx[u] - m[u]) for u in range(len(units))]
    den = [jnp.sum(p_loc[u], axis=-1, keepdims=True) + jnp.sum(p_ctx[u], axis=-1, keepdims=True)
           for u in range(len(units))]
    o = [(jnp.dot(p_loc[u].astype(BF16), vwin[j][:, grp[u]], preferred_element_type=F32)
          + jnp.dot(p_ctx[u].astype(BF16), cvb_ref[:, grp[u]], preferred_element_type=F32)) / den[u]
         for u, (j, h) in enumerate(units)]
    for u, (j, h) in enumerate(units):
        if h % 2 == 0:
            o_ref[j * GRID_W:(j + 1) * GRID_W, grp[u]] = jnp.where(lower, o[u], o[u + 1]).astype(BF16)


def _na_attn(l, z, zkv, cache_k, cache_v, tz, cos, sin_up, sin_dn):
    seq_blk0 = T_PROMPT // DEC_SEQ
    row_blk0 = T_PROMPT // NA_TQ
    steps = GRID_H // NA_ROWS
    full = pl.BlockSpec((DEC_SEQ, D_NA), lambda b, r: (0, 0))
    return pl.pallas_call(
        _na_attn_kernel,
        name="na_attn",
        grid=(DEC_BATCH, steps),
        in_specs=[
            pl.BlockSpec((NA_TQ, COL_BLK), lambda b, r: (row_blk0 + b * steps + r, CB_Q)),
            pl.BlockSpec((DEC_SEQ, COL_BLK), lambda b, r: (seq_blk0 + b, CB_K)),
            pl.BlockSpec((DEC_SEQ, COL_BLK), lambda b, r: (seq_blk0 + b, CB_V)),
            pl.BlockSpec((None, None, PAST_LEN, D_NA), lambda b, r: (b, l, 0, 0)),
            pl.BlockSpec((None, None, PAST_LEN, D_NA), lambda b, r: (b, l, 0, 0)),
            pl.BlockSpec((None, N_HEADS, 2 * WIN_H - 2, GRID_W, 2 * GRID_W),
                         lambda b, r: (l, 0, 0, 0, 0)),
            full, full, full,
        ],
        out_specs=pl.BlockSpec((NA_TQ, D_NA), lambda b, r: (b * steps + r, 0)),
        out_shape=jax.ShapeDtypeStruct((T_SAMPLE, D_NA), BF16),
        scratch_shapes=[pltpu.VMEM((DEC_SEQ, D_NA), BF16), pltpu.VMEM((DEC_SEQ, D_NA), BF16),
                        pltpu.VMEM((PAST_LEN, D_NA), BF16), pltpu.VMEM((PAST_LEN, D_NA), BF16)],
        compiler_params=_cparams(("arbitrary", "arbitrary")),
    )(z, zkv, zkv, cache_k, cache_v, tz, cos, sin_up, sin_dn)


def _merge_kernel(x_ref, mod_ref, g0a, g0b, g1a, g1b, g2a, g2b, yc_ref, ug_ref, ap_ref, as_ref,
                  pw_ref, no_ref, go_ref, wo_ref, lg_ref, lb_ref, o_ref,
                  pwb_ref, nob_ref, gob_ref, wob_ref):
    i = pl.program_id(0)

    @pl.when(i == 0)
    def _():
        pwb_ref[...] = pw_ref[...].astype(BF16)
        nob_ref[...] = no_ref[...].astype(BF16)
        gob_ref[...] = go_ref[...].astype(BF16)
        wob_ref[...] = wo_ref[...].astype(BF16)

    att = jnp.where(i < N_TBM_PROMPT, ap_ref[...], as_ref[...])
    br_c = jnp.dot(yc_ref[...], pwb_ref[...], preferred_element_type=F32)
    br_a = jnp.dot(att, nob_ref[...], preferred_element_type=F32)
    br_g = jnp.dot(ug_ref[...], gob_ref[...], preferred_element_type=F32)
    h = COL_BLK
    for lo, ga, gb, gc in ((0, g0a, g1a, g2a), (h, g0b, g1b, g2b)):
        cols = slice(lo, lo + h)
        o_ref[:, cols] = (_sigmoid(ga[...].astype(F32)) * br_c[:, cols]
                          + _sigmoid(gb[...].astype(F32)) * br_a[:, cols]
                          + _sigmoid(gc[...].astype(F32)) * br_g[:, cols])
    mix = jnp.dot(o_ref[...].astype(BF16), wob_ref[...], preferred_element_type=F32)
    o_ref[...] = _ln(ALPHA * x_ref[...] + mod_ref[2] * mix, lg_ref[...], lb_ref[...])


def _merge(l, x, mods, z, yc, ug, att_p, att_s, conv_pw, na_out, gm_out, w_o, ln_g, ln_b):
    def gz(k):
        return pl.BlockSpec((TBM, COL_BLK), lambda i: (i, CB_GZ + k))

    def w(k, n):
        return pl.BlockSpec((None, k, n), lambda i: (l, 0, 0))

    blk512 = pl.BlockSpec((TBM, COL_BLK), lambda i: (i, 0))
    return pl.pallas_call(
        _merge_kernel,
        name="merge",
        grid=(N_TBM,),
        in_specs=[
            pl.BlockSpec((TBM, D_MODEL), lambda i: (i, 0)),
            pl.BlockSpec((None, 6, None, 1, D_MODEL),
                         lambda i: (l, 0, _mod_row(i, TBM_PER_SAMPLE, N_TBM_PROMPT), 0, 0)),
            gz(0), gz(1), gz(2), gz(3), gz(4), gz(5),
            blk512, blk512,
            pl.BlockSpec((TBM, D_NA), lambda i: (jnp.minimum(i, N_TBM_PROMPT - 1), 0)),
            pl.BlockSpec((TBM, D_NA), lambda i: (jnp.maximum(i - N_TBM_PROMPT, 0), 0)),
            w(D_CONV, D_MODEL), w(D_NA, D_MODEL), w(D_GM, D_MODEL), w(D_MODEL, D_MODEL),
            w(1, D_MODEL), w(1, D_MODEL),
        ],
        out_specs=pl.BlockSpec((TBM, D_MODEL), lambda i: (i, 0)),
        out_shape=jax.ShapeDtypeStruct((T_ALL, D_MODEL), F32),
        scratch_shapes=[pltpu.VMEM((D_CONV, D_MODEL), BF16), pltpu.VMEM((D_NA, D_MODEL), BF16),
                        pltpu.VMEM((D_GM, D_MODEL), BF16), pltpu.VMEM((D_MODEL, D_MODEL), BF16)],
        compiler_params=_cparams(("arbitrary",)),
    )(x, mods, z, z, z, z, z, z, yc, ug, att_p, att_s, conv_pw, na_out, gm_out, w_o, ln_g, ln_b)


def _route(logits):
    lane = lax.broadcasted_iota(jnp.int32, logits.shape, 1)
    big = jnp.int32(ROUTER_LANES)
    is_g = lane < N_EGROUPS
    gl = jnp.where(is_g, logits, -jnp.inf)
    gmax = jnp.max(gl, axis=-1, keepdims=True)
    gidx = jnp.min(jnp.where(gl == gmax, lane, big), axis=-1, keepdims=True)
    gp = 1.0 / jnp.sum(jnp.where(is_g, jnp.exp(gl - gmax), 0.0), axis=-1, keepdims=True)
    lo = N_EGROUPS + gidx * EXP_PER_GROUP
    el = jnp.where(jnp.logical_and(lane >= lo, lane < lo + EXP_PER_GROUP), logits, -jnp.inf)
    v1 = jnp.max(el, axis=-1, keepdims=True)
    i1 = jnp.min(jnp.where(el == v1, lane, big), axis=-1, keepdims=True)
    el2 = jnp.where(lane == i1, -jnp.inf, el)
    v2 = jnp.max(el2, axis=-1, keepdims=True)
    i2 = jnp.min(jnp.where(el2 == v2, lane, big), axis=-1, keepdims=True)
    e2 = jnp.exp(v2 - v1)
    w1 = gp / (1.0 + e2)
    w2 = gp * e2 / (1.0 + e2)
    return jnp.where(lane == i1, w1, 0.0) + jnp.where(lane == i2, w2, 0.0)


def _split_bf16(a):
    hi = a.astype(BF16)
    return hi, (a - hi.astype(F32)).astype(BF16)


def _moe_kernel(last, x_ref, mod_ref, modn_ref, rw_ref, rb_ref, w1_ref, w3_ref, w2_ref, lg_ref,
                lb_ref, out0_ref, out1_ref, t_ref, gate_ref, acc_ref):
    i = pl.program_id(0)
    e = pl.program_id(1)

    @pl.when(e == 0)
    def _():
        t = x_ref[...] * (1.0 + mod_ref[4]) + mod_ref[3]
        t_hi, t_lo = _split_bf16(t)
        t_ref[...] = t_hi
        logits = (jnp.dot(t_hi, rw_ref[0], preferred_element_type=F32)
                  + jnp.dot(t_lo, rw_ref[0], preferred_element_type=F32)
                  + jnp.dot(t_hi, rw_ref[1], preferred_element_type=F32))
        gate_ref[...] = _route(logits + rb_ref[...])
        acc_ref[...] = jnp.zeros_like(acc_ref)

    t = t_ref[...]
    gate = gate_ref[...]
    lane = lax.broadcasted_iota(jnp.int32, gate.shape, 1)
    hcol = lax.broadcasted_iota(jnp.int32, (t.shape[0], EXP_SUB * D_EXPERT), 1) // D_EXPERT
    for k0 in range(0, EXP_STEP, EXP_SUB):
        ks = range(k0, k0 + EXP_SUB)
        w1 = jnp.concatenate([w1_ref[k] for k in ks], axis=1).astype(BF16)
        w3 = jnp.concatenate([w3_ref[k] for k in ks], axis=1).astype(BF16)
        w2 = w2_ref[k0:k0 + EXP_SUB].reshape(EXP_SUB * D_EXPERT, D_MODEL).astype(BF16)
        h1 = jnp.dot(t, w1, preferred_element_type=F32)
        h3 = jnp.dot(t, w3, preferred_element_type=F32)
        gmul = jnp.zeros(h1.shape, F32)
        for k in ks:
            gcol = jnp.sum(jnp.where(lane == e * EXP_STEP + k + N_EGROUPS, gate, 0.0),
                           axis=-1, keepdims=True)
            gmul = jnp.where(hcol == k - k0, gcol, gmul)
        hid = (h1 * _sigmoid(h1) * h3 * gmul).astype(BF16)
        acc_ref[...] += jnp.dot(hid, w2, preferred_element_type=F32)

    @pl.when(e == N_EXPERTS // EXP_STEP - 1)
    def _():
        y = _ln(ALPHA * x_ref[...] + mod_ref[5] * acc_ref[...], lg_ref[...], lb_ref[...])
        if last:
            @pl.when(i < T_PROMPT // TM_MOE)
            def _():
                out0_ref[...] = y

            @pl.when(i >= T_PROMPT // TM_MOE)
            def _():
                out1_ref[...] = y
        else:
            out0_ref[...] = y
            out1_ref[...] = (y * (1.0 + modn_ref[1]) + modn_ref[0]).astype(BF16)


def _moe(l, x, mods, router_w, router_b, w1, w3, w2, ln_g, ln_b):
    n_m = T_ALL // TM_MOE
    bps = DEC_SEQ // TM_MOE
    npb = T_PROMPT // TM_MOE
    last = l + 1 == DEPTH
    l_next = l if last else l + 1
    tok_blk = pl.BlockSpec((TM_MOE, D_MODEL), lambda i, e: (i, 0))
    if last:
        out_specs = [pl.BlockSpec((TM_MOE, D_MODEL), lambda i, e: (jnp.minimum(i, npb - 1), 0)),
                     pl.BlockSpec((TM_MOE, D_MODEL), lambda i, e: (jnp.maximum(i - npb, 0), 0))]
        out_shape = [jax.ShapeDtypeStruct((T_PROMPT, D_MODEL), F32),
                     jax.ShapeDtypeStruct((T_SAMPLE, D_MODEL), F32)]
    else:
        out_specs = [tok_blk, tok_blk]
        out_shape = [jax.ShapeDtypeStruct((T_ALL, D_MODEL), F32),
                     jax.ShapeDtypeStruct((T_ALL, D_MODEL), BF16)]
    return pl.pallas_call(
        functools.partial(_moe_kernel, last),
        name="moe",
        grid=(n_m, N_EXPERTS // EXP_STEP),
        in_specs=[
            tok_blk,
            pl.BlockSpec((None, 6, None, 1, D_MODEL),
                         lambda i, e: (l, 0, _mod_row(i, bps, npb), 0, 0)),
            pl.BlockSpec((None, 6, None, 1, D_MODEL),
                         lambda i, e: (l_next, 0, _mod_row(i, bps, npb), 0, 0)),
            pl.BlockSpec((None, 2, D_MODEL, ROUTER_LANES), lambda i, e: (l, 0, 0, 0)),
            pl.BlockSpec((None, 1, ROUTER_LANES), lambda i, e: (l, 0, 0)),
            pl.BlockSpec((None, EXP_STEP, D_MODEL, D_EXPERT), lambda i, e: (l, e, 0, 0)),
            pl.BlockSpec((None, EXP_STEP, D_MODEL, D_EXPERT), lambda i, e: (l, e, 0, 0)),
            pl.BlockSpec((None, EXP_STEP, D_EXPERT, D_MODEL), lambda i, e: (l, e, 0, 0)),
            pl.BlockSpec((None, 1, D_MODEL), lambda i, e: (l, 0, 0)),
            pl.BlockSpec((None, 1, D_MODEL), lambda i, e: (l, 0, 0)),
        ],
        out_specs=out_specs,
        out_shape=out_shape,
        scratch_shapes=[pltpu.VMEM((TM_MOE, D_MODEL), BF16),
                        pltpu.VMEM((TM_MOE, ROUTER_LANES), F32),
                        pltpu.VMEM((TM_MOE, D_MODEL), F32)],
        compiler_params=_cparams(("arbitrary", "arbitrary")),
    )(x, mods, mods, router_w, router_b, w1, w3, w2, ln_g, ln_b)


def _rope_tables():
    t = np.arange(DEC_SEQ)
    pos = np.stack([t // GRID_W, t % GRID_W], axis=1).astype(np.float32)
    quarter = HEAD_DIM // 4
    d = np.arange(HEAD_DIM)
    axis = d // (HEAD_DIM // 2)
    freq = d % quarter
    upper = (d % (HEAD_DIM // 2)) >= quarter
    inv = jnp.asarray(ROPE_BASE, F32) ** (-jnp.arange(0, HEAD_DIM // 2, 2, dtype=F32) / (HEAD_DIM // 2))
    ang = jnp.asarray(pos)[:, axis] * inv[freq][None, :]
    cos = jnp.cos(ang)
    sin = jnp.sin(ang)
    sin_up = jnp.where(upper[None, :], 0.0, -sin)
    sin_dn = jnp.where(upper[None, :], sin, 0.0)
    tile = lambda a: jnp.tile(a, (1, N_HEADS))
    return tile(cos), tile(sin_up), tile(sin_dn)


def _bias_tables(na_rpb):
    qc = np.arange(GRID_W)[:, None]
    kc = np.arange(GRID_W)[None, :]
    start = np.clip(qc - WIN_W // 2, 0, GRID_W - WIN_W)
    valid = (kc >= start) & (kc < start + WIN_W)
    n_dc = 2 * WIN_W - 1
    dc = kc - qc + WIN_W - 1
    hit = ((dc[None] == np.arange(n_dc)[:, None, None]) & valid[None]).astype(np.float32)
    sel = np.zeros((2 * n_dc + 1, GRID_W, 2 * GRID_W), np.float32)
    sel[:n_dc, :, :GRID_W] = hit
    sel[n_dc:2 * n_dc, :, GRID_W:] = hit
    sel[2 * n_dc] = np.where(np.concatenate([valid, valid], axis=1), 0.0, NEG_INF)
    ones = jnp.ones(na_rpb.shape[:2] + (2 * WIN_H - 2, 1), F32)
    rows = jnp.concatenate([na_rpb[:, :, :-1], na_rpb[:, :, 1:], ones], axis=-1)
    return jnp.einsum('lhrd,dqk->lhrqk', rows, jnp.asarray(sel), precision=HIGHEST)


def kernel(x_prompt, x_sample, cache_na_k, cache_na_v, c, c_ctx, w_ada, b_ada, w_in, conv_dw,
           conv_b, conv_ln_g, conv_ln_b, conv_pw, na_rpb, na_out, gm_ln_g, gm_ln_b, gm_ws, gm_bs,
           gm_out, w_o, ln1_g, ln1_b, rg_w, rg_b, re_w, re_b, moe_w1, moe_w3, moe_w2, ln2_g, ln2_b):
    cond =jnp.zeros((N_COND, D_MODEL), F32).at[0].set(c_ctx).at[1:1 + DEC_BATCH].set(c)
    mods = _ada(cond, w_ada, b_ada)
    mods = mods.reshape(DEPTH, N_COND, 6, 1, D_MODEL).transpose(0, 2, 1, 3, 4)

    gm_ws_b = gm_ws.astype(BF16)
    vec = lambda a: a.reshape(DEPTH, 1, a.shape[-1])
    gm_bs_t = gm_bs.transpose(0, 2, 1)
    router_w = jnp.concatenate(
        [rg_w, re_w.transpose(0, 2, 1, 3).reshape(DEPTH, D_MODEL, N_EXPERTS)], axis=-1)
    router_w = jnp.pad(router_w, ((0, 0), (0, 0), (0, ROUTER_LANES - N_EGROUPS - N_EXPERTS)))
    router_hi = router_w.astype(BF16)
    router_lo = (router_w - router_hi.astype(F32)).astype(BF16)
    router_w = jnp.stack([router_hi, router_lo], axis=1)
    router_b = jnp.concatenate([rg_b, re_b.reshape(DEPTH, N_EXPERTS)], axis=-1)
    router_b = jnp.pad(router_b, ((0, 0), (0, ROUTER_LANES - N_EGROUPS - N_EXPERTS)))
    router_b = router_b.reshape(DEPTH, 1, ROUTER_LANES)
    cache_k = cache_na_k.reshape(DEC_BATCH, DEPTH, PAST_LEN, D_NA)
    cache_v = cache_na_v.reshape(DEC_BATCH, DEPTH, PAST_LEN, D_NA)
    tz = _bias_tables(na_rpb)
    cos, sin_up, sin_dn = _rope_tables()

    kt_all = vt_all = None
    x, h = _gather_modulate(x_prompt, x_sample, mods)
    for l in range(DEPTH):
        z = _inproj(l, h, w_in, lambda j: jnp.where(j < W_CB_KV, j, j + N_CB_KV), N_CB_MAIN, BF16)
        zkv = _inproj(l, h, w_in, lambda j: j + W_CB_KV, N_CB_KV, F32)
        yc, ug = _branches(l, z, conv_dw, vec(conv_b), vec(conv_ln_g), vec(conv_ln_b),
                           vec(gm_ln_g), vec(gm_ln_b), gm_ws_b, gm_bs_t)
        att_p, kt_all, vt_all = _ctx_attn(l, z, zkv, kt_all, vt_all)
        att_s = _na_attn(l, z, zkv, cache_k, cache_v, tz, cos, sin_up, sin_dn)
        x = _merge(l, x, mods, z, yc, ug, att_p, att_s, conv_pw, na_out, gm_out, w_o,
                   vec(ln1_g), vec(ln1_b))
        outs = _moe(l, x, mods, router_w, router_b, moe_w1, moe_w3, moe_w2, vec(ln2_g), vec(ln2_b))
        x, h = outs

    y_prompt, y_sample = outs
    return (y_prompt.reshape(BATCH, SEQ, D_MODEL), y_sample.reshape(DEC_BATCH, DEC_SEQ, D_MODEL),
            kt_all.transpose(0, 1, 4, 2, 3), vt_all.transpose(0, 1, 4, 2, 3))
```

```python
import functools

import jax
import jax.numpy as jnp
import numpy as np
from jax import lax
from jax.experimental import pallas as pl
from jax.experimental.pallas import tpu as pltpu

F32 = jnp.float32
BF16 = jnp.bfloat16
HIGHEST = lax.Precision.HIGHEST

D_MODEL = 1024
BATCH = 16
SEQ = 256
DEPTH = 4
DEC_BATCH = 2
DEC_SEQ = 1024
PAST_LEN = 256
GRID_W = 64
GRID_H = DEC_SEQ // GRID_W
D_CONV = 512
CONV_WIDTH = 31
CONV_HALF = CONV_WIDTH // 2
HEAD_DIM = 64
HEAD_PAIR = 2 * HEAD_DIM
N_HEADS = 8
D_NA = 512
WIN_H = 8
WIN_W = 16
ROPE_BASE = 10000.0
D_GM = 512
GM_CHUNK = 128
GM_GROUPS = 4
D_IN = 6656
N_EGROUPS = 4
EXP_PER_GROUP = 8
N_EXPERTS = 32
D_EXPERT = 128
ALPHA = (2 * DEPTH) ** 0.25
LN_EPS = 1e-5
NEG_INF = -1e30

T_PROMPT = BATCH * SEQ
T_SAMPLE = DEC_BATCH * DEC_SEQ
T_ALL = T_PROMPT + T_SAMPLE
N_COND = 8

COL_BLK = 512
W_CB_K, W_CB_V = 3, 4
N_CB_KV = 2
N_CB_MAIN = D_IN // COL_BLK - N_CB_KV
CB_A, CB_B, CB_Q, CB_GU, CB_GV, CB_GZ = 0, 1, 2, 3, 4, 5

TB = 256
N_TB = T_ALL // TB
N_TB_PROMPT = T_PROMPT // TB
TB_PER_SAMPLE = DEC_SEQ // TB
TBM = 512
N_TBM = T_ALL // TBM
N_TBM_PROMPT = T_PROMPT // TBM
TBM_PER_SAMPLE = DEC_SEQ // TBM
HALO = 16
CONV_ROWS = 32
SUBLANES = 8
SHIFT_ROWS = TB + 2 * HALO - SUBLANES

NA_ROWS = 4
NA_TQ = NA_ROWS * GRID_W
TM_IN = 2048
TM_MOE = 1024
EXP_STEP = 4
EXP_SUB = 4
ROUTER_LANES = 128
VMEM_LIMIT = 56 * 1024 * 1024


def _ln(x, g, b):
    mu = jnp.mean(x, axis=-1, keepdims=True)
    xc = x - mu
    var = jnp.mean(xc * xc, axis=-1, keepdims=True)
    return xc * lax.rsqrt(var + LN_EPS) * g + b


def _sigmoid(x):
    return jax.nn.sigmoid(x)


def _gelu(x):
    return jax.nn.gelu(x, approximate=True)


_NT = (((1,), (1,)), ((), ()))


def _cparams(sem):
    return pltpu.CompilerParams(dimension_semantics=sem, vmem_limit_bytes=VMEM_LIMIT)


def _ada_kernel(c_ref, w_ref, b_ref, o_ref):
    c = c_ref[...]
    s = c * _sigmoid(c)
    w = w_ref[...]
    s_hi, w_hi = s.astype(BF16), w.astype(BF16)
    s_lo = (s - s_hi.astype(F32)).astype(BF16)
    w_lo = (w - w_hi.astype(F32)).astype(BF16)
    o_ref[...] = (jnp.dot(s_hi, w_hi, preferred_element_type=F32)
                  + jnp.dot(s_lo, w_hi, preferred_element_type=F32)
                  + jnp.dot(s_hi, w_lo, preferred_element_type=F32)) + b_ref[...]


def _ada(cond, w_ada, b_ada):
    tn = 2048
    return pl.pallas_call(
        _ada_kernel,
        name="ada",
        grid=(DEPTH, 6 * D_MODEL // tn),
        in_specs=[
            pl.BlockSpec((N_COND, D_MODEL), lambda l, j: (0, 0)),
            pl.BlockSpec((None, D_MODEL, tn), lambda l, j: (l, 0, j)),
            pl.BlockSpec((None, 1, tn), lambda l, j: (l, 0, j)),
        ],
        out_specs=pl.BlockSpec((None, N_COND, tn), lambda l, j: (l, 0, j)),
        out_shape=jax.ShapeDtypeStruct((DEPTH, N_COND, 6 * D_MODEL), F32),
        compiler_params=_cparams(("arbitrary", "arbitrary")),
    )(cond, w_ada, b_ada.reshape(DEPTH, 1, 6 * D_MODEL))


def _mod_row(i, blocks_per_sample, n_prompt_blocks):
    return jnp.where(i < n_prompt_blocks, 0, 1 + (i - n_prompt_blocks) // blocks_per_sample)


def _gather_modulate_kernel(xp_ref, xs_ref, mod_ref, x_ref, h_ref):
    x = jnp.where(pl.program_id(0) < T_PROMPT // TM_MOE, xp_ref[...], xs_ref[...])
    x_ref[...] = x
    h_ref[...] = (x * (1.0 + mod_ref[1]) + mod_ref[0]).astype(BF16)


def _gather_modulate(x_prompt, x_sample, mods):
    bps = DEC_SEQ // TM_MOE
    npb = T_PROMPT // TM_MOE
    tok_blk = pl.BlockSpec((TM_MOE, D_MODEL), lambda i: (i, 0))
    return pl.pallas_call(
        _gather_modulate_kernel,
        name="modulate",
        grid=(T_ALL // TM_MOE,),
        in_specs=[
            pl.BlockSpec((TM_MOE, D_MODEL), lambda i: (jnp.minimum(i, npb - 1), 0)),
            pl.BlockSpec((TM_MOE, D_MODEL), lambda i: (jnp.maximum(i - npb, 0), 0)),
            pl.BlockSpec((None, 6, None, 1, D_MODEL),
                         lambda i: (0, 0, _mod_row(i, bps, npb), 0, 0)),
        ],
        out_specs=[tok_blk, tok_blk],
        out_shape=[jax.ShapeDtypeStruct((T_ALL, D_MODEL), F32),
                   jax.ShapeDtypeStruct((T_ALL, D_MODEL), BF16)],
        compiler_params=_cparams(("arbitrary",)),
    )(x_prompt.reshape(T_PROMPT, D_MODEL), x_sample.reshape(T_SAMPLE, D_MODEL), mods)


def _inproj_kernel(h_ref, w_ref, z_ref):
    rows = pl.ds(pl.multiple_of(pl.program_id(1) * TM_IN, TM_IN), TM_IN)
    z = jnp.dot(h_ref[rows, :], w_ref[...].astype(BF16), preferred_element_type=F32)
    z_ref[...] = z.astype(BF16)


def _inproj(l, h, w_in):
    return pl.pallas_call(
        _inproj_kernel,
        name="inproj",
        grid=(N_CB_MAIN, T_ALL // TM_IN),
        in_specs=[
            pl.BlockSpec((T_ALL, D_MODEL), lambda j, i: (0, 0)),
            pl.BlockSpec((None, D_MODEL, COL_BLK),
                         lambda j, i: (l, 0, jnp.where(j < W_CB_K, j, j + N_CB_KV))),
        ],
        out_specs=pl.BlockSpec((TM_IN, COL_BLK), lambda j, i: (i, j)),
        out_shape=jax.ShapeDtypeStruct((T_ALL, N_CB_MAIN * COL_BLK), BF16),
        compiler_params=_cparams(("arbitrary", "arbitrary")),
    )(h, w_in)


def _branch_kernel(ap_ref, ac_ref, an_ref, bp_ref, bc_ref, bn_ref, gu_ref, gv_ref,
                   dw_ref, cb_ref, clg_ref, clb_ref, glg_ref, glb_ref, ws_ref, bst_ref,
                   yc_ref, ug_ref, ypad_ref, ysh_ref):
    i = pl.program_id(0)
    j = i - N_TB_PROMPT
    in_sample = i >= N_TB_PROMPT
    has_prev = jnp.logical_and(in_sample, j % TB_PER_SAMPLE != 0)
    has_next = jnp.logical_and(in_sample, j % TB_PER_SAMPLE != TB_PER_SAMPLE - 1)

    def glu(a_ref, b_ref):
        return a_ref[...].astype(F32) * _sigmoid(b_ref[...].astype(F32))

    ypad_ref[0:HALO, :] = jnp.where(has_prev, glu(ap_ref, bp_ref), 0.0)
    ypad_ref[HALO:HALO + TB, :] = glu(ac_ref, bc_ref)
    ypad_ref[HALO + TB:HALO + TB + HALO, :] = jnp.where(has_next, glu(an_ref, bn_ref), 0.0)

    for b in range(SUBLANES):
        ysh_ref[b] = ypad_ref[b:b + SHIFT_ROWS, :]

    off = HALO - CONV_HALF
    for c in range(TB // CONV_ROWS):
        base = c * CONV_ROWS
        acc = jnp.zeros((CONV_ROWS, D_CONV), F32)
        for k in range(CONV_WIDTH):
            tile, phase = divmod(off + k, SUBLANES)
            start = base + tile * SUBLANES
            acc = acc + ysh_ref[phase, start:start + CONV_ROWS, :] * dw_ref[k:k + 1, :]
        y = _ln(acc + cb_ref[...], clg_ref[...], clb_ref[...])
        yc_ref[base:base + CONV_ROWS, :] = (y * _sigmoid(y)).astype(BF16)

    for n in range(TB // GM_CHUNK):
        rows = slice(n * GM_CHUNK, (n + 1) * GM_CHUNK)
        u = _gelu(gu_ref[rows, :].astype(F32))
        v = _ln(_gelu(gv_ref[rows, :].astype(F32)), glg_ref[...], glb_ref[...]).astype(BF16)
        for g in range(GM_GROUPS):
            cols = slice(g * GM_CHUNK, (g + 1) * GM_CHUNK)
            sv = jnp.dot(ws_ref[g], v[:, cols], preferred_element_type=F32) + bst_ref[:, g:g + 1]
            ug_ref[rows, cols] = (u[:, cols] * sv).astype(BF16)


def _branches(l, z, conv_dw, conv_b, conv_ln_g, conv_ln_b, gm_ln_g, gm_ln_b, gm_ws, gm_bs_t):
    halo_per_tb = TB // HALO
    n_halo = T_ALL // HALO

    def cur(cb):
        return pl.BlockSpec((TB, COL_BLK), lambda i: (i, cb))

    def prev(cb):
        return pl.BlockSpec((HALO, COL_BLK), lambda i: (jnp.maximum(i * halo_per_tb - 1, 0), cb))

    def nxt(cb):
        return pl.BlockSpec((HALO, COL_BLK),
                            lambda i: (jnp.minimum((i + 1) * halo_per_tb, n_halo - 1), cb))

    def vec(n):
        return pl.BlockSpec((None, 1, n), lambda i: (l, 0, 0))

    return pl.pallas_call(
        _branch_kernel,
        name="branches",
        grid=(N_TB,),
        in_specs=[
            prev(CB_A), cur(CB_A), nxt(CB_A), prev(CB_B), cur(CB_B), nxt(CB_B),
            cur(CB_GU), cur(CB_GV),
            pl.BlockSpec((None, CONV_WIDTH, D_CONV), lambda i: (l, 0, 0)),
            vec(D_CONV), vec(D_CONV), vec(D_CONV), vec(D_GM), vec(D_GM),
            pl.BlockSpec((None, GM_GROUPS, GM_CHUNK, GM_CHUNK), lambda i: (l, 0, 0, 0)),
            pl.BlockSpec((None, GM_CHUNK, GM_GROUPS), lambda i: (l, 0, 0)),
        ],
        out_specs=[pl.BlockSpec((TB, D_CONV), lambda i: (i, 0)),
                   pl.BlockSpec((TB, D_GM), lambda i: (i, 0))],
        out_shape=[jax.ShapeDtypeStruct((T_ALL, D_CONV), BF16),
                   jax.ShapeDtypeStruct((T_ALL, D_GM), BF16)],
        scratch_shapes=[pltpu.VMEM((TB + 2 * HALO, D_CONV), F32),
                        pltpu.VMEM((SUBLANES, SHIFT_ROWS, D_CONV), F32)],
        compiler_params=_cparams(("arbitrary",)),
    )(z, z, z, z, z, z, z, z, conv_dw, conv_b, conv_ln_g, conv_ln_b, gm_ln_g, gm_ln_b,
      gm_ws, gm_bs_t)


def _ctx_attn_kernel(first, q_ref, h_ref, wk_ref, wv_ref, *rest):
    o_ref, ko_ref, vo_ref, wkt_ref, wvt_ref = rest[-5:]

    @pl.when(pl.program_id(0) == 0)
    def _():
        wkt_ref[...] = wk_ref[...].T.astype(BF16)
        wvt_ref[...] = wv_ref[...].T.astype(BF16)

    h = h_ref[...]
    kt = lax.dot_general(wkt_ref[...], h, _NT, preferred_element_type=F32)
    vt = lax.dot_general(wvt_ref[...], h, _NT, preferred_element_type=F32)
    if first:
        ko_ref[0] = kt.reshape(N_HEADS, HEAD_DIM, SEQ)
        vo_ref[0] = vt.reshape(N_HEADS, HEAD_DIM, SEQ)
        ko_ref[1:] = jnp.zeros((DEPTH - 1, N_HEADS, HEAD_DIM, SEQ), F32)
        vo_ref[1:] = jnp.zeros((DEPTH - 1, N_HEADS, HEAD_DIM, SEQ), F32)
    else:
        ko_ref[...] = kt.reshape(N_HEADS, HEAD_DIM, SEQ)
        vo_ref[...] = vt.reshape(N_HEADS, HEAD_DIM, SEQ)
    q = (q_ref[...].astype(F32) * HEAD_DIM ** -0.5).astype(BF16)
    kb = kt.astype(BF16)
    vb = vt.astype(BF16)
    lower = lax.broadcasted_iota(jnp.int32, (SEQ, HEAD_PAIR), 1) < HEAD_DIM
    upper = jnp.logical_not(lower)
    heads = range(N_HEADS)
    grp = [slice(h // 2 * HEAD_PAIR, (h // 2 + 1) * HEAD_PAIR) for h in heads]
    qh = [jnp.where(lower if h % 2 == 0 else upper, q[:, grp[h]],
                    jnp.zeros((SEQ, HEAD_PAIR), BF16)) for h in heads]
    s = [jnp.dot(qh[h], kb[grp[h], :], preferred_element_type=F32) for h in heads]
    m = [jnp.max(s[h], axis=-1, keepdims=True) for h in heads]
    p = [jnp.exp(s[h] - m[h]) for h in heads]
    den = [jnp.sum(p[h], axis=-1, keepdims=True) for h in heads]
    o = [lax.dot_general(p[h].astype(BF16), vb[grp[h], :], _NT, preferred_element_type=F32) / den[h]
         for h in heads]
    for h in range(0, N_HEADS, 2):
        o_ref[:, grp[h]] = jnp.where(lower, o[h], o[h + 1]).astype(BF16)


def _ctx_attn(l, z, h, w_in, kt_all=None, vt_all=None):
    first = kt_all is None
    if first:
        cache_blk = pl.BlockSpec((None, DEPTH, N_HEADS, HEAD_DIM, SEQ), lambda b: (b, 0, 0, 0, 0))
        carried, carried_specs, aliases = (), [], {}
    else:
        cache_blk = pl.BlockSpec((None, None, N_HEADS, HEAD_DIM, SEQ), lambda b: (b, l, 0, 0, 0))
        carried = (kt_all, vt_all)
        carried_specs = [pl.BlockSpec(memory_space=pl.ANY), pl.BlockSpec(memory_space=pl.ANY)]
        aliases = {4: 1, 5: 2}
    cache_shape = jax.ShapeDtypeStruct((BATCH, DEPTH, N_HEADS, HEAD_DIM, SEQ), F32)
    return pl.pallas_call(
        functools.partial(_ctx_attn_kernel, first),
        name="ctx_attn",
        grid=(BATCH,),
        in_specs=[pl.BlockSpec((SEQ, COL_BLK), lambda b: (b, CB_Q)),
                  pl.BlockSpec((SEQ, D_MODEL), lambda b: (b, 0)),
                  pl.BlockSpec((None, D_MODEL, COL_BLK), lambda b: (l, 0, W_CB_K)),
                  pl.BlockSpec((None, D_MODEL, COL_BLK), lambda b: (l, 0, W_CB_V))] + carried_specs,
        out_specs=[pl.BlockSpec((SEQ, D_NA), lambda b: (b, 0)), cache_blk, cache_blk],
        out_shape=[jax.ShapeDtypeStruct((T_PROMPT, D_NA), BF16), cache_shape, cache_shape],
        scratch_shapes=[pltpu.VMEM((COL_BLK, D_MODEL), BF16), pltpu.VMEM((COL_BLK, D_MODEL), BF16)],
        input_output_aliases=aliases,
        compiler_params=_cparams(("arbitrary",)),
    )(z, h, w_in, w_in, *carried)


def _rope(x, cos, sin_up, sin_dn):
    return (x * cos + pltpu.roll(x, D_NA - HEAD_DIM // 4, 1) * sin_up
            + pltpu.roll(x, HEAD_DIM // 4, 1) * sin_dn)


def _na_attn_kernel(q_ref, h_ref, wk_ref, wv_ref, ck_ref, cv_ref, tz_ref, cos_ref, sup_ref,
                    sdn_ref, o_ref, krot_ref, vb_ref, ckb_ref, cvb_ref):
    step = pl.program_id(1)

    @pl.when(step == 0)
    def _():
        h = h_ref[...]
        k = jnp.dot(h, wk_ref[...].astype(BF16), preferred_element_type=F32)
        v = jnp.dot(h, wv_ref[...].astype(BF16), preferred_element_type=F32)
        krot_ref[...] = _rope(k, cos_ref[...], sup_ref[...], sdn_ref[...]).astype(BF16)
        vb_ref[...] = v.astype(BF16)
        ckb_ref[...] = ck_ref[...].astype(BF16)
        cvb_ref[...] = cv_ref[...].astype(BF16)

    qrows = pl.ds(pl.multiple_of(step * NA_TQ, NA_TQ), NA_TQ)
    q = _rope(q_ref[...].astype(F32), cos_ref[qrows, :], sup_ref[qrows, :], sdn_ref[qrows, :])
    q = (q * HEAD_DIM ** -0.5).astype(BF16)

    lower = lax.broadcasted_iota(jnp.int32, (GRID_W, HEAD_PAIR), 1) < HEAD_DIM
    upper = jnp.logical_not(lower)
    kwin, vwin, dr0 = [], [], []
    for j in range(NA_ROWS):
        r = step * NA_ROWS + j
        row_start = jnp.clip(r - WIN_H // 2, 0, GRID_H - WIN_H)
        krows = pl.ds(pl.multiple_of(row_start * GRID_W, GRID_W), WIN_H * GRID_W)
        kwin.append(krot_ref[krows, :])
        vwin.append(vb_ref[krows, :])
        dr0.append(row_start - r + WIN_H - 1)

    units = [(j, h) for j in range(NA_ROWS) for h in range(N_HEADS)]
    grp = [slice(h // 2 * HEAD_PAIR, (h // 2 + 1) * HEAD_PAIR) for _, h in units]
    qh = [jnp.where(lower if h % 2 == 0 else upper, q[j * GRID_W:(j + 1) * GRID_W, grp[u]],
                    jnp.zeros((GRID_W, HEAD_PAIR), BF16)) for u, (j, h) in enumerate(units)]
    s_loc = [lax.dot_general(qh[u], kwin[j][:, grp[u]], _NT, preferred_element_type=F32)
             + jnp.concatenate([tz_ref[h, dr0[j] + w] for w in range(0, WIN_H, 2)], axis=1)
             for u, (j, h) in enumerate(units)]
    s_ctx = [lax.dot_general(qh[u], ckb_ref[:, grp[u]], _NT, preferred_element_type=F32)
             for u in range(len(units))]
    m = [jnp.maximum(jnp.max(s_loc[u], axis=-1, keepdims=True),
                     jnp.max(s_ctx[u], axis=-1, keepdims=True)) for u in range(len(units))]
    p_loc = [jnp.exp(s_loc[u] - m[u]) for u in range(len(units))]
    p_ctx = [jnp.exp(s_ctx[u] - m[u]) for u in range(len(units))]
    den = [jnp.sum(p_loc[u], axis=-1, keepdims=True) + jnp.sum(p_ctx[u], axis=-1, keepdims=True)
           for u in range(len(units))]
    o = [(jnp.dot(p_loc[u].astype(BF16), vwin[j][:, grp[u]], preferred_element_type=F32)
          + jnp.dot(p_ctx[u].astype(BF16), cvb_ref[:, grp[u]], preferred_element_type=F32)) / den[u]
         for u, (j, h) in enumerate(units)]
    for u, (j, h) in enumerate(units):
        if h % 2 == 0:
            o_ref[j * GRID_W:(j + 1) * GRID_W, grp[u]] = jnp.where(lower, o[u], o[u + 1]).astype(BF16)


def _na_attn(l, z, h, w_in, cache_k, cache_v, tz, cos, sin_up, sin_dn):
    seq_blk0 = T_PROMPT // DEC_SEQ
    row_blk0 = T_PROMPT // NA_TQ
    steps = GRID_H // NA_ROWS
    full = pl.BlockSpec((DEC_SEQ, D_NA), lambda b, r: (0, 0))
    return pl.pallas_call(
        _na_attn_kernel,
        name="na_attn",
        grid=(DEC_BATCH, steps),
        in_specs=[
            pl.BlockSpec((NA_TQ, COL_BLK), lambda b, r: (row_blk0 + b * steps + r, CB_Q)),
            pl.BlockSpec((DEC_SEQ, D_MODEL), lambda b, r: (seq_blk0 + b, 0)),
            pl.BlockSpec((None, D_MODEL, COL_BLK), lambda b, r: (l, 0, W_CB_K)),
            pl.BlockSpec((None, D_MODEL, COL_BLK), lambda b, r: (l, 0, W_CB_V)),
            pl.BlockSpec((None, None, PAST_LEN, D_NA), lambda b, r: (b, l, 0, 0)),
            pl.BlockSpec((None, None, PAST_LEN, D_NA), lambda b, r: (b, l, 0, 0)),
            pl.BlockSpec((None, N_HEADS, 2 * WIN_H - 2, GRID_W, 2 * GRID_W),
                         lambda b, r: (l, 0, 0, 0, 0)),
            full, full, full,
        ],
        out_specs=pl.BlockSpec((NA_TQ, D_NA), lambda b, r: (b * steps + r, 0)),
        out_shape=jax.ShapeDtypeStruct((T_SAMPLE, D_NA), BF16),
        scratch_shapes=[pltpu.VMEM((DEC_SEQ, D_NA), BF16), pltpu.VMEM((DEC_SEQ, D_NA), BF16),
                        pltpu.VMEM((PAST_LEN, D_NA), BF16), pltpu.VMEM((PAST_LEN, D_NA), BF16)],
        compiler_params=_cparams(("arbitrary", "arbitrary")),
    )(z, h, w_in, w_in, cache_k, cache_v, tz, cos, sin_up, sin_dn)


def _merge_kernel(x_ref, mod_ref, g0a, g0b, g1a, g1b, g2a, g2b, yc_ref, ug_ref, ap_ref, as_ref,
                  pw_ref, no_ref, go_ref, wo_ref, lg_ref, lb_ref, o_ref,
                  pwb_ref, nob_ref, gob_ref, wob_ref):
    i = pl.program_id(0)

    @pl.when(i == 0)
    def _():
        pwb_ref[...] = pw_ref[...].astype(BF16)
        nob_ref[...] = no_ref[...].astype(BF16)
        gob_ref[...] = go_ref[...].astype(BF16)
        wob_ref[...] = wo_ref[...].astype(BF16)

    att = jnp.where(i < N_TBM_PROMPT, ap_ref[...], as_ref[...])
    br_c = jnp.dot(yc_ref[...], pwb_ref[...], preferred_element_type=F32)
    br_a = jnp.dot(att, nob_ref[...], preferred_element_type=F32)
    br_g = jnp.dot(ug_ref[...], gob_ref[...], preferred_element_type=F32)
    h = COL_BLK
    for lo, ga, gb, gc in ((0, g0a, g1a, g2a), (h, g0b, g1b, g2b)):
        cols = slice(lo, lo + h)
        o_ref[:, cols] = (_sigmoid(ga[...].astype(F32)) * br_c[:, cols]
                          + _sigmoid(gb[...].astype(F32)) * br_a[:, cols]
                          + _sigmoid(gc[...].astype(F32)) * br_g[:, cols])
    mix = jnp.dot(o_ref[...].astype(BF16), wob_ref[...], preferred_element_type=F32)
    o_ref[...] = _ln(ALPHA * x_ref[...] + mod_ref[2] * mix, lg_ref[...], lb_ref[...])


def _merge(l, x, mods, z, yc, ug, att_p, att_s, conv_pw, na_out, gm_out, w_o, ln_g, ln_b):
    def gz(k):
        return pl.BlockSpec((TBM, COL_BLK), lambda i: (i, CB_GZ + k))

    def w(k, n):
        return pl.BlockSpec((None, k, n), lambda i: (l, 0, 0))

    blk512 = pl.BlockSpec((TBM, COL_BLK), lambda i: (i, 0))
    return pl.pallas_call(
        _merge_kernel,
        name="merge",
        grid=(N_TBM,),
        in_specs=[
            pl.BlockSpec((TBM, D_MODEL), lambda i: (i, 0)),
            pl.BlockSpec((None, 6, None, 1, D_MODEL),
                         lambda i: (l, 0, _mod_row(i, TBM_PER_SAMPLE, N_TBM_PROMPT), 0, 0)),
            gz(0), gz(1), gz(2), gz(3), gz(4), gz(5),
            blk512, blk512,
            pl.BlockSpec((TBM, D_NA), lambda i: (jnp.minimum(i, N_TBM_PROMPT - 1), 0)),
            pl.BlockSpec((TBM, D_NA), lambda i: (jnp.maximum(i - N_TBM_PROMPT, 0), 0)),
            w(D_CONV, D_MODEL), w(D_NA, D_MODEL), w(D_GM, D_MODEL), w(D_MODEL, D_MODEL),
            w(1, D_MODEL), w(1, D_MODEL),
        ],
        out_specs=pl.BlockSpec((TBM, D_MODEL), lambda i: (i, 0)),
        out_shape=jax.ShapeDtypeStruct((T_ALL, D_MODEL), F32),
        scratch_shapes=[pltpu.VMEM((D_CONV, D_MODEL), BF16), pltpu.VMEM((D_NA, D_MODEL), BF16),
                        pltpu.VMEM((D_GM, D_MODEL), BF16), pltpu.VMEM((D_MODEL, D_MODEL), BF16)],
        compiler_params=_cparams(("arbitrary",)),
    )(x, mods, z, z, z, z, z, z, yc, ug, att_p, att_s, conv_pw, na_out, gm_out, w_o, ln_g, ln_b)


def _route(logits):
    lane = lax.broadcasted_iota(jnp.int32, logits.shape, 1)
    big = jnp.int32(ROUTER_LANES)
    is_g = lane < N_EGROUPS
    gl = jnp.where(is_g, logits, -jnp.inf)
    gmax = jnp.max(gl, axis=-1, keepdims=True)
    gidx = jnp.min(jnp.where(gl == gmax, lane, big), axis=-1, keepdims=True)
    gp = 1.0 / jnp.sum(jnp.where(is_g, jnp.exp(gl - gmax), 0.0), axis=-1, keepdims=True)
    lo = N_EGROUPS + gidx * EXP_PER_GROUP
    el = jnp.where(jnp.logical_and(lane >= lo, lane < lo + EXP_PER_GROUP), logits, -jnp.inf)
    v1 = jnp.max(el, axis=-1, keepdims=True)
    i1 = jnp.min(jnp.where(el == v1, lane, big), axis=-1, keepdims=True)
    el2 = jnp.where(lane == i1, -jnp.inf, el)
    v2 = jnp.max(el2, axis=-1, keepdims=True)
    i2 = jnp.min(jnp.where(el2 == v2, lane, big), axis=-1, keepdims=True)
    e2 = jnp.exp(v2 - v1)
    w1 = gp / (1.0 + e2)
    w2 = gp * e2 / (1.0 + e2)
    return jnp.where(lane == i1, w1, 0.0) + jnp.where(lane == i2, w2, 0.0)


def _split_bf16(a):
    hi = a.astype(BF16)
    return hi, (a - hi.astype(F32)).astype(BF16)


def _moe_kernel(last, x_ref, mod_ref, modn_ref, rw_ref, rb_ref, w1_ref, w3_ref, w2_ref, lg_ref,
                lb_ref, out0_ref, out1_ref, t_ref, gate_ref, acc_ref):
    i = pl.program_id(0)
    e = pl.program_id(1)

    @pl.when(e == 0)
    def _():
        t = x_ref[...] * (1.0 + mod_ref[4]) + mod_ref[3]
        t_hi, t_lo = _split_bf16(t)
        t_ref[...] = t_hi
        logits = (jnp.dot(t_hi, rw_ref[0], preferred_element_type=F32)
                  + jnp.dot(t_lo, rw_ref[0], preferred_element_type=F32)
                  + jnp.dot(t_hi, rw_ref[1], preferred_element_type=F32))
        gate_ref[...] = _route(logits + rb_ref[...])
        acc_ref[...] = jnp.zeros_like(acc_ref)

    t = t_ref[...]
    gate = gate_ref[...]
    lane = lax.broadcasted_iota(jnp.int32, gate.shape, 1)
    hcol = lax.broadcasted_iota(jnp.int32, (t.shape[0], EXP_SUB * D_EXPERT), 1) // D_EXPERT
    for k0 in range(0, EXP_STEP, EXP_SUB):
        ks = range(k0, k0 + EXP_SUB)
        w1 = jnp.concatenate([w1_ref[k] for k in ks], axis=1).astype(BF16)
        w3 = jnp.concatenate([w3_ref[k] for k in ks], axis=1).astype(BF16)
        w2 = w2_ref[k0:k0 + EXP_SUB].reshape(EXP_SUB * D_EXPERT, D_MODEL).astype(BF16)
        h1 = jnp.dot(t, w1, preferred_element_type=F32)
        h3 = jnp.dot(t, w3, preferred_element_type=F32)
        gmul = jnp.zeros(h1.shape, F32)
        for k in ks:
            gcol = jnp.sum(jnp.where(lane == e * EXP_STEP + k + N_EGROUPS, gate, 0.0),
                           axis=-1, keepdims=True)
            gmul = jnp.where(hcol == k - k0, gcol, gmul)
        hid = (h1 * _sigmoid(h1) * h3 * gmul).astype(BF16)
        acc_ref[...] += jnp.dot(hid, w2, preferred_element_type=F32)

    @pl.when(e == N_EXPERTS // EXP_STEP - 1)
    def _():
        y = _ln(ALPHA * x_ref[...] + mod_ref[5] * acc_ref[...], lg_ref[...], lb_ref[...])
        if last:
            @pl.when(i < T_PROMPT // TM_MOE)
            def _():
                out0_ref[...] = y

            @pl.when(i >= T_PROMPT // TM_MOE)
            def _():
                out1_ref[...] = y
        else:
            out0_ref[...] = y
            out1_ref[...] = (y * (1.0 + modn_ref[1]) + modn_ref[0]).astype(BF16)


def _moe(l, x, mods, router_w, router_b, w1, w3, w2, ln_g, ln_b):
    n_m = T_ALL // TM_MOE
    bps = DEC_SEQ // TM_MOE
    npb = T_PROMPT // TM_MOE
    last = l + 1 == DEPTH
    l_next = l if last else l + 1
    tok_blk = pl.BlockSpec((TM_MOE, D_MODEL), lambda i, e: (i, 0))
    if last:
        out_specs = [pl.BlockSpec((TM_MOE, D_MODEL), lambda i, e: (jnp.minimum(i, npb - 1), 0)),
                     pl.BlockSpec((TM_MOE, D_MODEL), lambda i, e: (jnp.maximum(i - npb, 0), 0))]
        out_shape = [jax.ShapeDtypeStruct((T_PROMPT, D_MODEL), F32),
                     jax.ShapeDtypeStruct((T_SAMPLE, D_MODEL), F32)]
    else:
        out_specs = [tok_blk, tok_blk]
        out_shape = [jax.ShapeDtypeStruct((T_ALL, D_MODEL), F32),
                     jax.ShapeDtypeStruct((T_ALL, D_MODEL), BF16)]
    return pl.pallas_call(
        functools.partial(_moe_kernel, last),
        name="moe",
        grid=(n_m, N_EXPERTS // EXP_STEP),
        in_specs=[
            tok_blk,
            pl.BlockSpec((None, 6, None, 1, D_MODEL),
                         lambda i, e: (l, 0, _mod_row(i, bps, npb), 0, 0)),
            pl.BlockSpec((None, 6, None, 1, D_MODEL),
                         lambda i, e: (l_next, 0, _mod_row(i, bps, npb), 0, 0)),
            pl.BlockSpec((None, 2, D_MODEL, ROUTER_LANES), lambda i, e: (l, 0, 0, 0)),
            pl.BlockSpec((None, 1, ROUTER_LANES), lambda i, e: (l, 0, 0)),
            pl.BlockSpec((None, EXP_STEP, D_MODEL, D_EXPERT), lambda i, e: (l, e, 0, 0)),
            pl.BlockSpec((None, EXP_STEP, D_MODEL, D_EXPERT), lambda i, e: (l, e, 0, 0)),
            pl.BlockSpec((None, EXP_STEP, D_EXPERT, D_MODEL), lambda i, e: (l, e, 0, 0)),
            pl.BlockSpec((None, 1, D_MODEL), lambda i, e: (l, 0, 0)),
            pl.BlockSpec((None, 1, D_MODEL), lambda i, e: (l, 0, 0)),
        ],
        out_specs=out_specs,
        out_shape=out_shape,
        scratch_shapes=[pltpu.VMEM((TM_MOE, D_MODEL), BF16),
                        pltpu.VMEM((TM_MOE, ROUTER_LANES), F32),
                        pltpu.VMEM((TM_MOE, D_MODEL), F32)],
        compiler_params=_cparams(("arbitrary", "arbitrary")),
    )(x, mods, mods, router_w, router_b, w1, w3, w2, ln_g, ln_b)


def _rope_tables():
    t = np.arange(DEC_SEQ)
    pos = np.stack([t // GRID_W, t % GRID_W], axis=1).astype(np.float32)
    quarter = HEAD_DIM // 4
    d = np.arange(HEAD_DIM)
    axis = d // (HEAD_DIM // 2)
    freq = d % quarter
    upper = (d % (HEAD_DIM // 2)) >= quarter
    inv = jnp.asarray(ROPE_BASE, F32) ** (-jnp.arange(0, HEAD_DIM // 2, 2, dtype=F32) / (HEAD_DIM // 2))
    ang = jnp.asarray(pos)[:, axis] * inv[freq][None, :]
    cos = jnp.cos(ang)
    sin = jnp.sin(ang)
    sin_up = jnp.where(upper[None, :], 0.0, -sin)
    sin_dn = jnp.where(upper[None, :], sin, 0.0)
    tile = lambda a: jnp.tile(a, (1, N_HEADS))
    return tile(cos), tile(sin_up), tile(sin_dn)


def _bias_tables(na_rpb):
    qc = np.arange(GRID_W)[:, None]
    kc = np.arange(GRID_W)[None, :]
    start = np.clip(qc - WIN_W // 2, 0, GRID_W - WIN_W)
    valid = (kc >= start) & (kc < start + WIN_W)
    n_dc = 2 * WIN_W - 1
    dc = kc - qc + WIN_W - 1
    hit = ((dc[None] == np.arange(n_dc)[:, None, None]) & valid[None]).astype(np.float32)
    sel = np.zeros((2 * n_dc + 1, GRID_W, 2 * GRID_W), np.float32)
    sel[:n_dc, :, :GRID_W] = hit
    sel[n_dc:2 * n_dc, :, GRID_W:] = hit
    sel[2 * n_dc] = np.where(np.concatenate([valid, valid], axis=1), 0.0, NEG_INF)
    ones = jnp.ones(na_rpb.shape[:2] + (2 * WIN_H - 2, 1), F32)
    rows = jnp.concatenate([na_rpb[:, :, :-1], na_rpb[:, :, 1:], ones], axis=-1)
    return jnp.einsum('lhrd,dqk->lhrqk', rows, jnp.asarray(sel), precision=HIGHEST)


def kernel(x_prompt, x_sample, cache_na_k, cache_na_v, c, c_ctx, w_ada, b_ada, w_in, conv_dw,
           conv_b, conv_ln_g, conv_ln_b, conv_pw, na_rpb, na_out, gm_ln_g, gm_ln_b, gm_ws, gm_bs,
           gm_out, w_o, ln1_g, ln1_b, rg_w, rg_b, re_w, re_b, moe_w1, moe_w3, moe_w2, ln2_g, ln2_b):
    cond =jnp.zeros((N_COND, D_MODEL), F32).at[0].set(c_ctx).at[1:1 + DEC_BATCH].set(c)
    mods = _ada(cond, w_ada, b_ada)
    mods = mods.reshape(DEPTH, N_COND, 6, 1, D_MODEL).transpose(0, 2, 1, 3, 4)

    gm_ws_b = gm_ws.astype(BF16)
    vec = lambda a: a.reshape(DEPTH, 1, a.shape[-1])
    gm_bs_t = gm_bs.transpose(0, 2, 1)
    router_w = jnp.concatenate(
        [rg_w, re_w.transpose(0, 2, 1, 3).reshape(DEPTH, D_MODEL, N_EXPERTS)], axis=-1)
    router_w = jnp.pad(router_w, ((0, 0), (0, 0), (0, ROUTER_LANES - N_EGROUPS - N_EXPERTS)))
    router_hi = router_w.astype(BF16)
    router_lo = (router_w - router_hi.astype(F32)).astype(BF16)
    router_w = jnp.stack([router_hi, router_lo], axis=1)
    router_b = jnp.concatenate([rg_b, re_b.reshape(DEPTH, N_EXPERTS)], axis=-1)
    router_b = jnp.pad(router_b, ((0, 0), (0, ROUTER_LANES - N_EGROUPS - N_EXPERTS)))
    router_b = router_b.reshape(DEPTH, 1, ROUTER_LANES)
    cache_k = cache_na_k.reshape(DEC_BATCH, DEPTH, PAST_LEN, D_NA)
    cache_v = cache_na_v.reshape(DEC_BATCH, DEPTH, PAST_LEN, D_NA)
    tz = _bias_tables(na_rpb)
    cos, sin_up, sin_dn = _rope_tables()

    kt_all = vt_all = None
    x, h = _gather_modulate(x_prompt, x_sample, mods)
    for l in range(DEPTH):
        z = _inproj(l, h, w_in)
        yc, ug = _branches(l, z, conv_dw, vec(conv_b), vec(conv_ln_g), vec(conv_ln_b),
                           vec(gm_ln_g), vec(gm_ln_b), gm_ws_b, gm_bs_t)
        att_p, kt_all, vt_all = _ctx_attn(l, z, h, w_in, kt_all, vt_all)
        att_s = _na_attn(l, z, h, w_in, cache_k, cache_v, tz, cos, sin_up, sin_dn)
        x = _merge(l, x, mods, z, yc, ug, att_p, att_s, conv_pw, na_out, gm_out, w_o,
                   vec(ln1_g), vec(ln1_b))
        outs = _moe(l, x, mods, router_w, router_b, moe_w1, moe_w3, moe_w2, vec(ln2_g), vec(ln2_b))
        x, h = outs

    y_prompt, y_sample = outs
    return (y_prompt.reshape(BATCH, SEQ, D_MODEL), y_sample.reshape(DEC_BATCH, DEC_SEQ, D_MODEL),
            kt_all.transpose(0, 1, 4, 2, 3), vt_all.transpose(0, 1, 4, 2, 3))
```

```python
import functools

import jax
import jax.numpy as jnp
import numpy as np
from jax import lax
from jax.experimental import pallas as pl
from jax.experimental.pallas import tpu as pltpu

F32 = jnp.float32
BF16 = jnp.bfloat16
HIGHEST = lax.Precision.HIGHEST

D_MODEL = 1024
BATCH = 16
SEQ = 256
DEPTH = 4
DEC_BATCH = 2
DEC_SEQ = 1024
PAST_LEN = 256
GRID_W = 64
GRID_H = DEC_SEQ // GRID_W
D_CONV = 512
CONV_WIDTH = 31
CONV_HALF = CONV_WIDTH // 2
HEAD_DIM = 64
HEAD_PAIR = 2 * HEAD_DIM
N_HEADS = 8
D_NA = 512
WIN_H = 8
WIN_W = 16
ROPE_BASE = 10000.0
D_GM = 512
GM_CHUNK = 128
GM_GROUPS = 4
D_IN = 6656
N_EGROUPS = 4
EXP_PER_GROUP = 8
N_EXPERTS = 32
D_EXPERT = 128
ALPHA = (2 * DEPTH) ** 0.25
LN_EPS = 1e-5
NEG_INF = -1e30

T_PROMPT = BATCH * SEQ
T_SAMPLE = DEC_BATCH * DEC_SEQ
T_ALL = T_PROMPT + T_SAMPLE
N_COND = 8

COL_BLK = 512
W_CB_K, W_CB_V = 3, 4
N_CB_KV = 2
N_CB_MAIN = D_IN // COL_BLK - N_CB_KV
CB_A, CB_B, CB_Q, CB_GU, CB_GV, CB_GZ = 0, 1, 2, 3, 4, 5

TB = 256
N_TB = T_ALL // TB
N_TB_PROMPT = T_PROMPT // TB
TB_PER_SAMPLE = DEC_SEQ // TB
TBM = 512
N_TBM = T_ALL // TBM
N_TBM_PROMPT = T_PROMPT // TBM
TBM_PER_SAMPLE = DEC_SEQ // TBM
HALO = 16
CONV_ROWS = 32
SUBLANES = 8
SHIFT_ROWS = TB + 2 * HALO - SUBLANES

NA_ROWS = 4
NA_TQ = NA_ROWS * GRID_W
TM_IN = 3072
TM_MOE = 1024
EXP_STEP = 4
EXP_SUB = 4
ROUTER_LANES = 128
VMEM_LIMIT = 56 * 1024 * 1024


def _ln(x, g, b):
    mu = jnp.mean(x, axis=-1, keepdims=True)
    xc = x - mu
    var = jnp.mean(xc * xc, axis=-1, keepdims=True)
    return xc * lax.rsqrt(var + LN_EPS) * g + b


def _sigmoid(x):
    return jax.nn.sigmoid(x)


def _gelu(x):
    return jax.nn.gelu(x, approximate=True)


_NT = (((1,), (1,)), ((), ()))


def _cparams(sem):
    return pltpu.CompilerParams(dimension_semantics=sem, vmem_limit_bytes=VMEM_LIMIT)


def _ada_kernel(c_ref, w_ref, b_ref, o_ref):
    c = c_ref[...]
    s = c * _sigmoid(c)
    w = w_ref[...]
    s_hi, w_hi = s.astype(BF16), w.astype(BF16)
    s_lo = (s - s_hi.astype(F32)).astype(BF16)
    w_lo = (w - w_hi.astype(F32)).astype(BF16)
    o_ref[...] = (jnp.dot(s_hi, w_hi, preferred_element_type=F32)
                  + jnp.dot(s_lo, w_hi, preferred_element_type=F32)
                  + jnp.dot(s_hi, w_lo, preferred_element_type=F32)) + b_ref[...]


def _ada(cond, w_ada, b_ada):
    tn = 3072
    return pl.pallas_call(
        _ada_kernel,
        name="ada",
        grid=(DEPTH, 6 * D_MODEL // tn),
        in_specs=[
            pl.BlockSpec((N_COND, D_MODEL), lambda l, j: (0, 0)),
            pl.BlockSpec((None, D_MODEL, tn), lambda l, j: (l, 0, j)),
            pl.BlockSpec((None, 1, tn), lambda l, j: (l, 0, j)),
        ],
        out_specs=pl.BlockSpec((None, N_COND, tn), lambda l, j: (l, 0, j)),
        out_shape=jax.ShapeDtypeStruct((DEPTH, N_COND, 6 * D_MODEL), F32),
        compiler_params=_cparams(("arbitrary", "arbitrary")),
    )(cond, w_ada, b_ada.reshape(DEPTH, 1, 6 * D_MODEL))


def _mod_row(i, blocks_per_sample, n_prompt_blocks):
    return jnp.where(i < n_prompt_blocks, 0, 1 + (i - n_prompt_blocks) // blocks_per_sample)


def _gather_modulate_kernel(xp_ref, xs_ref, mod_ref, x_ref, h_ref):
    x = jnp.where(pl.program_id(0) < T_PROMPT // TM_MOE, xp_ref[...], xs_ref[...])
    x_ref[...] = x
    h_ref[...] = (x * (1.0 + mod_ref[1]) + mod_ref[0]).astype(BF16)


def _gather_modulate(x_prompt, x_sample, mods):
    bps = DEC_SEQ // TM_MOE
    npb = T_PROMPT // TM_MOE
    tok_blk = pl.BlockSpec((TM_MOE, D_MODEL), lambda i: (i, 0))
    return pl.pallas_call(
        _gather_modulate_kernel,
        name="modulate",
        grid=(T_ALL // TM_MOE,),
        in_specs=[
            pl.BlockSpec((TM_MOE, D_MODEL), lambda i: (jnp.minimum(i, npb - 1), 0)),
            pl.BlockSpec((TM_MOE, D_MODEL), lambda i: (jnp.maximum(i - npb, 0), 0)),
            pl.BlockSpec((None, 6, None, 1, D_MODEL),
                         lambda i: (0, 0, _mod_row(i, bps, npb), 0, 0)),
        ],
        out_specs=[tok_blk, tok_blk],
        out_shape=[jax.ShapeDtypeStruct((T_ALL, D_MODEL), F32),
                   jax.ShapeDtypeStruct((T_ALL, D_MODEL), BF16)],
        compiler_params=_cparams(("arbitrary",)),
    )(x_prompt.reshape(T_PROMPT, D_MODEL), x_sample.reshape(T_SAMPLE, D_MODEL), mods)


def _inproj_kernel(h_ref, w_ref, z_ref):
    rows = pl.ds(pl.multiple_of(pl.program_id(1) * TM_IN, TM_IN), TM_IN)
    z = jnp.dot(h_ref[rows, :], w_ref[...].astype(BF16), preferred_element_type=F32)
    z_ref[...] = z.astype(BF16)


def _inproj(l, h, w_in):
    return pl.pallas_call(
        _inproj_kernel,
        name="inproj",
        grid=(N_CB_MAIN, T_ALL // TM_IN),
        in_specs=[
            pl.BlockSpec((T_ALL, D_MODEL), lambda j, i: (0, 0)),
            pl.BlockSpec((None, D_MODEL, COL_BLK),
                         lambda j, i: (l, 0, jnp.where(j < W_CB_K, j, j + N_CB_KV))),
        ],
        out_specs=pl.BlockSpec((TM_IN, COL_BLK), lambda j, i: (i, j)),
        out_shape=jax.ShapeDtypeStruct((T_ALL, N_CB_MAIN * COL_BLK), BF16),
        compiler_params=_cparams(("arbitrary", "arbitrary")),
    )(h, w_in)


def _branch_kernel(ap_ref, ac_ref, an_ref, bp_ref, bc_ref, bn_ref, gu_ref, gv_ref,
                   dw_ref, cb_ref, clg_ref, clb_ref, glg_ref, glb_ref, ws_ref, bst_ref,
                   yc_ref, ug_ref, ypad_ref, ysh_ref):
    i = pl.program_id(0)
    j = i - N_TB_PROMPT
    in_sample = i >= N_TB_PROMPT
    has_prev = jnp.logical_and(in_sample, j % TB_PER_SAMPLE != 0)
    has_next = jnp.logical_and(in_sample, j % TB_PER_SAMPLE != TB_PER_SAMPLE - 1)

    def glu(a_ref, b_ref):
        return a_ref[...].astype(F32) * _sigmoid(b_ref[...].astype(F32))

    ypad_ref[0:HALO, :] = jnp.where(has_prev, glu(ap_ref, bp_ref), 0.0)
    ypad_ref[HALO:HALO + TB, :] = glu(ac_ref, bc_ref)
    ypad_ref[HALO + TB:HALO + TB + HALO, :] = jnp.where(has_next, glu(an_ref, bn_ref), 0.0)

    for b in range(SUBLANES):
        ysh_ref[b] = ypad_ref[b:b + SHIFT_ROWS, :]

    off = HALO - CONV_HALF
    for c in range(TB // CONV_ROWS):
        base = c * CONV_ROWS
        acc = jnp.zeros((CONV_ROWS, D_CONV), F32)
        for k in range(CONV_WIDTH):
            tile, phase = divmod(off + k, SUBLANES)
            start = base + tile * SUBLANES
            acc = acc + ysh_ref[phase, start:start + CONV_ROWS, :] * dw_ref[k:k + 1, :]
        y = _ln(acc + cb_ref[...], clg_ref[...], clb_ref[...])
        yc_ref[base:base + CONV_ROWS, :] = (y * _sigmoid(y)).astype(BF16)

    for n in range(TB // GM_CHUNK):
        rows = slice(n * GM_CHUNK, (n + 1) * GM_CHUNK)
        u = _gelu(gu_ref[rows, :].astype(F32))
        v = _ln(_gelu(gv_ref[rows, :].astype(F32)), glg_ref[...], glb_ref[...]).astype(BF16)
        for g in range(GM_GROUPS):
            cols = slice(g * GM_CHUNK, (g + 1) * GM_CHUNK)
            sv = jnp.dot(ws_ref[g], v[:, cols], preferred_element_type=F32) + bst_ref[:, g:g + 1]
            ug_ref[rows, cols] = (u[:, cols] * sv).astype(BF16)


def _branches(l, z, conv_dw, conv_b, conv_ln_g, conv_ln_b, gm_ln_g, gm_ln_b, gm_ws, gm_bs_t):
    halo_per_tb = TB // HALO
    n_halo = T_ALL // HALO

    def cur(cb):
        return pl.BlockSpec((TB, COL_BLK), lambda i: (i, cb))

    def prev(cb):
        return pl.BlockSpec((HALO, COL_BLK), lambda i: (jnp.maximum(i * halo_per_tb - 1, 0), cb))

    def nxt(cb):
        return pl.BlockSpec((HALO, COL_BLK),
                            lambda i: (jnp.minimum((i + 1) * halo_per_tb, n_halo - 1), cb))

    def vec(n):
        return pl.BlockSpec((None, 1, n), lambda i: (l, 0, 0))

    return pl.pallas_call(
        _branch_kernel,
        name="branches",
        grid=(N_TB,),
        in_specs=[
            prev(CB_A), cur(CB_A), nxt(CB_A), prev(CB_B), cur(CB_B), nxt(CB_B),
            cur(CB_GU), cur(CB_GV),
            pl.BlockSpec((None, CONV_WIDTH, D_CONV), lambda i: (l, 0, 0)),
            vec(D_CONV), vec(D_CONV), vec(D_CONV), vec(D_GM), vec(D_GM),
            pl.BlockSpec((None, GM_GROUPS, GM_CHUNK, GM_CHUNK), lambda i: (l, 0, 0, 0)),
            pl.BlockSpec((None, GM_CHUNK, GM_GROUPS), lambda i: (l, 0, 0)),
        ],
        out_specs=[pl.BlockSpec((TB, D_CONV), lambda i: (i, 0)),
                   pl.BlockSpec((TB, D_GM), lambda i: (i, 0))],
        out_shape=[jax.ShapeDtypeStruct((T_ALL, D_CONV), BF16),
                   jax.ShapeDtypeStruct((T_ALL, D_GM), BF16)],
        scratch_shapes=[pltpu.VMEM((TB + 2 * HALO, D_CONV), F32),
                        pltpu.VMEM((SUBLANES, SHIFT_ROWS, D_CONV), F32)],
        compiler_params=_cparams(("arbitrary",)),
    )(z, z, z, z, z, z, z, z, conv_dw, conv_b, conv_ln_g, conv_ln_b, gm_ln_g, gm_ln_b,
      gm_ws, gm_bs_t)


def _ctx_attn_kernel(first, q_ref, h_ref, wk_ref, wv_ref, *rest):
    o_ref, ko_ref, vo_ref, wkt_ref, wvt_ref = rest[-5:]

    @pl.when(pl.program_id(0) == 0)
    def _():
        wkt_ref[...] = wk_ref[...].T.astype(BF16)
        wvt_ref[...] = wv_ref[...].T.astype(BF16)

    h = h_ref[...]
    kt = lax.dot_general(wkt_ref[...], h, _NT, preferred_element_type=F32)
    vt = lax.dot_general(wvt_ref[...], h, _NT, preferred_element_type=F32)
    if first:
        ko_ref[0] = kt.reshape(N_HEADS, HEAD_DIM, SEQ)
        vo_ref[0] = vt.reshape(N_HEADS, HEAD_DIM, SEQ)
        ko_ref[1:] = jnp.zeros((DEPTH - 1, N_HEADS, HEAD_DIM, SEQ), F32)
        vo_ref[1:] = jnp.zeros((DEPTH - 1, N_HEADS, HEAD_DIM, SEQ), F32)
    else:
        ko_ref[...] = kt.reshape(N_HEADS, HEAD_DIM, SEQ)
        vo_ref[...] = vt.reshape(N_HEADS, HEAD_DIM, SEQ)
    q = (q_ref[...].astype(F32) * HEAD_DIM ** -0.5).astype(BF16)
    kb = kt.astype(BF16)
    vb = vt.astype(BF16)
    lower = lax.broadcasted_iota(jnp.int32, (SEQ, HEAD_PAIR), 1) < HEAD_DIM
    upper = jnp.logical_not(lower)
    heads = range(N_HEADS)
    grp = [slice(h // 2 * HEAD_PAIR, (h // 2 + 1) * HEAD_PAIR) for h in heads]
    qh = [jnp.where(lower if h % 2 == 0 else upper, q[:, grp[h]],
                    jnp.zeros((SEQ, HEAD_PAIR), BF16)) for h in heads]
    s = [jnp.dot(qh[h], kb[grp[h], :], preferred_element_type=F32) for h in heads]
    m = [jnp.max(s[h], axis=-1, keepdims=True) for h in heads]
    p = [jnp.exp(s[h] - m[h]) for h in heads]
    den = [jnp.sum(p[h], axis=-1, keepdims=True) for h in heads]
    o = [lax.dot_general(p[h].astype(BF16), vb[grp[h], :], _NT, preferred_element_type=F32) / den[h]
         for h in heads]
    for h in range(0, N_HEADS, 2):
        o_ref[:, grp[h]] = jnp.where(lower, o[h], o[h + 1]).astype(BF16)


def _ctx_attn(l, z, h, w_in, kt_all=None, vt_all=None):
    first = kt_all is None
    if first:
        cache_blk = pl.BlockSpec((None, DEPTH, N_HEADS, HEAD_DIM, SEQ), lambda b: (b, 0, 0, 0, 0))
        carried, carried_specs, aliases = (), [], {}
    else:
        cache_blk = pl.BlockSpec((None, None, N_HEADS, HEAD_DIM, SEQ), lambda b: (b, l, 0, 0, 0))
        carried = (kt_all, vt_all)
        carried_specs = [pl.BlockSpec(memory_space=pl.ANY), pl.BlockSpec(memory_space=pl.ANY)]
        aliases = {4: 1, 5: 2}
    cache_shape = jax.ShapeDtypeStruct((BATCH, DEPTH, N_HEADS, HEAD_DIM, SEQ), F32)
    return pl.pallas_call(
        functools.partial(_ctx_attn_kernel, first),
        name="ctx_attn",
        grid=(BATCH,),
        in_specs=[pl.BlockSpec((SEQ, COL_BLK), lambda b: (b, CB_Q)),
                  pl.BlockSpec((SEQ, D_MODEL), lambda b: (b, 0)),
                  pl.BlockSpec((None, D_MODEL, COL_BLK), lambda b: (l, 0, W_CB_K)),
                  pl.BlockSpec((None, D_MODEL, COL_BLK), lambda b: (l, 0, W_CB_V))] + carried_specs,
        out_specs=[pl.BlockSpec((SEQ, D_NA), lambda b: (b, 0)), cache_blk, cache_blk],
        out_shape=[jax.ShapeDtypeStruct((T_PROMPT, D_NA), BF16), cache_shape, cache_shape],
        scratch_shapes=[pltpu.VMEM((COL_BLK, D_MODEL), BF16), pltpu.VMEM((COL_BLK, D_MODEL), BF16)],
        input_output_aliases=aliases,
        compiler_params=_cparams(("arbitrary",)),
    )(z, h, w_in, w_in, *carried)


def _rope(x, cos, sin_up, sin_dn):
    return (x * cos + pltpu.roll(x, D_NA - HEAD_DIM // 4, 1) * sin_up
            + pltpu.roll(x, HEAD_DIM // 4, 1) * sin_dn)


def _na_attn_kernel(q_ref, h_ref, wk_ref, wv_ref, ck_ref, cv_ref, tz_ref, cos_ref, sup_ref,
                    sdn_ref, o_ref, krot_ref, vb_ref, ckb_ref, cvb_ref):
    step = pl.program_id(1)

    @pl.when(step == 0)
    def _():
        h = h_ref[...]
        k = jnp.dot(h, wk_ref[...].astype(BF16), preferred_element_type=F32)
        v = jnp.dot(h, wv_ref[...].astype(BF16), preferred_element_type=F32)
        krot_ref[...] = _rope(k, cos_ref[...], sup_ref[...], sdn_ref[...]).astype(BF16)
        vb_ref[...] = v.astype(BF16)
        ckb_ref[...] = ck_ref[...].astype(BF16)
        cvb_ref[...] = cv_ref[...].astype(BF16)

    qrows = pl.ds(pl.multiple_of(step * NA_TQ, NA_TQ), NA_TQ)
    q = _rope(q_ref[...].astype(F32), cos_ref[qrows, :], sup_ref[qrows, :], sdn_ref[qrows, :])
    q = (q * HEAD_DIM ** -0.5).astype(BF16)

    lower = lax.broadcasted_iota(jnp.int32, (GRID_W, HEAD_PAIR), 1) < HEAD_DIM
    upper = jnp.logical_not(lower)
    kwin, vwin, dr0 = [], [], []
    for j in range(NA_ROWS):
        r = step * NA_ROWS + j
        row_start = jnp.clip(r - WIN_H // 2, 0, GRID_H - WIN_H)
        krows = pl.ds(pl.multiple_of(row_start * GRID_W, GRID_W), WIN_H * GRID_W)
        kwin.append(krot_ref[krows, :])
        vwin.append(vb_ref[krows, :])
        dr0.append(row_start - r + WIN_H - 1)

    units = [(j, h) for j in range(NA_ROWS) for h in range(N_HEADS)]
    grp = [slice(h // 2 * HEAD_PAIR, (h // 2 + 1) * HEAD_PAIR) for _, h in units]
    qh = [jnp.where(lower if h % 2 == 0 else upper, q[j * GRID_W:(j + 1) * GRID_W, grp[u]],
                    jnp.zeros((GRID_W, HEAD_PAIR), BF16)) for u, (j, h) in enumerate(units)]
    s_loc = [lax.dot_general(qh[u], kwin[j][:, grp[u]], _NT, preferred_element_type=F32)
             + jnp.concatenate([tz_ref[h, dr0[j] + w] for w in range(0, WIN_H, 2)], axis=1)
             for u, (j, h) in enumerate(units)]
    s_ctx = [lax.dot_general(qh[u], ckb_ref[:, grp[u]], _NT, preferred_element_type=F32)
             for u in range(len(units))]
    m = [jnp.maximum(jnp.max(s_loc[u], axis=-1, keepdims=True),
                     jnp.max(s_ctx[u], axis=-1, keepdims=True)) for u in range(len(units))]
    p_loc = [jnp.exp(s_loc[u] - m[u]) for u in range(len(units))]
    p_ctx = [jnp.exp(s_ctx[u] - m[u]) for u in range(len(units))]
    den = [jnp.sum(p_loc[u], axis=-1, keepdims=True) + jnp.sum(p_ctx[u], axis=-1, keepdims=True)
           for u in range(len(units))]
    o = [(jnp.dot(p_loc[u].astype(BF16), vwin[j][:, grp[u]], preferred_element_type=F32)
          + jnp.dot(p_ctx[u].astype(BF16), cvb_ref[:, grp[u]], preferred_element_type=F32)) / den[u]
         for u, (j, h) in enumerate(units)]
    for u, (j, h) in enumerate(units):
        if h % 2 == 0:
            o_ref[j * GRID_W:(j + 1) * GRID_W, grp[u]] = jnp.where(lower, o[u], o[u + 1]).astype(BF16)


def _na_attn(l, z, h, w_in, cache_k, cache_v, tz, cos, sin_up, sin_dn):
    seq_blk0 = T_PROMPT // DEC_SEQ
    row_blk0 = T_PROMPT // NA_TQ
    steps = GRID_H // NA_ROWS
    full = pl.BlockSpec((DEC_SEQ, D_NA), lambda b, r: (0, 0))
    return pl.pallas_call(
        _na_attn_kernel,
        name="na_attn",
        grid=(DEC_BATCH, steps),
        in_specs=[
            pl.BlockSpec((NA_TQ, COL_BLK), lambda b, r: (row_blk0 + b * steps + r, CB_Q)),
            pl.BlockSpec((DEC_SEQ, D_MODEL), lambda b, r: (seq_blk0 + b, 0)),
            pl.BlockSpec((None, D_MODEL, COL_BLK), lambda b, r: (l, 0, W_CB_K)),
            pl.BlockSpec((None, D_MODEL, COL_BLK), lambda b, r: (l, 0, W_CB_V)),
            pl.BlockSpec((None, None, PAST_LEN, D_NA), lambda b, r: (b, l, 0, 0)),
            pl.BlockSpec((None, None, PAST_LEN, D_NA), lambda b, r: (b, l, 0, 0)),
            pl.BlockSpec((None, N_HEADS, 2 * WIN_H - 2, GRID_W, 2 * GRID_W),
                         lambda b, r: (l, 0, 0, 0, 0)),
            full, full, full,
        ],
        out_specs=pl.BlockSpec((NA_TQ, D_NA), lambda b, r: (b * steps + r, 0)),
        out_shape=jax.ShapeDtypeStruct((T_SAMPLE, D_NA), BF16),
        scratch_shapes=[pltpu.VMEM((DEC_SEQ, D_NA), BF16), pltpu.VMEM((DEC_SEQ, D_NA), BF16),
                        pltpu.VMEM((PAST_LEN, D_NA), BF16), pltpu.VMEM((PAST_LEN, D_NA), BF16)],
        compiler_params=_cparams(("arbitrary", "arbitrary")),
    )(z, h, w_in, w_in, cache_k, cache_v, tz, cos, sin_up, sin_dn)


def _merge_kernel(x_ref, mod_ref, g0a, g0b, g1a, g1b, g2a, g2b, yc_ref, ug_ref, ap_ref, as_ref,
                  pw_ref, no_ref, go_ref, wo_ref, lg_ref, lb_ref, o_ref,
                  pwb_ref, nob_ref, gob_ref, wob_ref):
    i = pl.program_id(0)

    @pl.when(i == 0)
    def _():
        pwb_ref[...] = pw_ref[...].astype(BF16)
        nob_ref[...] = no_ref[...].astype(BF16)
        gob_ref[...] = go_ref[...].astype(BF16)
        wob_ref[...] = wo_ref[...].astype(BF16)

    att = jnp.where(i < N_TBM_PROMPT, ap_ref[...], as_ref[...])
    br_c = jnp.dot(yc_ref[...], pwb_ref[...], preferred_element_type=F32)
    br_a = jnp.dot(att, nob_ref[...], preferred_element_type=F32)
    br_g = jnp.dot(ug_ref[...], gob_ref[...], preferred_element_type=F32)
    h = COL_BLK
    for lo, ga, gb, gc in ((0, g0a, g1a, g2a), (h, g0b, g1b, g2b)):
        cols = slice(lo, lo + h)
        o_ref[:, cols] = (_sigmoid(ga[...].astype(F32)) * br_c[:, cols]
                          + _sigmoid(gb[...].astype(F32)) * br_a[:, cols]
                          + _sigmoid(gc[...].astype(F32)) * br_g[:, cols])
    mix = jnp.dot(o_ref[...].astype(BF16), wob_ref[...], preferred_element_type=F32)
    o_ref[...] = _ln(ALPHA * x_ref[...] + mod_ref[2] * mix, lg_ref[...], lb_ref[...])


def _merge(l, x, mods, z, yc, ug, att_p, att_s, conv_pw, na_out, gm_out, w_o, ln_g, ln_b):
    def gz(k):
        return pl.BlockSpec((TBM, COL_BLK), lambda i: (i, CB_GZ + k))

    def w(k, n):
        return pl.BlockSpec((None, k, n), lambda i: (l, 0, 0))

    blk512 = pl.BlockSpec((TBM, COL_BLK), lambda i: (i, 0))
    return pl.pallas_call(
        _merge_kernel,
        name="merge",
        grid=(N_TBM,),
        in_specs=[
            pl.BlockSpec((TBM, D_MODEL), lambda i: (i, 0)),
            pl.BlockSpec((None, 6, None, 1, D_MODEL),
                         lambda i: (l, 0, _mod_row(i, TBM_PER_SAMPLE, N_TBM_PROMPT), 0, 0)),
            gz(0), gz(1), gz(2), gz(3), gz(4), gz(5),
            blk512, blk512,
            pl.BlockSpec((TBM, D_NA), lambda i: (jnp.minimum(i, N_TBM_PROMPT - 1), 0)),
            pl.BlockSpec((TBM, D_NA), lambda i: (jnp.maximum(i - N_TBM_PROMPT, 0), 0)),
            w(D_CONV, D_MODEL), w(D_NA, D_MODEL), w(D_GM, D_MODEL), w(D_MODEL, D_MODEL),
            w(1, D_MODEL), w(1, D_MODEL),
        ],
        out_specs=pl.BlockSpec((TBM, D_MODEL), lambda i: (i, 0)),
        out_shape=jax.ShapeDtypeStruct((T_ALL, D_MODEL), F32),
        scratch_shapes=[pltpu.VMEM((D_CONV, D_MODEL), BF16), pltpu.VMEM((D_NA, D_MODEL), BF16),
                        pltpu.VMEM((D_GM, D_MODEL), BF16), pltpu.VMEM((D_MODEL, D_MODEL), BF16)],
        compiler_params=_cparams(("arbitrary",)),
    )(x, mods, z, z, z, z, z, z, yc, ug, att_p, att_s, conv_pw, na_out, gm_out, w_o, ln_g, ln_b)


def _route(logits):
    lane = lax.broadcasted_iota(jnp.int32, logits.shape, 1)
    big = jnp.int32(ROUTER_LANES)
    is_g = lane < N_EGROUPS
    gl = jnp.where(is_g, logits, -jnp.inf)
    gmax = jnp.max(gl, axis=-1, keepdims=True)
    gidx = jnp.min(jnp.where(gl == gmax, lane, big), axis=-1, keepdims=True)
    gp = 1.0 / jnp.sum(jnp.where(is_g, jnp.exp(gl - gmax), 0.0), axis=-1, keepdims=True)
    lo = N_EGROUPS + gidx * EXP_PER_GROUP
    el = jnp.where(jnp.logical_and(lane >= lo, lane < lo + EXP_PER_GROUP), logits, -jnp.inf)
    v1 = jnp.max(el, axis=-1, keepdims=True)
    i1 = jnp.min(jnp.where(el == v1, lane, big), axis=-1, keepdims=True)
    el2 = jnp.where(lane == i1, -jnp.inf, el)
    v2 = jnp.max(el2, axis=-1, keepdims=True)
    i2 = jnp.min(jnp.where(el2 == v2, lane, big), axis=-1, keepdims=True)
    e2 = jnp.exp(v2 - v1)
    w1 = gp / (1.0 + e2)
    w2 = gp * e2 / (1.0 + e2)
    return jnp.where(lane == i1, w1, 0.0) + jnp.where(lane == i2, w2, 0.0)


def _split_bf16(a):
    hi = a.astype(BF16)
    return hi, (a - hi.astype(F32)).astype(BF16)


def _moe_kernel(last, x_ref, mod_ref, modn_ref, rw_ref, rb_ref, w1_ref, w3_ref, w2_ref, lg_ref,
                lb_ref, out0_ref, out1_ref, t_ref, gate_ref, acc_ref):
    i = pl.program_id(0)
    e = pl.program_id(1)

    @pl.when(e == 0)
    def _():
        t = x_ref[...] * (1.0 + mod_ref[4]) + mod_ref[3]
        t_hi, t_lo = _split_bf16(t)
        t_ref[...] = t_hi
        both = jnp.dot(t_hi, rw_ref[...], preferred_element_type=F32)
        logits = (both[:, :ROUTER_LANES] + both[:, ROUTER_LANES:]
                  + jnp.dot(t_lo, rw_ref[:, :ROUTER_LANES], preferred_element_type=F32))
        gate_ref[...] = _route(logits + rb_ref[...])
        acc_ref[...] = jnp.zeros_like(acc_ref)

    t = t_ref[...]
    gate = gate_ref[...]
    lane = lax.broadcasted_iota(jnp.int32, gate.shape, 1)
    hcol = lax.broadcasted_iota(jnp.int32, (t.shape[0], EXP_SUB * D_EXPERT), 1) // D_EXPERT
    for k0 in range(0, EXP_STEP, EXP_SUB):
        ks = range(k0, k0 + EXP_SUB)
        w1 = jnp.concatenate([w1_ref[k] for k in ks], axis=1).astype(BF16)
        w3 = jnp.concatenate([w3_ref[k] for k in ks], axis=1).astype(BF16)
        w2 = w2_ref[k0:k0 + EXP_SUB].reshape(EXP_SUB * D_EXPERT, D_MODEL).astype(BF16)
        h1 = jnp.dot(t, w1, preferred_element_type=F32)
        h3 = jnp.dot(t, w3, preferred_element_type=F32)
        gmul = jnp.zeros(h1.shape, F32)
        for k in ks:
            gcol = jnp.sum(jnp.where(lane == e * EXP_STEP + k + N_EGROUPS, gate, 0.0),
                           axis=-1, keepdims=True)
            gmul = jnp.where(hcol == k - k0, gcol, gmul)
        hid = (h1 * _sigmoid(h1) * h3 * gmul).astype(BF16)
        acc_ref[...] += jnp.dot(hid, w2, preferred_element_type=F32)

    @pl.when(e == N_EXPERTS // EXP_STEP - 1)
    def _():
        y = _ln(ALPHA * x_ref[...] + mod_ref[5] * acc_ref[...], lg_ref[...], lb_ref[...])
        if last:
            @pl.when(i < T_PROMPT // TM_MOE)
            def _():
                out0_ref[...] = y

            @pl.when(i >= T_PROMPT // TM_MOE)
            def _():
                out1_ref[...] = y
        else:
            out0_ref[...] = y
            out1_ref[...] = (y * (1.0 + modn_ref[1]) + modn_ref[0]).astype(BF16)


def _moe(l, x, mods, router_w, router_b, w1, w3, w2, ln_g, ln_b):
    n_m = T_ALL // TM_MOE
    bps = DEC_SEQ // TM_MOE
    npb = T_PROMPT // TM_MOE
    last = l + 1 == DEPTH
    l_next = l if last else l + 1
    tok_blk = pl.BlockSpec((TM_MOE, D_MODEL), lambda i, e: (i, 0))
    if last:
        out_specs = [pl.BlockSpec((TM_MOE, D_MODEL), lambda i, e: (jnp.minimum(i, npb - 1), 0)),
                     pl.BlockSpec((TM_MOE, D_MODEL), lambda i, e: (jnp.maximum(i - npb, 0), 0))]
        out_shape = [jax.ShapeDtypeStruct((T_PROMPT, D_MODEL), F32),
                     jax.ShapeDtypeStruct((T_SAMPLE, D_MODEL), F32)]
    else:
        out_specs = [tok_blk, tok_blk]
        out_shape = [jax.ShapeDtypeStruct((T_ALL, D_MODEL), F32),
                     jax.ShapeDtypeStruct((T_ALL, D_MODEL), BF16)]
    return pl.pallas_call(
        functools.partial(_moe_kernel, last),
        name="moe",
        grid=(n_m, N_EXPERTS // EXP_STEP),
        in_specs=[
            tok_blk,
            pl.BlockSpec((None, 6, None, 1, D_MODEL),
                         lambda i, e: (l, 0, _mod_row(i, bps, npb), 0, 0)),
            pl.BlockSpec((None, 6, None, 1, D_MODEL),
                         lambda i, e: (l_next, 0, _mod_row(i, bps, npb), 0, 0)),
            pl.BlockSpec((None, D_MODEL, 2 * ROUTER_LANES), lambda i, e: (l, 0, 0)),
            pl.BlockSpec((None, 1, ROUTER_LANES), lambda i, e: (l, 0, 0)),
            pl.BlockSpec((None, EXP_STEP, D_MODEL, D_EXPERT), lambda i, e: (l, e, 0, 0)),
            pl.BlockSpec((None, EXP_STEP, D_MODEL, D_EXPERT), lambda i, e: (l, e, 0, 0)),
            pl.BlockSpec((None, EXP_STEP, D_EXPERT, D_MODEL), lambda i, e: (l, e, 0, 0)),
            pl.BlockSpec((None, 1, D_MODEL), lambda i, e: (l, 0, 0)),
            pl.BlockSpec((None, 1, D_MODEL), lambda i, e: (l, 0, 0)),
        ],
        out_specs=out_specs,
        out_shape=out_shape,
        scratch_shapes=[pltpu.VMEM((TM_MOE, D_MODEL), BF16),
                        pltpu.VMEM((TM_MOE, ROUTER_LANES), F32),
                        pltpu.VMEM((TM_MOE, D_MODEL), F32)],
        compiler_params=_cparams(("arbitrary", "arbitrary")),
    )(x, mods, mods, router_w, router_b, w1, w3, w2, ln_g, ln_b)


def _rope_tables():
    t = np.arange(DEC_SEQ)
    pos = np.stack([t // GRID_W, t % GRID_W], axis=1).astype(np.float32)
    quarter = HEAD_DIM // 4
    d = np.arange(HEAD_DIM)
    axis = d // (HEAD_DIM // 2)
    freq = d % quarter
    upper = (d % (HEAD_DIM // 2)) >= quarter
    inv = jnp.asarray(ROPE_BASE, F32) ** (-jnp.arange(0, HEAD_DIM // 2, 2, dtype=F32) / (HEAD_DIM // 2))
    ang = jnp.asarray(pos)[:, axis] * inv[freq][None, :]
    cos = jnp.cos(ang)
    sin = jnp.sin(ang)
    sin_up = jnp.where(upper[None, :], 0.0, -sin)
    sin_dn = jnp.where(upper[None, :], sin, 0.0)
    tile = lambda a: jnp.tile(a, (1, N_HEADS))
    return tile(cos), tile(sin_up), tile(sin_dn)


def _bias_tables(na_rpb):
    qc = np.arange(GRID_W)[:, None]
    kc = np.arange(GRID_W)[None, :]
    start = np.clip(qc - WIN_W // 2, 0, GRID_W - WIN_W)
    valid = (kc >= start) & (kc < start + WIN_W)
    n_dc = 2 * WIN_W - 1
    dc = kc - qc + WIN_W - 1
    hit = ((dc[None] == np.arange(n_dc)[:, None, None]) & valid[None]).astype(np.float32)
    sel = np.zeros((2 * n_dc + 1, GRID_W, 2 * GRID_W), np.float32)
    sel[:n_dc, :, :GRID_W] = hit
    sel[n_dc:2 * n_dc, :, GRID_W:] = hit
    sel[2 * n_dc] = np.where(np.concatenate([valid, valid], axis=1), 0.0, NEG_INF)
    ones = jnp.ones(na_rpb.shape[:2] + (2 * WIN_H - 2, 1), F32)
    rows = jnp.concatenate([na_rpb[:, :, :-1], na_rpb[:, :, 1:], ones], axis=-1)
    return jnp.einsum('lhrd,dqk->lhrqk', rows, jnp.asarray(sel), precision=HIGHEST)


def kernel(x_prompt, x_sample, cache_na_k, cache_na_v, c, c_ctx, w_ada, b_ada, w_in, conv_dw,
           conv_b, conv_ln_g, conv_ln_b, conv_pw, na_rpb, na_out, gm_ln_g, gm_ln_b, gm_ws, gm_bs,
           gm_out, w_o, ln1_g, ln1_b, rg_w, rg_b, re_w, re_b, moe_w1, moe_w3, moe_w2, ln2_g, ln2_b):
    cond =jnp.zeros((N_COND, D_MODEL), F32).at[0].set(c_ctx).at[1:1 + DEC_BATCH].set(c)
    mods = _ada(cond, w_ada, b_ada)
    mods = mods.reshape(DEPTH, N_COND, 6, 1, D_MODEL).transpose(0, 2, 1, 3, 4)

    gm_ws_b = gm_ws.astype(BF16)
    vec = lambda a: a.reshape(DEPTH, 1, a.shape[-1])
    gm_bs_t = gm_bs.transpose(0, 2, 1)
    router_w = jnp.concatenate(
        [rg_w, re_w.transpose(0, 2, 1, 3).reshape(DEPTH, D_MODEL, N_EXPERTS)], axis=-1)
    router_w = jnp.pad(router_w, ((0, 0), (0, 0), (0, ROUTER_LANES - N_EGROUPS - N_EXPERTS)))
    router_hi = router_w.astype(BF16)
    router_lo = (router_w - router_hi.astype(F32)).astype(BF16)
    router_w = jnp.concatenate([router_hi, router_lo], axis=-1)
    router_b = jnp.concatenate([rg_b, re_b.reshape(DEPTH, N_EXPERTS)], axis=-1)
    router_b = jnp.pad(router_b, ((0, 0), (0, ROUTER_LANES - N_EGROUPS - N_EXPERTS)))
    router_b = router_b.reshape(DEPTH, 1, ROUTER_LANES)
    cache_k = cache_na_k.reshape(DEC_BATCH, DEPTH, PAST_LEN, D_NA)
    cache_v = cache_na_v.reshape(DEC_BATCH, DEPTH, PAST_LEN, D_NA)
    tz = _bias_tables(na_rpb)
    cos, sin_up, sin_dn = _rope_tables()

    kt_all = vt_all = None
    x, h = _gather_modulate(x_prompt, x_sample, mods)
    for l in range(DEPTH):
        z = _inproj(l, h, w_in)
        yc, ug = _branches(l, z, conv_dw, vec(conv_b), vec(conv_ln_g), vec(conv_ln_b),
                           vec(gm_ln_g), vec(gm_ln_b), gm_ws_b, gm_bs_t)
        att_p, kt_all, vt_all = _ctx_attn(l, z, h, w_in, kt_all, vt_all)
        att_s = _na_attn(l, z, h, w_in, cache_k, cache_v, tz, cos, sin_up, sin_dn)
        x = _merge(l, x, mods, z, yc, ug, att_p, att_s, conv_pw, na_out, gm_out, w_o,
                   vec(ln1_g), vec(ln1_b))
        outs = _moe(l, x, mods, router_w, router_b, moe_w1, moe_w3, moe_w2, vec(ln2_g), vec(ln2_b))
        x, h = outs

    y_prompt, y_sample = outs
    return (y_prompt.reshape(BATCH, SEQ, D_MODEL), y_sample.reshape(DEC_BATCH, DEC_SEQ, D_MODEL),
            kt_all.transpose(0, 1, 4, 2, 3), vt_all.transpose(0, 1, 4, 2, 3))
```

```python
import functools

import jax
import jax.numpy as jnp
import numpy as np
from jax import lax
from jax.experimental import pallas as pl
from jax.experimental.pallas import tpu as pltpu

F32 = jnp.float32
BF16 = jnp.bfloat16
HIGHEST = lax.Precision.HIGHEST

D_MODEL = 1024
BATCH = 16
SEQ = 256
DEPTH = 4
DEC_BATCH = 2
DEC_SEQ = 1024
PAST_LEN = 256
GRID_W = 64
GRID_H = DEC_SEQ // GRID_W
D_CONV = 512
CONV_WIDTH = 31
CONV_HALF = CONV_WIDTH // 2
HEAD_DIM = 64
HEAD_PAIR = 2 * HEAD_DIM
N_HEADS = 8
D_NA = 512
WIN_H = 8
WIN_W = 16
ROPE_BASE = 10000.0
D_GM = 512
GM_CHUNK = 128
GM_GROUPS = 4
D_IN = 6656
N_EGROUPS = 4
EXP_PER_GROUP = 8
N_EXPERTS = 32
D_EXPERT = 128
ALPHA = (2 * DEPTH) ** 0.25
LN_EPS = 1e-5
NEG_INF = -1e30

T_PROMPT = BATCH * SEQ
T_SAMPLE = DEC_BATCH * DEC_SEQ
T_ALL = T_PROMPT + T_SAMPLE
N_COND = 8

COL_BLK = 512
W_CB_Q, W_CB_K, W_CB_V = 2, 3, 4
N_CB_KV = 2
N_CB_MAIN = D_IN // COL_BLK - N_CB_KV
N_CB_PRE_Q = 2
CB_Q = N_CB_MAIN - 1
CB2_AB, CB2_GUV, CB2_GZ = 0, 1, 2
COL_BLK2 = 2 * COL_BLK

TB = 256
N_TB = T_ALL // TB
N_TB_PROMPT = T_PROMPT // TB
TB_PER_SAMPLE = DEC_SEQ // TB
TBM = 512
N_TBM = T_ALL // TBM
N_TBM_PROMPT = T_PROMPT // TBM
TBM_PER_SAMPLE = DEC_SEQ // TBM
HALO = 16
CONV_ROWS = 32
SUBLANES = 8
SHIFT_ROWS = TB + 2 * HALO - SUBLANES

NA_ROWS = 4
NA_TQ = NA_ROWS * GRID_W
TM_IN = 3072
TM_MOE = 1024
EXP_STEP = 4
EXP_SUB = 4
ROUTER_LANES = 128
VMEM_LIMIT = 56 * 1024 * 1024


def _ln(x, g, b):
    mu = jnp.mean(x, axis=-1, keepdims=True)
    xc = x - mu
    var = jnp.mean(xc * xc, axis=-1, keepdims=True)
    return xc * lax.rsqrt(var + LN_EPS) * g + b


def _sigmoid(x):
    return jax.nn.sigmoid(x)


def _gelu(x):
    return jax.nn.gelu(x, approximate=True)


_NT = (((1,), (1,)), ((), ()))


def _cparams(sem):
    return pltpu.CompilerParams(dimension_semantics=sem, vmem_limit_bytes=VMEM_LIMIT)


def _ada_kernel(c_ref, w_ref, b_ref, o_ref):
    c = c_ref[...]
    s = c * _sigmoid(c)
    w = w_ref[...]
    s_hi, w_hi = s.astype(BF16), w.astype(BF16)
    s_lo = (s - s_hi.astype(F32)).astype(BF16)
    w_lo = (w - w_hi.astype(F32)).astype(BF16)
    o_ref[...] = (jnp.dot(s_hi, w_hi, preferred_element_type=F32)
                  + jnp.dot(s_lo, w_hi, preferred_element_type=F32)
                  + jnp.dot(s_hi, w_lo, preferred_element_type=F32)) + b_ref[...]


def _ada(cond, w_ada, b_ada):
    tn = 3072
    return pl.pallas_call(
        _ada_kernel,
        name="ada",
        grid=(DEPTH, 6 * D_MODEL // tn),
        in_specs=[
            pl.BlockSpec((N_COND, D_MODEL), lambda l, j: (0, 0)),
            pl.BlockSpec((None, D_MODEL, tn), lambda l, j: (l, 0, j)),
            pl.BlockSpec((None, 1, tn), lambda l, j: (l, 0, j)),
        ],
        out_specs=pl.BlockSpec((None, N_COND, tn), lambda l, j: (l, 0, j)),
        out_shape=jax.ShapeDtypeStruct((DEPTH, N_COND, 6 * D_MODEL), F32),
        compiler_params=_cparams(("arbitrary", "arbitrary")),
    )(cond, w_ada, b_ada.reshape(DEPTH, 1, 6 * D_MODEL))


def _mod_row(i, blocks_per_sample, n_prompt_blocks):
    return jnp.where(i < n_prompt_blocks, 0, 1 + (i - n_prompt_blocks) // blocks_per_sample)


def _gather_modulate_kernel(xp_ref, xs_ref, mod_ref, x_ref, h_ref):
    x = jnp.where(pl.program_id(0) < T_PROMPT // TM_MOE, xp_ref[...], xs_ref[...])
    x_ref[...] = x
    h_ref[...] = (x * (1.0 + mod_ref[1]) + mod_ref[0]).astype(BF16)


def _gather_modulate(x_prompt, x_sample, mods):
    bps = DEC_SEQ // TM_MOE
    npb = T_PROMPT // TM_MOE
    tok_blk = pl.BlockSpec((TM_MOE, D_MODEL), lambda i: (i, 0))
    return pl.pallas_call(
        _gather_modulate_kernel,
        name="modulate",
        grid=(T_ALL // TM_MOE,),
        in_specs=[
            pl.BlockSpec((TM_MOE, D_MODEL), lambda i: (jnp.minimum(i, npb - 1), 0)),
            pl.BlockSpec((TM_MOE, D_MODEL), lambda i: (jnp.maximum(i - npb, 0), 0)),
            pl.BlockSpec((None, 6, None, 1, D_MODEL),
                         lambda i: (0, 0, _mod_row(i, bps, npb), 0, 0)),
        ],
        out_specs=[tok_blk, tok_blk],
        out_shape=[jax.ShapeDtypeStruct((T_ALL, D_MODEL), F32),
                   jax.ShapeDtypeStruct((T_ALL, D_MODEL), BF16)],
        compiler_params=_cparams(("arbitrary",)),
    )(x_prompt.reshape(T_PROMPT, D_MODEL), x_sample.reshape(T_SAMPLE, D_MODEL), mods)


def _inproj_kernel(h_ref, w_ref, z_ref):
    rows = pl.ds(pl.multiple_of(pl.program_id(1) * TM_IN, TM_IN), TM_IN)
    z = jnp.dot(h_ref[rows, :], w_ref[...].astype(BF16), preferred_element_type=F32)
    z_ref[...] = z.astype(BF16)


def _inproj(l, h, w_in):
    return pl.pallas_call(
        _inproj_kernel,
        name="inproj",
        grid=(N_CB_MAIN, T_ALL // TM_IN),
        in_specs=[
            pl.BlockSpec((T_ALL, D_MODEL), lambda j, i: (0, 0)),
            pl.BlockSpec((None, D_MODEL, COL_BLK),
                         lambda j, i: (l, 0, jnp.where(j < N_CB_PRE_Q, j,
                                                       jnp.where(j < CB_Q, j + 1 + N_CB_KV, W_CB_Q)))),
        ],
        out_specs=pl.BlockSpec((TM_IN, COL_BLK), lambda j, i: (i, j)),
        out_shape=jax.ShapeDtypeStruct((T_ALL, N_CB_MAIN * COL_BLK), BF16),
        compiler_params=_cparams(("arbitrary", "arbitrary")),
    )(h, w_in)


def _branch_kernel(abp_ref, abc_ref, abn_ref, guv_ref,
                   dw_ref, cb_ref, clg_ref, clb_ref, glg_ref, glb_ref, ws_ref, bst_ref,
                   yc_ref, ug_ref, ypad_ref, ysh_ref):
    i = pl.program_id(0)
    j = i - N_TB_PROMPT
    in_sample = i >= N_TB_PROMPT
    has_prev = jnp.logical_and(in_sample, j % TB_PER_SAMPLE != 0)
    has_next = jnp.logical_and(in_sample, j % TB_PER_SAMPLE != TB_PER_SAMPLE - 1)

    def glu(ab_ref):
        return (ab_ref[:, :D_CONV].astype(F32) * _sigmoid(ab_ref[:, D_CONV:].astype(F32)))

    ypad_ref[0:HALO, :] = jnp.where(has_prev, glu(abp_ref), 0.0)
    ypad_ref[HALO:HALO + TB, :] = glu(abc_ref)
    ypad_ref[HALO + TB:HALO + TB + HALO, :] = jnp.where(has_next, glu(abn_ref), 0.0)

    for b in range(SUBLANES):
        ysh_ref[b] = ypad_ref[b:b + SHIFT_ROWS, :]

    off = HALO - CONV_HALF
    for c in range(TB // CONV_ROWS):
        base = c * CONV_ROWS
        acc = jnp.zeros((CONV_ROWS, D_CONV), F32)
        for k in range(CONV_WIDTH):
            tile, phase = divmod(off + k, SUBLANES)
            start = base + tile * SUBLANES
            acc = acc + ysh_ref[phase, start:start + CONV_ROWS, :] * dw_ref[k:k + 1, :]
        y = _ln(acc + cb_ref[...], clg_ref[...], clb_ref[...])
        yc_ref[base:base + CONV_ROWS, :] = (y * _sigmoid(y)).astype(BF16)

    for n in range(TB // GM_CHUNK):
        rows = slice(n * GM_CHUNK, (n + 1) * GM_CHUNK)
        u = _gelu(guv_ref[rows, :D_GM].astype(F32))
        v = _ln(_gelu(guv_ref[rows, D_GM:].astype(F32)), glg_ref[...], glb_ref[...]).astype(BF16)
        for g in range(GM_GROUPS):
            cols = slice(g * GM_CHUNK, (g + 1) * GM_CHUNK)
            sv = jnp.dot(ws_ref[g], v[:, cols], preferred_element_type=F32) + bst_ref[:, g:g + 1]
            ug_ref[rows, cols] = (u[:, cols] * sv).astype(BF16)


def _branches(l, z, conv_dw, conv_b, conv_ln_g, conv_ln_b, gm_ln_g, gm_ln_b, gm_ws, gm_bs_t):
    halo_per_tb = TB // HALO
    n_halo = T_ALL // HALO

    def cur(cb):
        return pl.BlockSpec((TB, COL_BLK2), lambda i: (i, cb))

    def prev(cb):
        return pl.BlockSpec((HALO, COL_BLK2), lambda i: (jnp.maximum(i * halo_per_tb - 1, 0), cb))

    def nxt(cb):
        return pl.BlockSpec((HALO, COL_BLK2),
                            lambda i: (jnp.minimum((i + 1) * halo_per_tb, n_halo - 1), cb))

    def vec(n):
        return pl.BlockSpec((None, 1, n), lambda i: (l, 0, 0))

    return pl.pallas_call(
        _branch_kernel,
        name="branches",
        grid=(N_TB,),
        in_specs=[
            prev(CB2_AB), cur(CB2_AB), nxt(CB2_AB), cur(CB2_GUV),
            pl.BlockSpec((None, CONV_WIDTH, D_CONV), lambda i: (l, 0, 0)),
            vec(D_CONV), vec(D_CONV), vec(D_CONV), vec(D_GM), vec(D_GM),
            pl.BlockSpec((None, GM_GROUPS, GM_CHUNK, GM_CHUNK), lambda i: (l, 0, 0, 0)),
            pl.BlockSpec((None, GM_CHUNK, GM_GROUPS), lambda i: (l, 0, 0)),
        ],
        out_specs=[pl.BlockSpec((TB, D_CONV), lambda i: (i, 0)),
                   pl.BlockSpec((TB, D_GM), lambda i: (i, 0))],
        out_shape=[jax.ShapeDtypeStruct((T_ALL, D_CONV), BF16),
                   jax.ShapeDtypeStruct((T_ALL, D_GM), BF16)],
        scratch_shapes=[pltpu.VMEM((TB + 2 * HALO, D_CONV), F32),
                        pltpu.VMEM((SUBLANES, SHIFT_ROWS, D_CONV), F32)],
        compiler_params=_cparams(("arbitrary",)),
    )(z, z, z, z, conv_dw, conv_b, conv_ln_g, conv_ln_b, gm_ln_g, gm_ln_b, gm_ws, gm_bs_t)


def _ctx_attn_kernel(first, q_ref, h_ref, wk_ref, wv_ref, *rest):
    o_ref, ko_ref, vo_ref, wkt_ref, wvt_ref = rest[-5:]

    @pl.when(pl.program_id(0) == 0)
    def _():
        wkt_ref[...] = wk_ref[...].T.astype(BF16)
        wvt_ref[...] = wv_ref[...].T.astype(BF16)

    h = h_ref[...]
    kt = lax.dot_general(wkt_ref[...], h, _NT, preferred_element_type=F32)
    vt = lax.dot_general(wvt_ref[...], h, _NT, preferred_element_type=F32)
    if first:
        ko_ref[0] = kt.reshape(N_HEADS, HEAD_DIM, SEQ)
        vo_ref[0] = vt.reshape(N_HEADS, HEAD_DIM, SEQ)
        ko_ref[1:] = jnp.zeros((DEPTH - 1, N_HEADS, HEAD_DIM, SEQ), F32)
        vo_ref[1:] = jnp.zeros((DEPTH - 1, N_HEADS, HEAD_DIM, SEQ), F32)
    else:
        ko_ref[...] = kt.reshape(N_HEADS, HEAD_DIM, SEQ)
        vo_ref[...] = vt.reshape(N_HEADS, HEAD_DIM, SEQ)
    q = (q_ref[...].astype(F32) * HEAD_DIM ** -0.5).astype(BF16)
    kb = kt.astype(BF16)
    vb = vt.astype(BF16)
    lower = lax.broadcasted_iota(jnp.int32, (SEQ, HEAD_PAIR), 1) < HEAD_DIM
    upper = jnp.logical_not(lower)
    heads = range(N_HEADS)
    grp = [slice(h // 2 * HEAD_PAIR, (h // 2 + 1) * HEAD_PAIR) for h in heads]
    qh = [jnp.where(lower if h % 2 == 0 else upper, q[:, grp[h]],
                    jnp.zeros((SEQ, HEAD_PAIR), BF16)) for h in heads]
    s = [jnp.dot(qh[h], kb[grp[h], :], preferred_element_type=F32) for h in heads]
    m = [jnp.max(s[h], axis=-1, keepdims=True) for h in heads]
    p = [jnp.exp(s[h] - m[h]) for h in heads]
    den = [jnp.sum(p[h], axis=-1, keepdims=True) for h in heads]
    o = [lax.dot_general(p[h].astype(BF16), vb[grp[h], :], _NT, preferred_element_type=F32) / den[h]
         for h in heads]
    for h in range(0, N_HEADS, 2):
        o_ref[:, grp[h]] = jnp.where(lower, o[h], o[h + 1]).astype(BF16)


def _ctx_attn(l, z, h, w_in, kt_all=None, vt_all=None):
    first = kt_all is None
    if first:
        cache_blk = pl.BlockSpec((None, DEPTH, N_HEADS, HEAD_DIM, SEQ), lambda b: (b, 0, 0, 0, 0))
        carried, carried_specs, aliases = (), [], {}
    else:
        cache_blk = pl.BlockSpec((None, None, N_HEADS, HEAD_DIM, SEQ), lambda b: (b, l, 0, 0, 0))
        carried = (kt_all, vt_all)
        carried_specs = [pl.BlockSpec(memory_space=pl.ANY), pl.BlockSpec(memory_space=pl.ANY)]
        aliases = {4: 1, 5: 2}
    cache_shape = jax.ShapeDtypeStruct((BATCH, DEPTH, N_HEADS, HEAD_DIM, SEQ), F32)
    return pl.pallas_call(
        functools.partial(_ctx_attn_kernel, first),
        name="ctx_attn",
        grid=(BATCH,),
        in_specs=[pl.BlockSpec((SEQ, COL_BLK), lambda b: (b, CB_Q)),
                  pl.BlockSpec((SEQ, D_MODEL), lambda b: (b, 0)),
                  pl.BlockSpec((None, D_MODEL, COL_BLK), lambda b: (l, 0, W_CB_K)),
                  pl.BlockSpec((None, D_MODEL, COL_BLK), lambda b: (l, 0, W_CB_V))] + carried_specs,
        out_specs=[pl.BlockSpec((SEQ, D_NA), lambda b: (b, 0)), cache_blk, cache_blk],
        out_shape=[jax.ShapeDtypeStruct((T_PROMPT, D_NA), BF16), cache_shape, cache_shape],
        scratch_shapes=[pltpu.VMEM((COL_BLK, D_MODEL), BF16), pltpu.VMEM((COL_BLK, D_MODEL), BF16)],
        input_output_aliases=aliases,
        compiler_params=_cparams(("arbitrary",)),
    )(z, h, w_in, w_in, *carried)


def _rope(x, cos, sin_up, sin_dn):
    return (x * cos + pltpu.roll(x, D_NA - HEAD_DIM // 4, 1) * sin_up
            + pltpu.roll(x, HEAD_DIM // 4, 1) * sin_dn)


def _na_attn_kernel(q_ref, h_ref, wk_ref, wv_ref, ck_ref, cv_ref, tz_ref, cos_ref, sup_ref,
                    sdn_ref, o_ref, krot_ref, vb_ref, ckb_ref, cvb_ref):
    step = pl.program_id(1)

    @pl.when(step == 0)
    def _():
        h = h_ref[...]
        k = jnp.dot(h, wk_ref[...].astype(BF16), preferred_element_type=F32)
        v = jnp.dot(h, wv_ref[...].astype(BF16), preferred_element_type=F32)
        krot_ref[...] = _rope(k, cos_ref[...], sup_ref[...], sdn_ref[...]).astype(BF16)
        vb_ref[...] = v.astype(BF16)
        ckb_ref[...] = ck_ref[...].astype(BF16)
        cvb_ref[...] = cv_ref[...].astype(BF16)

    qrows = pl.ds(pl.multiple_of(step * NA_TQ, NA_TQ), NA_TQ)
    q = _rope(q_ref[...].astype(F32), cos_ref[qrows, :], sup_ref[qrows, :], sdn_ref[qrows, :])
    q = (q * HEAD_DIM ** -0.5).astype(BF16)

    lower = lax.broadcasted_iota(jnp.int32, (GRID_W, HEAD_PAIR), 1) < HEAD_DIM
    upper = jnp.logical_not(lower)
    kwin, vwin, dr0 = [], [], []
    for j in range(NA_ROWS):
        r = step * NA_ROWS + j
        row_start = jnp.clip(r - WIN_H // 2, 0, GRID_H - WIN_H)
        krows = pl.ds(pl.multiple_of(row_start * GRID_W, GRID_W), WIN_H * GRID_W)
        kwin.append(krot_ref[krows, :])
        vwin.append(vb_ref[krows, :])
        dr0.append(row_start - r + WIN_H - 1)

    units = [(j, h) for j in range(NA_ROWS) for h in range(N_HEADS)]
    grp = [slice(h // 2 * HEAD_PAIR, (h // 2 + 1) * HEAD_PAIR) for _, h in units]
    qh = [jnp.where(lower if h % 2 == 0 else upper, q[j * GRID_W:(j + 1) * GRID_W, grp[u]],
                    jnp.zeros((GRID_W, HEAD_PAIR), BF16)) for u, (j, h) in enumerate(units)]
    s_loc = [lax.dot_general(qh[u], kwin[j][:, grp[u]], _NT, preferred_element_type=F32)
             + jnp.concatenate([tz_ref[h, dr0[j] + w] for w in range(0, WIN_H, 2)], axis=1)
             for u, (j, h) in enumerate(units)]
    s_ctx = [lax.dot_general(qh[u], ckb_ref[:, grp[u]], _NT, preferred_element_type=F32)
             for u in range(len(units))]
    m = [jnp.maximum(jnp.max(s_loc[u], axis=-1, keepdims=True),
                     jnp.max(s_ctx[u], axis=-1, keepdims=True)) for u in range(len(units))]
    p_loc = [jnp.exp(s_loc[u] - m[u]) for u in range(len(units))]
    p_ctx = [jnp.exp(s_ctx[u] - m[u]) for u in range(len(units))]
    den = [jnp.sum(p_loc[u], axis=-1, keepdims=True) + jnp.sum(p_ctx[u], axis=-1, keepdims=True)
           for u in range(len(units))]
    o = [(jnp.dot(p_loc[u].astype(BF16), vwin[j][:, grp[u]], preferred_element_type=F32)
          + jnp.dot(p_ctx[u].astype(BF16), cvb_ref[:, grp[u]], preferred_element_type=F32)) / den[u]
         for u, (j, h) in enumerate(units)]
    for u, (j, h) in enumerate(units):
        if h % 2 == 0:
            o_ref[j * GRID_W:(j + 1) * GRID_W, grp[u]] = jnp.where(lower, o[u], o[u + 1]).astype(BF16)


def _na_attn(l, z, h, w_in, cache_k, cache_v, tz, cos, sin_up, sin_dn):
    seq_blk0 = T_PROMPT // DEC_SEQ
    row_blk0 = T_PROMPT // NA_TQ
    steps = GRID_H // NA_ROWS
    full = pl.BlockSpec((DEC_SEQ, D_NA), lambda b, r: (0, 0))
    return pl.pallas_call(
        _na_attn_kernel,
        name="na_attn",
        grid=(DEC_BATCH, steps),
        in_specs=[
            pl.BlockSpec((NA_TQ, COL_BLK), lambda b, r: (row_blk0 + b * steps + r, CB_Q)),
            pl.BlockSpec((DEC_SEQ, D_MODEL), lambda b, r: (seq_blk0 + b, 0)),
            pl.BlockSpec((None, D_MODEL, COL_BLK), lambda b, r: (l, 0, W_CB_K)),
            pl.BlockSpec((None, D_MODEL, COL_BLK), lambda b, r: (l, 0, W_CB_V)),
            pl.BlockSpec((None, None, PAST_LEN, D_NA), lambda b, r: (b, l, 0, 0)),
            pl.BlockSpec((None, None, PAST_LEN, D_NA), lambda b, r: (b, l, 0, 0)),
            pl.BlockSpec((None, N_HEADS, 2 * WIN_H - 2, GRID_W, 2 * GRID_W),
                         lambda b, r: (l, 0, 0, 0, 0)),
            full, full, full,
        ],
        out_specs=pl.BlockSpec((NA_TQ, D_NA), lambda b, r: (b * steps + r, 0)),
        out_shape=jax.ShapeDtypeStruct((T_SAMPLE, D_NA), BF16),
        scratch_shapes=[pltpu.VMEM((DEC_SEQ, D_NA), BF16), pltpu.VMEM((DEC_SEQ, D_NA), BF16),
                        pltpu.VMEM((PAST_LEN, D_NA), BF16), pltpu.VMEM((PAST_LEN, D_NA), BF16)],
        compiler_params=_cparams(("arbitrary", "arbitrary")),
    )(z, h, w_in, w_in, cache_k, cache_v, tz, cos, sin_up, sin_dn)


def _merge_kernel(x_ref, mod_ref, gc_ref, ga_ref, gg_ref, yc_ref, ug_ref, ap_ref, as_ref,
                  pw_ref, no_ref, go_ref, wo_ref, lg_ref, lb_ref, o_ref,
                  pwb_ref, nob_ref, gob_ref, wob_ref):
    i = pl.program_id(0)

    @pl.when(i == 0)
    def _():
        pwb_ref[...] = pw_ref[...].astype(BF16)
        nob_ref[...] = no_ref[...].astype(BF16)
        gob_ref[...] = go_ref[...].astype(BF16)
        wob_ref[...] = wo_ref[...].astype(BF16)

    att = jnp.where(i < N_TBM_PROMPT, ap_ref[...], as_ref[...])
    br_c = jnp.dot(yc_ref[...], pwb_ref[...], preferred_element_type=F32)
    br_a = jnp.dot(att, nob_ref[...], preferred_element_type=F32)
    br_g = jnp.dot(ug_ref[...], gob_ref[...], preferred_element_type=F32)
    merged = (_sigmoid(gc_ref[...].astype(F32)) * br_c + _sigmoid(ga_ref[...].astype(F32)) * br_a
              + _sigmoid(gg_ref[...].astype(F32)) * br_g)
    mix = jnp.dot(merged.astype(BF16), wob_ref[...], preferred_element_type=F32)
    o_ref[...] = _ln(ALPHA * x_ref[...] + mod_ref[2] * mix, lg_ref[...], lb_ref[...])


def _merge(l, x, mods, z, yc, ug, att_p, att_s, conv_pw, na_out, gm_out, w_o, ln_g, ln_b):
    def gz(k):
        return pl.BlockSpec((TBM, COL_BLK2), lambda i: (i, CB2_GZ + k))

    def w(k, n):
        return pl.BlockSpec((None, k, n), lambda i: (l, 0, 0))

    blk512 = pl.BlockSpec((TBM, COL_BLK), lambda i: (i, 0))
    return pl.pallas_call(
        _merge_kernel,
        name="merge",
        grid=(N_TBM,),
        in_specs=[
            pl.BlockSpec((TBM, D_MODEL), lambda i: (i, 0)),
            pl.BlockSpec((None, 6, None, 1, D_MODEL),
                         lambda i: (l, 0, _mod_row(i, TBM_PER_SAMPLE, N_TBM_PROMPT), 0, 0)),
            gz(0), gz(1), gz(2),
            blk512, blk512,
            pl.BlockSpec((TBM, D_NA), lambda i: (jnp.minimum(i, N_TBM_PROMPT - 1), 0)),
            pl.BlockSpec((TBM, D_NA), lambda i: (jnp.maximum(i - N_TBM_PROMPT, 0), 0)),
            w(D_CONV, D_MODEL), w(D_NA, D_MODEL), w(D_GM, D_MODEL), w(D_MODEL, D_MODEL),
            w(1, D_MODEL), w(1, D_MODEL),
        ],
        out_specs=pl.BlockSpec((TBM, D_MODEL), lambda i: (i, 0)),
        out_shape=jax.ShapeDtypeStruct((T_ALL, D_MODEL), F32),
        scratch_shapes=[pltpu.VMEM((D_CONV, D_MODEL), BF16), pltpu.VMEM((D_NA, D_MODEL), BF16),
                        pltpu.VMEM((D_GM, D_MODEL), BF16), pltpu.VMEM((D_MODEL, D_MODEL), BF16)],
        compiler_params=_cparams(("arbitrary",)),
    )(x, mods, z, z, z, yc, ug, att_p, att_s, conv_pw, na_out, gm_out, w_o, ln_g, ln_b)


def _route(logits):
    lane = lax.broadcasted_iota(jnp.int32, logits.shape, 1)
    big = jnp.int32(ROUTER_LANES)
    is_g = lane < N_EGROUPS
    gl = jnp.where(is_g, logits, -jnp.inf)
    gmax = jnp.max(gl, axis=-1, keepdims=True)
    gidx = jnp.min(jnp.where(gl == gmax, lane, big), axis=-1, keepdims=True)
    gp = 1.0 / jnp.sum(jnp.where(is_g, jnp.exp(gl - gmax), 0.0), axis=-1, keepdims=True)
    lo = N_EGROUPS + gidx * EXP_PER_GROUP
    el = jnp.where(jnp.logical_and(lane >= lo, lane < lo + EXP_PER_GROUP), logits, -jnp.inf)
    v1 = jnp.max(el, axis=-1, keepdims=True)
    i1 = jnp.min(jnp.where(el == v1, lane, big), axis=-1, keepdims=True)
    el2 = jnp.where(lane == i1, -jnp.inf, el)
    v2 = jnp.max(el2, axis=-1, keepdims=True)
    i2 = jnp.min(jnp.where(el2 == v2, lane, big), axis=-1, keepdims=True)
    e2 = jnp.exp(v2 - v1)
    w1 = gp / (1.0 + e2)
    w2 = gp * e2 / (1.0 + e2)
    return jnp.where(lane == i1, w1, 0.0) + jnp.where(lane == i2, w2, 0.0)


def _split_bf16(a):
    hi = a.astype(BF16)
    return hi, (a - hi.astype(F32)).astype(BF16)


def _moe_kernel(last, x_ref, mod_ref, modn_ref, rw_ref, rb_ref, w1_ref, w3_ref, w2_ref, lg_ref,
                lb_ref, out0_ref, out1_ref, t_ref, gate_ref, acc_ref):
    i = pl.program_id(0)
    e = pl.program_id(1)

    @pl.when(e == 0)
    def _():
        t = x_ref[...] * (1.0 + mod_ref[4]) + mod_ref[3]
        t_hi, t_lo = _split_bf16(t)
        t_ref[...] = t_hi
        both = jnp.dot(t_hi, rw_ref[...], preferred_element_type=F32)
        logits = (both[:, :ROUTER_LANES] + both[:, ROUTER_LANES:]
                  + jnp.dot(t_lo, rw_ref[:, :ROUTER_LANES], preferred_element_type=F32))
        gate_ref[...] = _route(logits + rb_ref[...])
        acc_ref[...] = jnp.zeros_like(acc_ref)

    t = t_ref[...]
    gate = gate_ref[...]
    lane = lax.broadcasted_iota(jnp.int32, gate.shape, 1)
    hcol = lax.broadcasted_iota(jnp.int32, (t.shape[0], EXP_SUB * D_EXPERT), 1) // D_EXPERT
    for k0 in range(0, EXP_STEP, EXP_SUB):
        ks = range(k0, k0 + EXP_SUB)
        w1 = jnp.concatenate([w1_ref[k] for k in ks], axis=1).astype(BF16)
        w3 = jnp.concatenate([w3_ref[k] for k in ks], axis=1).astype(BF16)
        w2 = w2_ref[k0:k0 + EXP_SUB].reshape(EXP_SUB * D_EXPERT, D_MODEL).astype(BF16)
        h1 = jnp.dot(t, w1, preferred_element_type=F32)
        h3 = jnp.dot(t, w3, preferred_element_type=F32)
        gmul = jnp.zeros(h1.shape, F32)
        for k in ks:
            gcol = jnp.sum(jnp.where(lane == e * EXP_STEP + k + N_EGROUPS, gate, 0.0),
                           axis=-1, keepdims=True)
            gmul = jnp.where(hcol == k - k0, gcol, gmul)
        hid = (h1 * _sigmoid(h1) * h3 * gmul).astype(BF16)
        acc_ref[...] += jnp.dot(hid, w2, preferred_element_type=F32)

    @pl.when(e == N_EXPERTS // EXP_STEP - 1)
    def _():
        y = _ln(ALPHA * x_ref[...] + mod_ref[5] * acc_ref[...], lg_ref[...], lb_ref[...])
        if last:
            @pl.when(i < T_PROMPT // TM_MOE)
            def _():
                out0_ref[...] = y

            @pl.when(i >= T_PROMPT // TM_MOE)
            def _():
                out1_ref[...] = y
        else:
            out0_ref[...] = y
            out1_ref[...] = (y * (1.0 + modn_ref[1]) + modn_ref[0]).astype(BF16)


def _moe(l, x, mods, router_w, router_b, w1, w3, w2, ln_g, ln_b):
    n_m = T_ALL // TM_MOE
    bps = DEC_SEQ // TM_MOE
    npb = T_PROMPT // TM_MOE
    last = l + 1 == DEPTH
    l_next = l if last else l + 1
    tok_blk = pl.BlockSpec((TM_MOE, D_MODEL), lambda i, e: (i, 0))
    if last:
        out_specs = [pl.BlockSpec((TM_MOE, D_MODEL), lambda i, e: (jnp.minimum(i, npb - 1), 0)),
                     pl.BlockSpec((TM_MOE, D_MODEL), lambda i, e: (jnp.maximum(i - npb, 0), 0))]
        out_shape = [jax.ShapeDtypeStruct((T_PROMPT, D_MODEL), F32),
                     jax.ShapeDtypeStruct((T_SAMPLE, D_MODEL), F32)]
    else:
        out_specs = [tok_blk, tok_blk]
        out_shape = [jax.ShapeDtypeStruct((T_ALL, D_MODEL), F32),
                     jax.ShapeDtypeStruct((T_ALL, D_MODEL), BF16)]
    return pl.pallas_call(
        functools.partial(_moe_kernel, last),
        name="moe",
        grid=(n_m, N_EXPERTS // EXP_STEP),
        in_specs=[
            tok_blk,
            pl.BlockSpec((None, 6, None, 1, D_MODEL),
                         lambda i, e: (l, 0, _mod_row(i, bps, npb), 0, 0)),
            pl.BlockSpec((None, 6, None, 1, D_MODEL),
                         lambda i, e: (l_next, 0, _mod_row(i, bps, npb), 0, 0)),
            pl.BlockSpec((None, D_MODEL, 2 * ROUTER_LANES), lambda i, e: (l, 0, 0)),
            pl.BlockSpec((None, 1, ROUTER_LANES), lambda i, e: (l, 0, 0)),
            pl.BlockSpec((None, EXP_STEP, D_MODEL, D_EXPERT), lambda i, e: (l, e, 0, 0)),
            pl.BlockSpec((None, EXP_STEP, D_MODEL, D_EXPERT), lambda i, e: (l, e, 0, 0)),
            pl.BlockSpec((None, EXP_STEP, D_EXPERT, D_MODEL), lambda i, e: (l, e, 0, 0)),
            pl.BlockSpec((None, 1, D_MODEL), lambda i, e: (l, 0, 0)),
            pl.BlockSpec((None, 1, D_MODEL), lambda i, e: (l, 0, 0)),
        ],
        out_specs=out_specs,
        out_shape=out_shape,
        scratch_shapes=[pltpu.VMEM((TM_MOE, D_MODEL), BF16),
                        pltpu.VMEM((TM_MOE, ROUTER_LANES), F32),
                        pltpu.VMEM((TM_MOE, D_MODEL), F32)],
        compiler_params=_cparams(("arbitrary", "arbitrary")),
    )(x, mods, mods, router_w, router_b, w1, w3, w2, ln_g, ln_b)


def _rope_tables():
    t = np.arange(DEC_SEQ)
    pos = np.stack([t // GRID_W, t % GRID_W], axis=1).astype(np.float32)
    quarter = HEAD_DIM // 4
    d = np.arange(HEAD_DIM)
    axis = d // (HEAD_DIM // 2)
    freq = d % quarter
    upper = (d % (HEAD_DIM // 2)) >= quarter
    inv = jnp.asarray(ROPE_BASE, F32) ** (-jnp.arange(0, HEAD_DIM // 2, 2, dtype=F32) / (HEAD_DIM // 2))
    ang = jnp.asarray(pos)[:, axis] * inv[freq][None, :]
    cos = jnp.cos(ang)
    sin = jnp.sin(ang)
    sin_up = jnp.where(upper[None, :], 0.0, -sin)
    sin_dn = jnp.where(upper[None, :], sin, 0.0)
    tile = lambda a: jnp.tile(a, (1, N_HEADS))
    return tile(cos), tile(sin_up), tile(sin_dn)


def _bias_tables(na_rpb):
    qc = np.arange(GRID_W)[:, None]
    kc = np.arange(GRID_W)[None, :]
    start = np.clip(qc - WIN_W // 2, 0, GRID_W - WIN_W)
    valid = (kc >= start) & (kc < start + WIN_W)
    n_dc = 2 * WIN_W - 1
    dc = kc - qc + WIN_W - 1
    hit = ((dc[None] == np.arange(n_dc)[:, None, None]) & valid[None]).astype(np.float32)
    sel = np.zeros((2 * n_dc + 1, GRID_W, 2 * GRID_W), np.float32)
    sel[:n_dc, :, :GRID_W] = hit
    sel[n_dc:2 * n_dc, :, GRID_W:] = hit
    sel[2 * n_dc] = np.where(np.concatenate([valid, valid], axis=1), 0.0, NEG_INF)
    ones = jnp.ones(na_rpb.shape[:2] + (2 * WIN_H - 2, 1), F32)
    rows = jnp.concatenate([na_rpb[:, :, :-1], na_rpb[:, :, 1:], ones], axis=-1)
    return jnp.einsum('lhrd,dqk->lhrqk', rows, jnp.asarray(sel), precision=HIGHEST)


def kernel(x_prompt, x_sample, cache_na_k, cache_na_v, c, c_ctx, w_ada, b_ada, w_in, conv_dw,
           conv_b, conv_ln_g, conv_ln_b, conv_pw, na_rpb, na_out, gm_ln_g, gm_ln_b, gm_ws, gm_bs,
           gm_out, w_o, ln1_g, ln1_b, rg_w, rg_b, re_w, re_b, moe_w1, moe_w3, moe_w2, ln2_g, ln2_b):
    cond =jnp.zeros((N_COND, D_MODEL), F32).at[0].set(c_ctx).at[1:1 + DEC_BATCH].set(c)
    mods = _ada(cond, w_ada, b_ada)
    mods = mods.reshape(DEPTH, N_COND, 6, 1, D_MODEL).transpose(0, 2, 1, 3, 4)

    gm_ws_b = gm_ws.astype(BF16)
    vec = lambda a: a.reshape(DEPTH, 1, a.shape[-1])
    gm_bs_t = gm_bs.transpose(0, 2, 1)
    router_w = jnp.concatenate(
        [rg_w, re_w.transpose(0, 2, 1, 3).reshape(DEPTH, D_MODEL, N_EXPERTS)], axis=-1)
    router_w = jnp.pad(router_w, ((0, 0), (0, 0), (0, ROUTER_LANES - N_EGROUPS - N_EXPERTS)))
    router_hi = router_w.astype(BF16)
    router_lo = (router_w - router_hi.astype(F32)).astype(BF16)
    router_w = jnp.concatenate([router_hi, router_lo], axis=-1)
    router_b = jnp.concatenate([rg_b, re_b.reshape(DEPTH, N_EXPERTS)], axis=-1)
    router_b = jnp.pad(router_b, ((0, 0), (0, ROUTER_LANES - N_EGROUPS - N_EXPERTS)))
    router_b = router_b.reshape(DEPTH, 1, ROUTER_LANES)
    cache_k = cache_na_k.reshape(DEC_BATCH, DEPTH, PAST_LEN, D_NA)
    cache_v = cache_na_v.reshape(DEC_BATCH, DEPTH, PAST_LEN, D_NA)
    tz = _bias_tables(na_rpb)
    cos, sin_up, sin_dn = _rope_tables()

    kt_all = vt_all = None
    x, h = _gather_modulate(x_prompt, x_sample, mods)
    for l in range(DEPTH):
        z = _inproj(l, h, w_in)
        yc, ug = _branches(l, z, conv_dw, vec(conv_b), vec(conv_ln_g), vec(conv_ln_b),
                           vec(gm_ln_g), vec(gm_ln_b), gm_ws_b, gm_bs_t)
        att_p, kt_all, vt_all = _ctx_attn(l, z, h, w_in, kt_all, vt_all)
        att_s = _na_attn(l, z, h, w_in, cache_k, cache_v, tz, cos, sin_up, sin_dn)
        x = _merge(l, x, mods, z, yc, ug, att_p, att_s, conv_pw, na_out, gm_out, w_o,
                   vec(ln1_g), vec(ln1_b))
        outs = _moe(l, x, mods, router_w, router_b, moe_w1, moe_w3, moe_w2, vec(ln2_g), vec(ln2_b))
        x, h = outs

    y_prompt, y_sample = outs
    return (y_prompt.reshape(BATCH, SEQ, D_MODEL), y_sample.reshape(DEC_BATCH, DEC_SEQ, D_MODEL),
            kt_all.transpose(0, 1, 4, 2, 3), vt_all.transpose(0, 1, 4, 2, 3))
```

```python
import functools

import jax
import jax.numpy as jnp
import numpy as np
from jax import lax
from jax.experimental import pallas as pl
from jax.experimental.pallas import tpu as pltpu

F32 = jnp.float32
BF16 = jnp.bfloat16
HIGHEST = lax.Precision.HIGHEST

D_MODEL = 1024
BATCH = 16
SEQ = 256
DEPTH = 4
DEC_BATCH = 2
DEC_SEQ = 1024
PAST_LEN = 256
GRID_W = 64
GRID_H = DEC_SEQ // GRID_W
D_CONV = 512
CONV_WIDTH = 31
CONV_HALF = CONV_WIDTH // 2
HEAD_DIM = 64
HEAD_PAIR = 2 * HEAD_DIM
N_HEADS = 8
D_NA = 512
WIN_H = 8
WIN_W = 16
ROPE_BASE = 10000.0
D_GM = 512
GM_CHUNK = 128
GM_GROUPS = 4
D_IN = 6656
N_EGROUPS = 4
EXP_PER_GROUP = 8
N_EXPERTS = 32
D_EXPERT = 128
ALPHA = (2 * DEPTH) ** 0.25
LN_EPS = 1e-5
NEG_INF = -1e30

T_PROMPT = BATCH * SEQ
T_SAMPLE = DEC_BATCH * DEC_SEQ
T_ALL = T_PROMPT + T_SAMPLE
N_COND = 8

COL_BLK = 512
W_CB_Q, W_CB_K, W_CB_V = 2, 3, 4
N_CB_KV = 2
N_CB_MAIN = D_IN // COL_BLK - N_CB_KV
N_CB_PRE_Q = 2
CB_Q = N_CB_MAIN - 1
CB2_AB, CB2_GUV, CB2_GZ = 0, 1, 2
COL_BLK2 = 2 * COL_BLK

TB = 256
N_TB = T_ALL // TB
N_TB_PROMPT = T_PROMPT // TB
TB_PER_SAMPLE = DEC_SEQ // TB
TBM = 512
N_TBM = T_ALL // TBM
N_TBM_PROMPT = T_PROMPT // TBM
TBM_PER_SAMPLE = DEC_SEQ // TBM
HALO = 16
CONV_ROWS = 32
SUBLANES = 8
SHIFT_ROWS = TB + 2 * HALO - SUBLANES

NA_ROWS = 4
NA_TQ = NA_ROWS * GRID_W
TM_IN = 3072
TM_MOE = 1024
EXP_STEP = 4
EXP_SUB = 4
ROUTER_LANES = 128
VMEM_LIMIT = 56 * 1024 * 1024


def _ln(x, g, b):
    mu = jnp.mean(x, axis=-1, keepdims=True)
    xc = x - mu
    var = jnp.mean(xc * xc, axis=-1, keepdims=True)
    return xc * lax.rsqrt(var + LN_EPS) * g + b


def _sigmoid(x):
    return jax.nn.sigmoid(x)


def _gelu(x):
    return jax.nn.gelu(x, approximate=True)


_NT = (((1,), (1,)), ((), ()))


def _cparams(sem):
    return pltpu.CompilerParams(dimension_semantics=sem, vmem_limit_bytes=VMEM_LIMIT)


def _ada_kernel(c_ref, w_ref, b_ref, o_ref):
    c = c_ref[...]
    s = c * _sigmoid(c)
    w = w_ref[...]
    s_hi, w_hi = s.astype(BF16), w.astype(BF16)
    s_lo = (s - s_hi.astype(F32)).astype(BF16)
    w_lo = (w - w_hi.astype(F32)).astype(BF16)
    o_ref[...] = (jnp.dot(s_hi, w_hi, preferred_element_type=F32)
                  + jnp.dot(s_lo, w_hi, preferred_element_type=F32)
                  + jnp.dot(s_hi, w_lo, preferred_element_type=F32)) + b_ref[...]


def _ada(cond, w_ada, b_ada):
    tn = 3072
    return pl.pallas_call(
        _ada_kernel,
        name="ada",
        grid=(DEPTH, 6 * D_MODEL // tn),
        in_specs=[
            pl.BlockSpec((N_COND, D_MODEL), lambda l, j: (0, 0)),
            pl.BlockSpec((None, D_MODEL, tn), lambda l, j: (l, 0, j)),
            pl.BlockSpec((None, 1, tn), lambda l, j: (l, 0, j)),
        ],
        out_specs=pl.BlockSpec((None, N_COND, tn), lambda l, j: (l, 0, j)),
        out_shape=jax.ShapeDtypeStruct((DEPTH, N_COND, 6 * D_MODEL), F32),
        compiler_params=_cparams(("arbitrary", "arbitrary")),
    )(cond, w_ada, b_ada.reshape(DEPTH, 1, 6 * D_MODEL))


def _mod_row(i, blocks_per_sample, n_prompt_blocks):
    return jnp.where(i < n_prompt_blocks, 0, 1 + (i - n_prompt_blocks) // blocks_per_sample)


def _gather_modulate_kernel(xp_ref, xs_ref, mod_ref, x_ref, h_ref):
    x = jnp.where(pl.program_id(0) < T_PROMPT // TM_MOE, xp_ref[...], xs_ref[...])
    x_ref[...] = x
    h_ref[...] = (x * (1.0 + mod_ref[1]) + mod_ref[0]).astype(BF16)


def _gather_modulate(x_prompt, x_sample, mods):
    bps = DEC_SEQ // TM_MOE
    npb = T_PROMPT // TM_MOE
    tok_blk = pl.BlockSpec((TM_MOE, D_MODEL), lambda i: (i, 0))
    return pl.pallas_call(
        _gather_modulate_kernel,
        name="modulate",
        grid=(T_ALL // TM_MOE,),
        in_specs=[
            pl.BlockSpec((TM_MOE, D_MODEL), lambda i: (jnp.minimum(i, npb - 1), 0)),
            pl.BlockSpec((TM_MOE, D_MODEL), lambda i: (jnp.maximum(i - npb, 0), 0)),
            pl.BlockSpec((None, 6, None, 1, D_MODEL),
                         lambda i: (0, 0, _mod_row(i, bps, npb), 0, 0)),
        ],
        out_specs=[tok_blk, tok_blk],
        out_shape=[jax.ShapeDtypeStruct((T_ALL, D_MODEL), F32),
                   jax.ShapeDtypeStruct((T_ALL, D_MODEL), BF16)],
        compiler_params=_cparams(("arbitrary",)),
    )(x_prompt.reshape(T_PROMPT, D_MODEL), x_sample.reshape(T_SAMPLE, D_MODEL), mods)


def _inproj_kernel(h_ref, w_ref, z_ref):
    rows = pl.ds(pl.multiple_of(pl.program_id(1) * TM_IN, TM_IN), TM_IN)
    z = jnp.dot(h_ref[rows, :], w_ref[...].astype(BF16), preferred_element_type=F32)
    z_ref[...] = z.astype(BF16)


def _inproj(l, h, w_in):
    return pl.pallas_call(
        _inproj_kernel,
        name="inproj",
        grid=(N_CB_MAIN, T_ALL // TM_IN),
        in_specs=[
            pl.BlockSpec((T_ALL, D_MODEL), lambda j, i: (0, 0)),
            pl.BlockSpec((None, D_MODEL, COL_BLK),
                         lambda j, i: (l, 0, jnp.where(j < N_CB_PRE_Q, j,
                                                       jnp.where(j < CB_Q, j + 1 + N_CB_KV, W_CB_Q)))),
        ],
        out_specs=pl.BlockSpec((TM_IN, COL_BLK), lambda j, i: (i, j)),
        out_shape=jax.ShapeDtypeStruct((T_ALL, N_CB_MAIN * COL_BLK), BF16),
        compiler_params=_cparams(("arbitrary", "arbitrary")),
    )(h, w_in)


def _branch_kernel(abp_ref, abc_ref, abn_ref, guv_ref,
                   dw_ref, cb_ref, clg_ref, clb_ref, glg_ref, glb_ref, ws_ref, bst_ref,
                   yc_ref, ug_ref, ypad_ref, ysh_ref):
    i = pl.program_id(0)
    j = i - N_TB_PROMPT
    in_sample = i >= N_TB_PROMPT
    has_prev = jnp.logical_and(in_sample, j % TB_PER_SAMPLE != 0)
    has_next = jnp.logical_and(in_sample, j % TB_PER_SAMPLE != TB_PER_SAMPLE - 1)

    def glu(ab_ref):
        return (ab_ref[:, :D_CONV].astype(F32) * _sigmoid(ab_ref[:, D_CONV:].astype(F32)))

    ypad_ref[0:HALO, :] = jnp.where(has_prev, glu(abp_ref), 0.0)
    ypad_ref[HALO:HALO + TB, :] = glu(abc_ref)
    ypad_ref[HALO + TB:HALO + TB + HALO, :] = jnp.where(has_next, glu(abn_ref), 0.0)

    for b in range(SUBLANES):
        ysh_ref[b] = ypad_ref[b:b + SHIFT_ROWS, :]

    off = HALO - CONV_HALF
    for c in range(TB // CONV_ROWS):
        base = c * CONV_ROWS
        acc = jnp.zeros((CONV_ROWS, D_CONV), F32)
        for k in range(CONV_WIDTH):
            tile, phase = divmod(off + k, SUBLANES)
            start = base + tile * SUBLANES
            acc = acc + ysh_ref[phase, start:start + CONV_ROWS, :] * dw_ref[k:k + 1, :]
        y = _ln(acc + cb_ref[...], clg_ref[...], clb_ref[...])
        yc_ref[base:base + CONV_ROWS, :] = (y * _sigmoid(y)).astype(BF16)

    for n in range(TB // GM_CHUNK):
        rows = slice(n * GM_CHUNK, (n + 1) * GM_CHUNK)
        u = _gelu(guv_ref[rows, :D_GM].astype(F32))
        v = _ln(_gelu(guv_ref[rows, D_GM:].astype(F32)), glg_ref[...], glb_ref[...]).astype(BF16)
        for g in range(GM_GROUPS):
            cols = slice(g * GM_CHUNK, (g + 1) * GM_CHUNK)
            sv = jnp.dot(ws_ref[g], v[:, cols], preferred_element_type=F32) + bst_ref[:, g:g + 1]
            ug_ref[rows, cols] = (u[:, cols] * sv).astype(BF16)


def _branches(l, z, conv_dw, conv_b, conv_ln_g, conv_ln_b, gm_ln_g, gm_ln_b, gm_ws, gm_bs_t):
    halo_per_tb = TB // HALO
    n_halo = T_ALL // HALO

    def cur(cb):
        return pl.BlockSpec((TB, COL_BLK2), lambda i: (i, cb))

    def prev(cb):
        return pl.BlockSpec((HALO, COL_BLK2), lambda i: (jnp.maximum(i * halo_per_tb - 1, 0), cb))

    def nxt(cb):
        return pl.BlockSpec((HALO, COL_BLK2),
                            lambda i: (jnp.minimum((i + 1) * halo_per_tb, n_halo - 1), cb))

    def vec(n):
        return pl.BlockSpec((None, 1, n), lambda i: (l, 0, 0))

    return pl.pallas_call(
        _branch_kernel,
        name="branches",
        grid=(N_TB,),
        in_specs=[
            prev(CB2_AB), cur(CB2_AB), nxt(CB2_AB), cur(CB2_GUV),
            pl.BlockSpec((None, CONV_WIDTH, D_CONV), lambda i: (l, 0, 0)),
            vec(D_CONV), vec(D_CONV), vec(D_CONV), vec(D_GM), vec(D_GM),
            pl.BlockSpec((None, GM_GROUPS, GM_CHUNK, GM_CHUNK), lambda i: (l, 0, 0, 0)),
            pl.BlockSpec((None, GM_CHUNK, GM_GROUPS), lambda i: (l, 0, 0)),
        ],
        out_specs=[pl.BlockSpec((TB, D_CONV), lambda i: (i, 0)),
                   pl.BlockSpec((TB, D_GM), lambda i: (i, 0))],
        out_shape=[jax.ShapeDtypeStruct((T_ALL, D_CONV), BF16),
                   jax.ShapeDtypeStruct((T_ALL, D_GM), BF16)],
        scratch_shapes=[pltpu.VMEM((TB + 2 * HALO, D_CONV), F32),
                        pltpu.VMEM((SUBLANES, SHIFT_ROWS, D_CONV), F32)],
        compiler_params=_cparams(("arbitrary",)),
    )(z, z, z, z, conv_dw, conv_b, conv_ln_g, conv_ln_b, gm_ln_g, gm_ln_b, gm_ws, gm_bs_t)


def _ctx_attn_kernel(first, q_ref, h_ref, wk_ref, wv_ref, *rest):
    o_ref, ko_ref, vo_ref, wkt_ref, wvt_ref = rest[-5:]

    @pl.when(pl.program_id(0) == 0)
    def _():
        wkt_ref[...] = wk_ref[...].T.astype(BF16)
        wvt_ref[...] = wv_ref[...].T.astype(BF16)

    h = h_ref[...]
    kt = lax.dot_general(wkt_ref[...], h, _NT, preferred_element_type=F32)
    vt = lax.dot_general(wvt_ref[...], h, _NT, preferred_element_type=F32)
    if first:
        ko_ref[0] = kt.reshape(N_HEADS, HEAD_DIM, SEQ)
        vo_ref[0] = vt.reshape(N_HEADS, HEAD_DIM, SEQ)
        ko_ref[1:] = jnp.zeros((DEPTH - 1, N_HEADS, HEAD_DIM, SEQ), F32)
        vo_ref[1:] = jnp.zeros((DEPTH - 1, N_HEADS, HEAD_DIM, SEQ), F32)
    else:
        ko_ref[...] = kt.reshape(N_HEADS, HEAD_DIM, SEQ)
        vo_ref[...] = vt.reshape(N_HEADS, HEAD_DIM, SEQ)
    q = (q_ref[...].astype(F32) * HEAD_DIM ** -0.5).astype(BF16)
    kb = kt.astype(BF16)
    vb = vt.astype(BF16)
    lower = lax.broadcasted_iota(jnp.int32, (SEQ, HEAD_PAIR), 1) < HEAD_DIM
    upper = jnp.logical_not(lower)
    heads = range(N_HEADS)
    grp = [slice(h // 2 * HEAD_PAIR, (h // 2 + 1) * HEAD_PAIR) for h in heads]
    qh = [jnp.where(lower if h % 2 == 0 else upper, q[:, grp[h]],
                    jnp.zeros((SEQ, HEAD_PAIR), BF16)) for h in heads]
    s = [jnp.dot(qh[h], kb[grp[h], :], preferred_element_type=F32) for h in heads]
    m = [jnp.max(s[h], axis=-1, keepdims=True) for h in heads]
    p = [jnp.exp(s[h] - m[h]) for h in heads]
    den = [jnp.sum(p[h], axis=-1, keepdims=True) for h in heads]
    o = [lax.dot_general(p[h].astype(BF16), vb[grp[h], :], _NT, preferred_element_type=F32) / den[h]
         for h in heads]
    for h in range(0, N_HEADS, 2):
        o_ref[:, grp[h]] = jnp.where(lower, o[h], o[h + 1]).astype(BF16)


def _ctx_attn(l, z, h, w_in, kt_all=None, vt_all=None):
    first = kt_all is None
    if first:
        cache_blk = pl.BlockSpec((None, DEPTH, N_HEADS, HEAD_DIM, SEQ), lambda b: (b, 0, 0, 0, 0))
        carried, carried_specs, aliases = (), [], {}
    else:
        cache_blk = pl.BlockSpec((None, None, N_HEADS, HEAD_DIM, SEQ), lambda b: (b, l, 0, 0, 0))
        carried = (kt_all, vt_all)
        carried_specs = [pl.BlockSpec(memory_space=pl.ANY), pl.BlockSpec(memory_space=pl.ANY)]
        aliases = {4: 1, 5: 2}
    cache_shape = jax.ShapeDtypeStruct((BATCH, DEPTH, N_HEADS, HEAD_DIM, SEQ), F32)
    return pl.pallas_call(
        functools.partial(_ctx_attn_kernel, first),
        name="ctx_attn",
        grid=(BATCH,),
        in_specs=[pl.BlockSpec((SEQ, COL_BLK), lambda b: (b, CB_Q)),
                  pl.BlockSpec((SEQ, D_MODEL), lambda b: (b, 0)),
                  pl.BlockSpec((None, D_MODEL, COL_BLK), lambda b: (l, 0, W_CB_K)),
                  pl.BlockSpec((None, D_MODEL, COL_BLK), lambda b: (l, 0, W_CB_V))] + carried_specs,
        out_specs=[pl.BlockSpec((SEQ, D_NA), lambda b: (b, 0)), cache_blk, cache_blk],
        out_shape=[jax.ShapeDtypeStruct((T_PROMPT, D_NA), BF16), cache_shape, cache_shape],
        scratch_shapes=[pltpu.VMEM((COL_BLK, D_MODEL), BF16), pltpu.VMEM((COL_BLK, D_MODEL), BF16)],
        input_output_aliases=aliases,
        compiler_params=_cparams(("arbitrary",)),
    )(z, h, w_in, w_in, *carried)


def _rope(x, cos, sin_up, sin_dn):
    return (x * cos + pltpu.roll(x, D_NA - HEAD_DIM // 4, 1) * sin_up
            + pltpu.roll(x, HEAD_DIM // 4, 1) * sin_dn)


def _na_attn_kernel(q_ref, h_ref, wk_ref, wv_ref, ck_ref, cv_ref, tz_ref, cos_ref, sup_ref,
                    sdn_ref, o_ref, krot_ref, vb_ref, ckb_ref, cvb_ref):
    step = pl.program_id(1)

    @pl.when(step == 0)
    def _():
        h = h_ref[...]
        k = jnp.dot(h, wk_ref[...].astype(BF16), preferred_element_type=F32)
        v = jnp.dot(h, wv_ref[...].astype(BF16), preferred_element_type=F32)
        krot_ref[...] = _rope(k, cos_ref[...], sup_ref[...], sdn_ref[...]).astype(BF16)
        vb_ref[...] = v.astype(BF16)
        ckb_ref[...] = ck_ref[...].astype(BF16)
        cvb_ref[...] = cv_ref[...].astype(BF16)

    qrows = pl.ds(pl.multiple_of(step * NA_TQ, NA_TQ), NA_TQ)
    q = _rope(q_ref[...].astype(F32), cos_ref[qrows, :], sup_ref[qrows, :], sdn_ref[qrows, :])
    q = (q * HEAD_DIM ** -0.5).astype(BF16)

    lower = lax.broadcasted_iota(jnp.int32, (GRID_W, HEAD_PAIR), 1) < HEAD_DIM
    upper = jnp.logical_not(lower)
    kwin, vwin, dr0 = [], [], []
    for j in range(NA_ROWS):
        r = step * NA_ROWS + j
        row_start = jnp.clip(r - WIN_H // 2, 0, GRID_H - WIN_H)
        krows = pl.ds(pl.multiple_of(row_start * GRID_W, GRID_W), WIN_H * GRID_W)
        kwin.append(krot_ref[krows, :])
        vwin.append(vb_ref[krows, :])
        dr0.append(row_start - r + WIN_H - 1)

    units = [(j, h) for j in range(NA_ROWS) for h in range(N_HEADS)]
    grp = [slice(h // 2 * HEAD_PAIR, (h // 2 + 1) * HEAD_PAIR) for _, h in units]
    qh = [jnp.where(lower if h % 2 == 0 else upper, q[j * GRID_W:(j + 1) * GRID_W, grp[u]],
                    jnp.zeros((GRID_W, HEAD_PAIR), BF16)) for u, (j, h) in enumerate(units)]
    s_loc = [lax.dot_general(qh[u], kwin[j][:, grp[u]], _NT, preferred_element_type=F32)
             + jnp.concatenate([tz_ref[h, dr0[j] + w] for w in range(0, WIN_H, 2)], axis=1)
             for u, (j, h) in enumerate(units)]
    s_ctx = [jnp.dot(qh[u], ckb_ref[grp[u], :], preferred_element_type=F32)
             for u in range(len(units))]
    m = [jnp.maximum(jnp.max(s_loc[u], axis=-1, keepdims=True),
                     jnp.max(s_ctx[u], axis=-1, keepdims=True)) for u in range(len(units))]
    p_loc = [jnp.exp(s_loc[u] - m[u]) for u in range(len(units))]
    p_ctx = [jnp.exp(s_ctx[u] - m[u]) for u in range(len(units))]
    den = [jnp.sum(p_loc[u], axis=-1, keepdims=True) + jnp.sum(p_ctx[u], axis=-1, keepdims=True)
           for u in range(len(units))]
    o = [(jnp.dot(p_loc[u].astype(BF16), vwin[j][:, grp[u]], preferred_element_type=F32)
          + lax.dot_general(p_ctx[u].astype(BF16), cvb_ref[grp[u], :], _NT,
                            preferred_element_type=F32)) / den[u]
         for u, (j, h) in enumerate(units)]
    for u, (j, h) in enumerate(units):
        if h % 2 == 0:
            o_ref[j * GRID_W:(j + 1) * GRID_W, grp[u]] = jnp.where(lower, o[u], o[u + 1]).astype(BF16)


def _na_attn(l, z, h, w_in, cache_k, cache_v, tz, cos, sin_up, sin_dn):
    seq_blk0 = T_PROMPT // DEC_SEQ
    row_blk0 = T_PROMPT // NA_TQ
    steps = GRID_H // NA_ROWS
    full = pl.BlockSpec((DEC_SEQ, D_NA), lambda b, r: (0, 0))
    return pl.pallas_call(
        _na_attn_kernel,
        name="na_attn",
        grid=(DEC_BATCH, steps),
        in_specs=[
            pl.BlockSpec((NA_TQ, COL_BLK), lambda b, r: (row_blk0 + b * steps + r, CB_Q)),
            pl.BlockSpec((DEC_SEQ, D_MODEL), lambda b, r: (seq_blk0 + b, 0)),
            pl.BlockSpec((None, D_MODEL, COL_BLK), lambda b, r: (l, 0, W_CB_K)),
            pl.BlockSpec((None, D_MODEL, COL_BLK), lambda b, r: (l, 0, W_CB_V)),
            pl.BlockSpec((None, None, D_NA, PAST_LEN), lambda b, r: (b, l, 0, 0)),
            pl.BlockSpec((None, None, D_NA, PAST_LEN), lambda b, r: (b, l, 0, 0)),
            pl.BlockSpec((None, N_HEADS, 2 * WIN_H - 2, GRID_W, 2 * GRID_W),
                         lambda b, r: (l, 0, 0, 0, 0)),
            full, full, full,
        ],
        out_specs=pl.BlockSpec((NA_TQ, D_NA), lambda b, r: (b * steps + r, 0)),
        out_shape=jax.ShapeDtypeStruct((T_SAMPLE, D_NA), BF16),
        scratch_shapes=[pltpu.VMEM((DEC_SEQ, D_NA), BF16), pltpu.VMEM((DEC_SEQ, D_NA), BF16),
                        pltpu.VMEM((D_NA, PAST_LEN), BF16), pltpu.VMEM((D_NA, PAST_LEN), BF16)],
        compiler_params=_cparams(("arbitrary", "arbitrary")),
    )(z, h, w_in, w_in, cache_k, cache_v, tz, cos, sin_up, sin_dn)


def _merge_kernel(x_ref, mod_ref, gc_ref, ga_ref, gg_ref, yc_ref, ug_ref, ap_ref, as_ref,
                  pw_ref, no_ref, go_ref, wo_ref, lg_ref, lb_ref, o_ref,
                  pwb_ref, nob_ref, gob_ref, wob_ref):
    i = pl.program_id(0)

    @pl.when(i == 0)
    def _():
        pwb_ref[...] = pw_ref[...].astype(BF16)
        nob_ref[...] = no_ref[...].astype(BF16)
        gob_ref[...] = go_ref[...].astype(BF16)
        wob_ref[...] = wo_ref[...].astype(BF16)

    att = jnp.where(i < N_TBM_PROMPT, ap_ref[...], as_ref[...])
    br_c = jnp.dot(yc_ref[...], pwb_ref[...], preferred_element_type=F32)
    br_a = jnp.dot(att, nob_ref[...], preferred_element_type=F32)
    br_g = jnp.dot(ug_ref[...], gob_ref[...], preferred_element_type=F32)
    merged = (_sigmoid(gc_ref[...].astype(F32)) * br_c + _sigmoid(ga_ref[...].astype(F32)) * br_a
              + _sigmoid(gg_ref[...].astype(F32)) * br_g)
    mix = jnp.dot(merged.astype(BF16), wob_ref[...], preferred_element_type=F32)
    o_ref[...] = _ln(ALPHA * x_ref[...] + mod_ref[2] * mix, lg_ref[...], lb_ref[...])


def _merge(l, x, mods, z, yc, ug, att_p, att_s, conv_pw, na_out, gm_out, w_o, ln_g, ln_b):
    def gz(k):
        return pl.BlockSpec((TBM, COL_BLK2), lambda i: (i, CB2_GZ + k))

    def w(k, n):
        return pl.BlockSpec((None, k, n), lambda i: (l, 0, 0))

    blk512 = pl.BlockSpec((TBM, COL_BLK), lambda i: (i, 0))
    return pl.pallas_call(
        _merge_kernel,
        name="merge",
        grid=(N_TBM,),
        in_specs=[
            pl.BlockSpec((TBM, D_MODEL), lambda i: (i, 0)),
            pl.BlockSpec((None, 6, None, 1, D_MODEL),
                         lambda i: (l, 0, _mod_row(i, TBM_PER_SAMPLE, N_TBM_PROMPT), 0, 0)),
            gz(0), gz(1), gz(2),
            blk512, blk512,
            pl.BlockSpec((TBM, D_NA), lambda i: (jnp.minimum(i, N_TBM_PROMPT - 1), 0)),
            pl.BlockSpec((TBM, D_NA), lambda i: (jnp.maximum(i - N_TBM_PROMPT, 0), 0)),
            w(D_CONV, D_MODEL), w(D_NA, D_MODEL), w(D_GM, D_MODEL), w(D_MODEL, D_MODEL),
            w(1, D_MODEL), w(1, D_MODEL),
        ],
        out_specs=pl.BlockSpec((TBM, D_MODEL), lambda i: (i, 0)),
        out_shape=jax.ShapeDtypeStruct((T_ALL, D_MODEL), F32),
        scratch_shapes=[pltpu.VMEM((D_CONV, D_MODEL), BF16), pltpu.VMEM((D_NA, D_MODEL), BF16),
                        pltpu.VMEM((D_GM, D_MODEL), BF16), pltpu.VMEM((D_MODEL, D_MODEL), BF16)],
        compiler_params=_cparams(("arbitrary",)),
    )(x, mods, z, z, z, yc, ug, att_p, att_s, conv_pw, na_out, gm_out, w_o, ln_g, ln_b)


def _route(logits):
    lane = lax.broadcasted_iota(jnp.int32, logits.shape, 1)
    big = jnp.int32(ROUTER_LANES)
    is_g = lane < N_EGROUPS
    gl = jnp.where(is_g, logits, -jnp.inf)
    gmax = jnp.max(gl, axis=-1, keepdims=True)
    gidx = jnp.min(jnp.where(gl == gmax, lane, big), axis=-1, keepdims=True)
    gp = 1.0 / jnp.sum(jnp.where(is_g, jnp.exp(gl - gmax), 0.0), axis=-1, keepdims=True)
    lo = N_EGROUPS + gidx * EXP_PER_GROUP
    el = jnp.where(jnp.logical_and(lane >= lo, lane < lo + EXP_PER_GROUP), logits, -jnp.inf)
    v1 = jnp.max(el, axis=-1, keepdims=True)
    i1 = jnp.min(jnp.where(el == v1, lane, big), axis=-1, keepdims=True)
    el2 = jnp.where(lane == i1, -jnp.inf, el)
    v2 = jnp.max(el2, axis=-1, keepdims=True)
    i2 = jnp.min(jnp.where(el2 == v2, lane, big), axis=-1, keepdims=True)
    e2 = jnp.exp(v2 - v1)
    w1 = gp / (1.0 + e2)
    w2 = gp * e2 / (1.0 + e2)
    return jnp.where(lane == i1, w1, 0.0) + jnp.where(lane == i2, w2, 0.0)


def _split_bf16(a):
    hi = a.astype(BF16)
    return hi, (a - hi.astype(F32)).astype(BF16)


def _moe_kernel(last, x_ref, mod_ref, modn_ref, rw_ref, rb_ref, w1_ref, w3_ref, w2_ref, lg_ref,
                lb_ref, out0_ref, out1_ref, t_ref, gate_ref, acc_ref):
    i = pl.program_id(0)
    e = pl.program_id(1)

    @pl.when(e == 0)
    def _():
        t = x_ref[...] * (1.0 + mod_ref[4]) + mod_ref[3]
        t_hi, t_lo = _split_bf16(t)
        t_ref[...] = t_hi
        both = jnp.dot(t_hi, rw_ref[...], preferred_element_type=F32)
        logits = (both[:, :ROUTER_LANES] + both[:, ROUTER_LANES:]
                  + jnp.dot(t_lo, rw_ref[:, :ROUTER_LANES], preferred_element_type=F32))
        gate_ref[...] = _route(logits + rb_ref[...])
        acc_ref[...] = jnp.zeros_like(acc_ref)

    t = t_ref[...]
    gate = gate_ref[...]
    lane = lax.broadcasted_iota(jnp.int32, gate.shape, 1)
    hcol = lax.broadcasted_iota(jnp.int32, (t.shape[0], EXP_SUB * D_EXPERT), 1) // D_EXPERT
    for k0 in range(0, EXP_STEP, EXP_SUB):
        ks = range(k0, k0 + EXP_SUB)
        w1 = jnp.concatenate([w1_ref[k] for k in ks], axis=1).astype(BF16)
        w3 = jnp.concatenate([w3_ref[k] for k in ks], axis=1).astype(BF16)
        w2 = w2_ref[k0:k0 + EXP_SUB].reshape(EXP_SUB * D_EXPERT, D_MODEL).astype(BF16)
        h1 = jnp.dot(t, w1, preferred_element_type=F32)
        h3 = jnp.dot(t, w3, preferred_element_type=F32)
        gmul = jnp.zeros(h1.shape, F32)
        for k in ks:
            gcol = jnp.sum(jnp.where(lane == e * EXP_STEP + k + N_EGROUPS, gate, 0.0),
                           axis=-1, keepdims=True)
            gmul = jnp.where(hcol == k - k0, gcol, gmul)
        hid = (h1 * _sigmoid(h1) * h3 * gmul).astype(BF16)
        acc_ref[...] += jnp.dot(hid, w2, preferred_element_type=F32)

    @pl.when(e == N_EXPERTS // EXP_STEP - 1)
    def _():
        y = _ln(ALPHA * x_ref[...] + mod_ref[5] * acc_ref[...], lg_ref[...], lb_ref[...])
        if last:
            @pl.when(i < T_PROMPT // TM_MOE)
            def _():
                out0_ref[...] = y

            @pl.when(i >= T_PROMPT // TM_MOE)
            def _():
                out1_ref[...] = y
        else:
            out0_ref[...] = y
            out1_ref[...] = (y * (1.0 + modn_ref[1]) + modn_ref[0]).astype(BF16)


def _moe(l, x, mods, router_w, router_b, w1, w3, w2, ln_g, ln_b):
    n_m = T_ALL // TM_MOE
    bps = DEC_SEQ // TM_MOE
    npb = T_PROMPT // TM_MOE
    last = l + 1 == DEPTH
    l_next = l if last else l + 1
    tok_blk = pl.BlockSpec((TM_MOE, D_MODEL), lambda i, e: (i, 0))
    if last:
        out_specs = [pl.BlockSpec((TM_MOE, D_MODEL), lambda i, e: (jnp.minimum(i, npb - 1), 0)),
                     pl.BlockSpec((TM_MOE, D_MODEL), lambda i, e: (jnp.maximum(i - npb, 0), 0))]
        out_shape = [jax.ShapeDtypeStruct((T_PROMPT, D_MODEL), F32),
                     jax.ShapeDtypeStruct((T_SAMPLE, D_MODEL), F32)]
    else:
        out_specs = [tok_blk, tok_blk]
        out_shape = [jax.ShapeDtypeStruct((T_ALL, D_MODEL), F32),
                     jax.ShapeDtypeStruct((T_ALL, D_MODEL), BF16)]
    return pl.pallas_call(
        functools.partial(_moe_kernel, last),
        name="moe",
        grid=(n_m, N_EXPERTS // EXP_STEP),
        in_specs=[
            tok_blk,
            pl.BlockSpec((None, 6, None, 1, D_MODEL),
                         lambda i, e: (l, 0, _mod_row(i, bps, npb), 0, 0)),
            pl.BlockSpec((None, 6, None, 1, D_MODEL),
                         lambda i, e: (l_next, 0, _mod_row(i, bps, npb), 0, 0)),
            pl.BlockSpec((None, D_MODEL, 2 * ROUTER_LANES), lambda i, e: (l, 0, 0)),
            pl.BlockSpec((None, 1, ROUTER_LANES), lambda i, e: (l, 0, 0)),
            pl.BlockSpec((None, EXP_STEP, D_MODEL, D_EXPERT), lambda i, e: (l, e, 0, 0)),
            pl.BlockSpec((None, EXP_STEP, D_MODEL, D_EXPERT), lambda i, e: (l, e, 0, 0)),
            pl.BlockSpec((None, EXP_STEP, D_EXPERT, D_MODEL), lambda i, e: (l, e, 0, 0)),
            pl.BlockSpec((None, 1, D_MODEL), lambda i, e: (l, 0, 0)),
            pl.BlockSpec((None, 1, D_MODEL), lambda i, e: (l, 0, 0)),
        ],
        out_specs=out_specs,
        out_shape=out_shape,
        scratch_shapes=[pltpu.VMEM((TM_MOE, D_MODEL), BF16),
                        pltpu.VMEM((TM_MOE, ROUTER_LANES), F32),
                        pltpu.VMEM((TM_MOE, D_MODEL), F32)],
        compiler_params=_cparams(("arbitrary", "arbitrary")),
    )(x, mods, mods, router_w, router_b, w1, w3, w2, ln_g, ln_b)


def _rope_tables():
    t = np.arange(DEC_SEQ)
    pos = np.stack([t // GRID_W, t % GRID_W], axis=1).astype(np.float32)
    quarter = HEAD_DIM // 4
    d = np.arange(HEAD_DIM)
    axis = d // (HEAD_DIM // 2)
    freq = d % quarter
    upper = (d % (HEAD_DIM // 2)) >= quarter
    inv = jnp.asarray(ROPE_BASE, F32) ** (-jnp.arange(0, HEAD_DIM // 2, 2, dtype=F32) / (HEAD_DIM // 2))
    ang = jnp.asarray(pos)[:, axis] * inv[freq][None, :]
    cos = jnp.cos(ang)
    sin = jnp.sin(ang)
    sin_up = jnp.where(upper[None, :], 0.0, -sin)
    sin_dn = jnp.where(upper[None, :], sin, 0.0)
    tile = lambda a: jnp.tile(a, (1, N_HEADS))
    return tile(cos), tile(sin_up), tile(sin_dn)


def _bias_tables(na_rpb):
    qc = np.arange(GRID_W)[:, None]
    kc = np.arange(GRID_W)[None, :]
    start = np.clip(qc - WIN_W // 2, 0, GRID_W - WIN_W)
    valid = (kc >= start) & (kc < start + WIN_W)
    n_dc = 2 * WIN_W - 1
    dc = kc - qc + WIN_W - 1
    hit = ((dc[None] == np.arange(n_dc)[:, None, None]) & valid[None]).astype(np.float32)
    sel = np.zeros((2 * n_dc + 1, GRID_W, 2 * GRID_W), np.float32)
    sel[:n_dc, :, :GRID_W] = hit
    sel[n_dc:2 * n_dc, :, GRID_W:] = hit
    sel[2 * n_dc] = np.where(np.concatenate([valid, valid], axis=1), 0.0, NEG_INF)
    ones = jnp.ones(na_rpb.shape[:2] + (2 * WIN_H - 2, 1), F32)
    rows = jnp.concatenate([na_rpb[:, :, :-1], na_rpb[:, :, 1:], ones], axis=-1)
    return jnp.einsum('lhrd,dqk->lhrqk', rows, jnp.asarray(sel), precision=HIGHEST)


def kernel(x_prompt, x_sample, cache_na_k, cache_na_v, c, c_ctx, w_ada, b_ada, w_in, conv_dw,
           conv_b, conv_ln_g, conv_ln_b, conv_pw, na_rpb, na_out, gm_ln_g, gm_ln_b, gm_ws, gm_bs,
           gm_out, w_o, ln1_g, ln1_b, rg_w, rg_b, re_w, re_b, moe_w1, moe_w3, moe_w2, ln2_g, ln2_b):
    cond =jnp.zeros((N_COND, D_MODEL), F32).at[0].set(c_ctx).at[1:1 + DEC_BATCH].set(c)
    mods = _ada(cond, w_ada, b_ada)
    mods = mods.reshape(DEPTH, N_COND, 6, 1, D_MODEL).transpose(0, 2, 1, 3, 4)

    gm_ws_b = gm_ws.astype(BF16)
    vec = lambda a: a.reshape(DEPTH, 1, a.shape[-1])
    gm_bs_t = gm_bs.transpose(0, 2, 1)
    router_w = jnp.concatenate(
        [rg_w, re_w.transpose(0, 2, 1, 3).reshape(DEPTH, D_MODEL, N_EXPERTS)], axis=-1)
    router_w = jnp.pad(router_w, ((0, 0), (0, 0), (0, ROUTER_LANES - N_EGROUPS - N_EXPERTS)))
    router_hi = router_w.astype(BF16)
    router_lo = (router_w - router_hi.astype(F32)).astype(BF16)
    router_w = jnp.concatenate([router_hi, router_lo], axis=-1)
    router_b = jnp.concatenate([rg_b, re_b.reshape(DEPTH, N_EXPERTS)], axis=-1)
    router_b = jnp.pad(router_b, ((0, 0), (0, ROUTER_LANES - N_EGROUPS - N_EXPERTS)))
    router_b = router_b.reshape(DEPTH, 1, ROUTER_LANES)
    cache_k = cache_na_k.transpose(0, 1, 3, 4, 2).reshape(DEC_BATCH, DEPTH, D_NA, PAST_LEN)
    cache_v = cache_na_v.transpose(0, 1, 3, 4, 2).reshape(DEC_BATCH, DEPTH, D_NA, PAST_LEN)
    tz = _bias_tables(na_rpb)
    cos, sin_up, sin_dn = _rope_tables()

    kt_all = vt_all = None
    x, h = _gather_modulate(x_prompt, x_sample, mods)
    for l in range(DEPTH):
        z = _inproj(l, h, w_in)
        yc, ug = _branches(l, z, conv_dw, vec(conv_b), vec(conv_ln_g), vec(conv_ln_b),
                           vec(gm_ln_g), vec(gm_ln_b), gm_ws_b, gm_bs_t)
        att_p, kt_all, vt_all = _ctx_attn(l, z, h, w_in, kt_all, vt_all)
        att_s = _na_attn(l, z, h, w_in, cache_k, cache_v, tz, cos, sin_up, sin_dn)
        x = _merge(l, x, mods, z, yc, ug, att_p, att_s, conv_pw, na_out, gm_out, w_o,
                   vec(ln1_g), vec(ln1_b))
        outs = _moe(l, x, mods, router_w, router_b, moe_w1, moe_w3, moe_w2, vec(ln2_g), vec(ln2_b))
        x, h = outs

    y_prompt, y_sample = outs
    return (y_prompt.reshape(BATCH, SEQ, D_MODEL), y_sample.reshape(DEC_BATCH, DEC_SEQ, D_MODEL),
            kt_all.transpose(0, 1, 4, 2, 3), vt_all.transpose(0, 1, 4, 2, 3))
```

```python
import functools

import jax
import jax.numpy as jnp
import numpy as np
from jax import lax
from jax.experimental import pallas as pl
from jax.experimental.pallas import tpu as pltpu

F32 = jnp.float32
BF16 = jnp.bfloat16
HIGHEST = lax.Precision.HIGHEST

D_MODEL = 1024
BATCH = 16
SEQ = 256
DEPTH = 4
DEC_BATCH = 2
DEC_SEQ = 1024
PAST_LEN = 256
GRID_W = 64
GRID_H = DEC_SEQ // GRID_W
D_CONV = 512
CONV_WIDTH = 31
CONV_HALF = CONV_WIDTH // 2
HEAD_DIM = 64
HEAD_PAIR = 2 * HEAD_DIM
N_HEADS = 8
D_NA = 512
WIN_H = 8
WIN_W = 16
ROPE_BASE = 10000.0
D_GM = 512
GM_CHUNK = 128
GM_GROUPS = 4
D_IN = 6656
N_EGROUPS = 4
EXP_PER_GROUP = 8
N_EXPERTS = 32
D_EXPERT = 128
ALPHA = (2 * DEPTH) ** 0.25
LN_EPS = 1e-5
NEG_INF = -1e30

T_PROMPT = BATCH * SEQ
T_SAMPLE = DEC_BATCH * DEC_SEQ
T_ALL = T_PROMPT + T_SAMPLE
N_COND = 8

COL_BLK = 512
W_CB_Q, W_CB_K, W_CB_V = 2, 3, 4
N_CB_KV = 2
N_CB_MAIN = D_IN // COL_BLK - N_CB_KV
N_CB_PRE_Q = 2
CB_Q = N_CB_MAIN - 1
CB2_AB, CB2_GUV, CB2_GZ = 0, 1, 2
COL_BLK2 = 2 * COL_BLK

TB = 256
N_TB = T_ALL // TB
N_TB_PROMPT = T_PROMPT // TB
TB_PER_SAMPLE = DEC_SEQ // TB
TBM = 512
N_TBM = T_ALL // TBM
N_TBM_PROMPT = T_PROMPT // TBM
TBM_PER_SAMPLE = DEC_SEQ // TBM
HALO = 16
CONV_ROWS = 256
SUBLANES = 8
SHIFT_ROWS = TB + 2 * HALO - SUBLANES

NA_ROWS = 4
NA_TQ = NA_ROWS * GRID_W
TM_IN = 3072
TM_MOE = 1024
EXP_STEP = 4
EXP_SUB = 4
ROUTER_LANES = 128
VMEM_LIMIT = 56 * 1024 * 1024


def _ln(x, g, b):
    mu = jnp.mean(x, axis=-1, keepdims=True)
    xc = x - mu
    var = jnp.mean(xc * xc, axis=-1, keepdims=True)
    return xc * lax.rsqrt(var + LN_EPS) * g + b


def _sigmoid(x):
    return jax.nn.sigmoid(x)


def _gelu(x):
    return jax.nn.gelu(x, approximate=True)


_NT = (((1,), (1,)), ((), ()))


def _cparams(sem):
    return pltpu.CompilerParams(dimension_semantics=sem, vmem_limit_bytes=VMEM_LIMIT)


def _ada_kernel(c_ref, w_ref, b_ref, o_ref):
    c = c_ref[...]
    s = c * _sigmoid(c)
    w = w_ref[...]
    s_hi, w_hi = s.astype(BF16), w.astype(BF16)
    s_lo = (s - s_hi.astype(F32)).astype(BF16)
    w_lo = (w - w_hi.astype(F32)).astype(BF16)
    o_ref[...] = (jnp.dot(s_hi, w_hi, preferred_element_type=F32)
                  + jnp.dot(s_lo, w_hi, preferred_element_type=F32)
                  + jnp.dot(s_hi, w_lo, preferred_element_type=F32)) + b_ref[...]


def _ada(cond, w_ada, b_ada):
    tn = 3072
    return pl.pallas_call(
        _ada_kernel,
        name="ada",
        grid=(DEPTH, 6 * D_MODEL // tn),
        in_specs=[
            pl.BlockSpec((N_COND, D_MODEL), lambda l, j: (0, 0)),
            pl.BlockSpec((None, D_MODEL, tn), lambda l, j: (l, 0, j)),
            pl.BlockSpec((None, 1, tn), lambda l, j: (l, 0, j)),
        ],
        out_specs=pl.BlockSpec((None, N_COND, tn), lambda l, j: (l, 0, j)),
        out_shape=jax.ShapeDtypeStruct((DEPTH, N_COND, 6 * D_MODEL), F32),
        compiler_params=_cparams(("arbitrary", "arbitrary")),
    )(cond, w_ada, b_ada.reshape(DEPTH, 1, 6 * D_MODEL))


def _mod_row(i, blocks_per_sample, n_prompt_blocks):
    return jnp.where(i < n_prompt_blocks, 0, 1 + (i - n_prompt_blocks) // blocks_per_sample)


def _gather_modulate_kernel(xp_ref, xs_ref, mod_ref, x_ref, h_ref):
    x = jnp.where(pl.program_id(0) < T_PROMPT // TM_MOE, xp_ref[...], xs_ref[...])
    x_ref[...] = x
    h_ref[...] = (x * (1.0 + mod_ref[1]) + mod_ref[0]).astype(BF16)


def _gather_modulate(x_prompt, x_sample, mods):
    bps = DEC_SEQ // TM_MOE
    npb = T_PROMPT // TM_MOE
    tok_blk = pl.BlockSpec((TM_MOE, D_MODEL), lambda i: (i, 0))
    return pl.pallas_call(
        _gather_modulate_kernel,
        name="modulate",
        grid=(T_ALL // TM_MOE,),
        in_specs=[
            pl.BlockSpec((TM_MOE, D_MODEL), lambda i: (jnp.minimum(i, npb - 1), 0)),
            pl.BlockSpec((TM_MOE, D_MODEL), lambda i: (jnp.maximum(i - npb, 0), 0)),
            pl.BlockSpec((None, 6, None, 1, D_MODEL),
                         lambda i: (0, 0, _mod_row(i, bps, npb), 0, 0)),
        ],
        out_specs=[tok_blk, tok_blk],
        out_shape=[jax.ShapeDtypeStruct((T_ALL, D_MODEL), F32),
                   jax.ShapeDtypeStruct((T_ALL, D_MODEL), BF16)],
        compiler_params=_cparams(("arbitrary",)),
    )(x_prompt.reshape(T_PROMPT, D_MODEL), x_sample.reshape(T_SAMPLE, D_MODEL), mods)


def _inproj_kernel(h_ref, w_ref, z_ref):
    rows = pl.ds(pl.multiple_of(pl.program_id(1) * TM_IN, TM_IN), TM_IN)
    z = jnp.dot(h_ref[rows, :], w_ref[...].astype(BF16), preferred_element_type=F32)
    z_ref[...] = z.astype(BF16)


def _inproj(l, h, w_in):
    return pl.pallas_call(
        _inproj_kernel,
        name="inproj",
        grid=(N_CB_MAIN, T_ALL // TM_IN),
        in_specs=[
            pl.BlockSpec((T_ALL, D_MODEL), lambda j, i: (0, 0)),
            pl.BlockSpec((None, D_MODEL, COL_BLK),
                         lambda j, i: (l, 0, jnp.where(j < N_CB_PRE_Q, j,
                                                       jnp.where(j < CB_Q, j + 1 + N_CB_KV, W_CB_Q)))),
        ],
        out_specs=pl.BlockSpec((TM_IN, COL_BLK), lambda j, i: (i, j)),
        out_shape=jax.ShapeDtypeStruct((T_ALL, N_CB_MAIN * COL_BLK), BF16),
        compiler_params=_cparams(("arbitrary", "arbitrary")),
    )(h, w_in)


def _branch_kernel(abp_ref, abc_ref, abn_ref, guv_ref,
                   dw_ref, cb_ref, clg_ref, clb_ref, glg_ref, glb_ref, ws_ref, bst_ref,
                   yc_ref, ug_ref, ypad_ref, ysh_ref):
    i = pl.program_id(0)
    j = i - N_TB_PROMPT
    in_sample = i >= N_TB_PROMPT
    has_prev = jnp.logical_and(in_sample, j % TB_PER_SAMPLE != 0)
    has_next = jnp.logical_and(in_sample, j % TB_PER_SAMPLE != TB_PER_SAMPLE - 1)

    def glu(ab_ref):
        return (ab_ref[:, :D_CONV].astype(F32) * _sigmoid(ab_ref[:, D_CONV:].astype(F32)))

    ypad_ref[0:HALO, :] = jnp.where(has_prev, glu(abp_ref), 0.0)
    ypad_ref[HALO:HALO + TB, :] = glu(abc_ref)
    ypad_ref[HALO + TB:HALO + TB + HALO, :] = jnp.where(has_next, glu(abn_ref), 0.0)

    for b in range(SUBLANES):
        ysh_ref[b] = ypad_ref[b:b + SHIFT_ROWS, :]

    off = HALO - CONV_HALF
    for c in range(TB // CONV_ROWS):
        base = c * CONV_ROWS
        acc = jnp.zeros((CONV_ROWS, D_CONV), F32)
        for k in range(CONV_WIDTH):
            tile, phase = divmod(off + k, SUBLANES)
            start = base + tile * SUBLANES
            acc = acc + ysh_ref[phase, start:start + CONV_ROWS, :] * dw_ref[k:k + 1, :]
        y = _ln(acc + cb_ref[...], clg_ref[...], clb_ref[...])
        yc_ref[base:base + CONV_ROWS, :] = (y * _sigmoid(y)).astype(BF16)

    for n in range(TB // GM_CHUNK):
        rows = slice(n * GM_CHUNK, (n + 1) * GM_CHUNK)
        u = _gelu(guv_ref[rows, :D_GM].astype(F32))
        v = _ln(_gelu(guv_ref[rows, D_GM:].astype(F32)), glg_ref[...], glb_ref[...]).astype(BF16)
        for g in range(GM_GROUPS):
            cols = slice(g * GM_CHUNK, (g + 1) * GM_CHUNK)
            sv = jnp.dot(ws_ref[g], v[:, cols], preferred_element_type=F32) + bst_ref[:, g:g + 1]
            ug_ref[rows, cols] = (u[:, cols] * sv).astype(BF16)


def _branches(l, z, conv_dw, conv_b, conv_ln_g, conv_ln_b, gm_ln_g, gm_ln_b, gm_ws, gm_bs_t):
    halo_per_tb = TB // HALO
    n_halo = T_ALL // HALO

    def cur(cb):
        return pl.BlockSpec((TB, COL_BLK2), lambda i: (i, cb))

    def prev(cb):
        return pl.BlockSpec((HALO, COL_BLK2), lambda i: (jnp.maximum(i * halo_per_tb - 1, 0), cb))

    def nxt(cb):
        return pl.BlockSpec((HALO, COL_BLK2),
                            lambda i: (jnp.minimum((i + 1) * halo_per_tb, n_halo - 1), cb))

    def vec(n):
        return pl.BlockSpec((None, 1, n), lambda i: (l, 0, 0))

    return pl.pallas_call(
        _branch_kernel,
        name="branches",
        grid=(N_TB,),
        in_specs=[
            prev(CB2_AB), cur(CB2_AB), nxt(CB2_AB), cur(CB2_GUV),
            pl.BlockSpec((None, CONV_WIDTH, D_CONV), lambda i: (l, 0, 0)),
            vec(D_CONV), vec(D_CONV), vec(D_CONV), vec(D_GM), vec(D_GM),
            pl.BlockSpec((None, GM_GROUPS, GM_CHUNK, GM_CHUNK), lambda i: (l, 0, 0, 0)),
            pl.BlockSpec((None, GM_CHUNK, GM_GROUPS), lambda i: (l, 0, 0)),
        ],
        out_specs=[pl.BlockSpec((TB, D_CONV), lambda i: (i, 0)),
                   pl.BlockSpec((TB, D_GM), lambda i: (i, 0))],
        out_shape=[jax.ShapeDtypeStruct((T_ALL, D_CONV), BF16),
                   jax.ShapeDtypeStruct((T_ALL, D_GM), BF16)],
        scratch_shapes=[pltpu.VMEM((TB + 2 * HALO, D_CONV), F32),
                        pltpu.VMEM((SUBLANES, SHIFT_ROWS, D_CONV), F32)],
        compiler_params=_cparams(("arbitrary",)),
    )(z, z, z, z, conv_dw, conv_b, conv_ln_g, conv_ln_b, gm_ln_g, gm_ln_b, gm_ws, gm_bs_t)


def _ctx_attn_kernel(first, q_ref, h_ref, wk_ref, wv_ref, *rest):
    o_ref, ko_ref, vo_ref, wkt_ref, wvt_ref = rest[-5:]

    @pl.when(pl.program_id(0) == 0)
    def _():
        wkt_ref[...] = wk_ref[...].T.astype(BF16)
        wvt_ref[...] = wv_ref[...].T.astype(BF16)

    h = h_ref[...]
    kt = lax.dot_general(wkt_ref[...], h, _NT, preferred_element_type=F32)
    vt = lax.dot_general(wvt_ref[...], h, _NT, preferred_element_type=F32)
    if first:
        ko_ref[0] = kt.reshape(N_HEADS, HEAD_DIM, SEQ)
        vo_ref[0] = vt.reshape(N_HEADS, HEAD_DIM, SEQ)
        ko_ref[1:] = jnp.zeros((DEPTH - 1, N_HEADS, HEAD_DIM, SEQ), F32)
        vo_ref[1:] = jnp.zeros((DEPTH - 1, N_HEADS, HEAD_DIM, SEQ), F32)
    else:
        ko_ref[...] = kt.reshape(N_HEADS, HEAD_DIM, SEQ)
        vo_ref[...] = vt.reshape(N_HEADS, HEAD_DIM, SEQ)
    q = (q_ref[...].astype(F32) * HEAD_DIM ** -0.5).astype(BF16)
    kb = kt.astype(BF16)
    vb = vt.astype(BF16)
    lower = lax.broadcasted_iota(jnp.int32, (SEQ, HEAD_PAIR), 1) < HEAD_DIM
    upper = jnp.logical_not(lower)
    heads = range(N_HEADS)
    grp = [slice(h // 2 * HEAD_PAIR, (h // 2 + 1) * HEAD_PAIR) for h in heads]
    qh = [jnp.where(lower if h % 2 == 0 else upper, q[:, grp[h]],
                    jnp.zeros((SEQ, HEAD_PAIR), BF16)) for h in heads]
    s = [jnp.dot(qh[h], kb[grp[h], :], preferred_element_type=F32) for h in heads]
    m = [jnp.max(s[h], axis=-1, keepdims=True) for h in heads]
    p = [jnp.exp(s[h] - m[h]) for h in heads]
    den = [jnp.sum(p[h], axis=-1, keepdims=True) for h in heads]
    o = [lax.dot_general(p[h].astype(BF16), vb[grp[h], :], _NT, preferred_element_type=F32) / den[h]
         for h in heads]
    for h in range(0, N_HEADS, 2):
        o_ref[:, grp[h]] = jnp.where(lower, o[h], o[h + 1]).astype(BF16)


def _ctx_attn(l, z, h, w_in, kt_all=None, vt_all=None):
    first = kt_all is None
    if first:
        cache_blk = pl.BlockSpec((None, DEPTH, N_HEADS, HEAD_DIM, SEQ), lambda b: (b, 0, 0, 0, 0))
        carried, carried_specs, aliases = (), [], {}
    else:
        cache_blk = pl.BlockSpec((None, None, N_HEADS, HEAD_DIM, SEQ), lambda b: (b, l, 0, 0, 0))
        carried = (kt_all, vt_all)
        carried_specs = [pl.BlockSpec(memory_space=pl.ANY), pl.BlockSpec(memory_space=pl.ANY)]
        aliases = {4: 1, 5: 2}
    cache_shape = jax.ShapeDtypeStruct((BATCH, DEPTH, N_HEADS, HEAD_DIM, SEQ), F32)
    return pl.pallas_call(
        functools.partial(_ctx_attn_kernel, first),
        name="ctx_attn",
        grid=(BATCH,),
        in_specs=[pl.BlockSpec((SEQ, COL_BLK), lambda b: (b, CB_Q)),
                  pl.BlockSpec((SEQ, D_MODEL), lambda b: (b, 0)),
                  pl.BlockSpec((None, D_MODEL, COL_BLK), lambda b: (l, 0, W_CB_K)),
                  pl.BlockSpec((None, D_MODEL, COL_BLK), lambda b: (l, 0, W_CB_V))] + carried_specs,
        out_specs=[pl.BlockSpec((SEQ, D_NA), lambda b: (b, 0)), cache_blk, cache_blk],
        out_shape=[jax.ShapeDtypeStruct((T_PROMPT, D_NA), BF16), cache_shape, cache_shape],
        scratch_shapes=[pltpu.VMEM((COL_BLK, D_MODEL), BF16), pltpu.VMEM((COL_BLK, D_MODEL), BF16)],
        input_output_aliases=aliases,
        compiler_params=_cparams(("arbitrary",)),
    )(z, h, w_in, w_in, *carried)


def _rope(x, cos, sin_up, sin_dn):
    return (x * cos + pltpu.roll(x, D_NA - HEAD_DIM // 4, 1) * sin_up
            + pltpu.roll(x, HEAD_DIM // 4, 1) * sin_dn)


def _na_attn_kernel(q_ref, h_ref, wk_ref, wv_ref, ck_ref, cv_ref, tz_ref, cos_ref, sup_ref,
                    sdn_ref, o_ref, krot_ref, vb_ref, ckb_ref, cvb_ref):
    step = pl.program_id(1)

    @pl.when(step == 0)
    def _():
        h = h_ref[...]
        k = jnp.dot(h, wk_ref[...].astype(BF16), preferred_element_type=F32)
        v = jnp.dot(h, wv_ref[...].astype(BF16), preferred_element_type=F32)
        krot_ref[...] = _rope(k, cos_ref[...], sup_ref[...], sdn_ref[...]).astype(BF16)
        vb_ref[...] = v.astype(BF16)
        ckb_ref[...] = ck_ref[...].astype(BF16)
        cvb_ref[...] = cv_ref[...].astype(BF16)

    qrows = pl.ds(pl.multiple_of(step * NA_TQ, NA_TQ), NA_TQ)
    q = _rope(q_ref[...].astype(F32), cos_ref[qrows, :], sup_ref[qrows, :], sdn_ref[qrows, :])
    q = (q * HEAD_DIM ** -0.5).astype(BF16)

    lower = lax.broadcasted_iota(jnp.int32, (GRID_W, HEAD_PAIR), 1) < HEAD_DIM
    upper = jnp.logical_not(lower)
    kwin, vwin, dr0 = [], [], []
    for j in range(NA_ROWS):
        r = step * NA_ROWS + j
        row_start = jnp.clip(r - WIN_H // 2, 0, GRID_H - WIN_H)
        krows = pl.ds(pl.multiple_of(row_start * GRID_W, GRID_W), WIN_H * GRID_W)
        kwin.append(krot_ref[krows, :])
        vwin.append(vb_ref[krows, :])
        dr0.append(row_start - r + WIN_H - 1)

    units = [(j, h) for j in range(NA_ROWS) for h in range(N_HEADS)]
    grp = [slice(h // 2 * HEAD_PAIR, (h // 2 + 1) * HEAD_PAIR) for _, h in units]
    qh = [jnp.where(lower if h % 2 == 0 else upper, q[j * GRID_W:(j + 1) * GRID_W, grp[u]],
                    jnp.zeros((GRID_W, HEAD_PAIR), BF16)) for u, (j, h) in enumerate(units)]
    s_loc = [lax.dot_general(qh[u], kwin[j][:, grp[u]], _NT, preferred_element_type=F32)
             + jnp.concatenate([tz_ref[h, dr0[j] + w] for w in range(0, WIN_H, 2)], axis=1)
             for u, (j, h) in enumerate(units)]
    s_ctx = [jnp.dot(qh[u], ckb_ref[grp[u], :], preferred_element_type=F32)
             for u in range(len(units))]
    m = [jnp.maximum(jnp.max(s_loc[u], axis=-1, keepdims=True),
                     jnp.max(s_ctx[u], axis=-1, keepdims=True)) for u in range(len(units))]
    p_loc = [jnp.exp(s_loc[u] - m[u]) for u in range(len(units))]
    p_ctx = [jnp.exp(s_ctx[u] - m[u]) for u in range(len(units))]
    den = [jnp.sum(p_loc[u], axis=-1, keepdims=True) + jnp.sum(p_ctx[u], axis=-1, keepdims=True)
           for u in range(len(units))]
    o = [(jnp.dot(p_loc[u].astype(BF16), vwin[j][:, grp[u]], preferred_element_type=F32)
          + lax.dot_general(p_ctx[u].astype(BF16), cvb_ref[grp[u], :], _NT,
                            preferred_element_type=F32)) / den[u]
         for u, (j, h) in enumerate(units)]
    for u, (j, h) in enumerate(units):
        if h % 2 == 0:
            o_ref[j * GRID_W:(j + 1) * GRID_W, grp[u]] = jnp.where(lower, o[u], o[u + 1]).astype(BF16)


def _na_attn(l, z, h, w_in, cache_k, cache_v, tz, cos, sin_up, sin_dn):
    seq_blk0 = T_PROMPT // DEC_SEQ
    row_blk0 = T_PROMPT // NA_TQ
    steps = GRID_H // NA_ROWS
    full = pl.BlockSpec((DEC_SEQ, D_NA), lambda b, r: (0, 0))
    return pl.pallas_call(
        _na_attn_kernel,
        name="na_attn",
        grid=(DEC_BATCH, steps),
        in_specs=[
            pl.BlockSpec((NA_TQ, COL_BLK), lambda b, r: (row_blk0 + b * steps + r, CB_Q)),
            pl.BlockSpec((DEC_SEQ, D_MODEL), lambda b, r: (seq_blk0 + b, 0)),
            pl.BlockSpec((None, D_MODEL, COL_BLK), lambda b, r: (l, 0, W_CB_K)),
            pl.BlockSpec((None, D_MODEL, COL_BLK), lambda b, r: (l, 0, W_CB_V)),
            pl.BlockSpec((None, None, D_NA, PAST_LEN), lambda b, r: (b, l, 0, 0)),
            pl.BlockSpec((None, None, D_NA, PAST_LEN), lambda b, r: (b, l, 0, 0)),
            pl.BlockSpec((None, N_HEADS, 2 * WIN_H - 2, GRID_W, 2 * GRID_W),
                         lambda b, r: (l, 0, 0, 0, 0)),
            full, full, full,
        ],
        out_specs=pl.BlockSpec((NA_TQ, D_NA), lambda b, r: (b * steps + r, 0)),
        out_shape=jax.ShapeDtypeStruct((T_SAMPLE, D_NA), BF16),
        scratch_shapes=[pltpu.VMEM((DEC_SEQ, D_NA), BF16), pltpu.VMEM((DEC_SEQ, D_NA), BF16),
                        pltpu.VMEM((D_NA, PAST_LEN), BF16), pltpu.VMEM((D_NA, PAST_LEN), BF16)],
        compiler_params=_cparams(("arbitrary", "arbitrary")),
    )(z, h, w_in, w_in, cache_k, cache_v, tz, cos, sin_up, sin_dn)


def _merge_kernel(x_ref, mod_ref, gc_ref, ga_ref, gg_ref, yc_ref, ug_ref, ap_ref, as_ref,
                  pw_ref, no_ref, go_ref, wo_ref, lg_ref, lb_ref, o_ref,
                  pwb_ref, nob_ref, gob_ref, wob_ref):
    i = pl.program_id(0)

    @pl.when(i == 0)
    def _():
        pwb_ref[...] = pw_ref[...].astype(BF16)
        nob_ref[...] = no_ref[...].astype(BF16)
        gob_ref[...] = go_ref[...].astype(BF16)
        wob_ref[...] = wo_ref[...].astype(BF16)

    att = jnp.where(i < N_TBM_PROMPT, ap_ref[...], as_ref[...])
    br_c = jnp.dot(yc_ref[...], pwb_ref[...], preferred_element_type=F32)
    br_a = jnp.dot(att, nob_ref[...], preferred_element_type=F32)
    br_g = jnp.dot(ug_ref[...], gob_ref[...], preferred_element_type=F32)
    merged = (_sigmoid(gc_ref[...].astype(F32)) * br_c + _sigmoid(ga_ref[...].astype(F32)) * br_a
              + _sigmoid(gg_ref[...].astype(F32)) * br_g)
    mix = jnp.dot(merged.astype(BF16), wob_ref[...], preferred_element_type=F32)
    o_ref[...] = _ln(ALPHA * x_ref[...] + mod_ref[2] * mix, lg_ref[...], lb_ref[...])


def _merge(l, x, mods, z, yc, ug, att_p, att_s, conv_pw, na_out, gm_out, w_o, ln_g, ln_b):
    def gz(k):
        return pl.BlockSpec((TBM, COL_BLK2), lambda i: (i, CB2_GZ + k))

    def w(k, n):
        return pl.BlockSpec((None, k, n), lambda i: (l, 0, 0))

    blk512 = pl.BlockSpec((TBM, COL_BLK), lambda i: (i, 0))
    return pl.pallas_call(
        _merge_kernel,
        name="merge",
        grid=(N_TBM,),
        in_specs=[
            pl.BlockSpec((TBM, D_MODEL), lambda i: (i, 0)),
            pl.BlockSpec((None, 6, None, 1, D_MODEL),
                         lambda i: (l, 0, _mod_row(i, TBM_PER_SAMPLE, N_TBM_PROMPT), 0, 0)),
            gz(0), gz(1), gz(2),
            blk512, blk512,
            pl.BlockSpec((TBM, D_NA), lambda i: (jnp.minimum(i, N_TBM_PROMPT - 1), 0)),
            pl.BlockSpec((TBM, D_NA), lambda i: (jnp.maximum(i - N_TBM_PROMPT, 0), 0)),
            w(D_CONV, D_MODEL), w(D_NA, D_MODEL), w(D_GM, D_MODEL), w(D_MODEL, D_MODEL),
            w(1, D_MODEL), w(1, D_MODEL),
        ],
        out_specs=pl.BlockSpec((TBM, D_MODEL), lambda i: (i, 0)),
        out_shape=jax.ShapeDtypeStruct((T_ALL, D_MODEL), F32),
        scratch_shapes=[pltpu.VMEM((D_CONV, D_MODEL), BF16), pltpu.VMEM((D_NA, D_MODEL), BF16),
                        pltpu.VMEM((D_GM, D_MODEL), BF16), pltpu.VMEM((D_MODEL, D_MODEL), BF16)],
        compiler_params=_cparams(("arbitrary",)),
    )(x, mods, z, z, z, yc, ug, att_p, att_s, conv_pw, na_out, gm_out, w_o, ln_g, ln_b)


def _route(logits):
    lane = lax.broadcasted_iota(jnp.int32, logits.shape, 1)
    big = jnp.int32(ROUTER_LANES)
    is_g = lane < N_EGROUPS
    gl = jnp.where(is_g, logits, -jnp.inf)
    gmax = jnp.max(gl, axis=-1, keepdims=True)
    gidx = jnp.min(jnp.where(gl == gmax, lane, big), axis=-1, keepdims=True)
    gp = 1.0 / jnp.sum(jnp.where(is_g, jnp.exp(gl - gmax), 0.0), axis=-1, keepdims=True)
    lo = N_EGROUPS + gidx * EXP_PER_GROUP
    el = jnp.where(jnp.logical_and(lane >= lo, lane < lo + EXP_PER_GROUP), logits, -jnp.inf)
    v1 = jnp.max(el, axis=-1, keepdims=True)
    i1 = jnp.min(jnp.where(el == v1, lane, big), axis=-1, keepdims=True)
    el2 = jnp.where(lane == i1, -jnp.inf, el)
    v2 = jnp.max(el2, axis=-1, keepdims=True)
    i2 = jnp.min(jnp.where(el2 == v2, lane, big), axis=-1, keepdims=True)
    e2 = jnp.exp(v2 - v1)
    w1 = gp / (1.0 + e2)
    w2 = gp * e2 / (1.0 + e2)
    return jnp.where(lane == i1, w1, 0.0) + jnp.where(lane == i2, w2, 0.0)


def _split_bf16(a):
    hi = a.astype(BF16)
    return hi, (a - hi.astype(F32)).astype(BF16)


def _moe_kernel(last, x_ref, mod_ref, modn_ref, rw_ref, rb_ref, w1_ref, w3_ref, w2_ref, lg_ref,
                lb_ref, out0_ref, out1_ref, t_ref, gate_ref, acc_ref):
    i = pl.program_id(0)
    e = pl.program_id(1)

    @pl.when(e == 0)
    def _():
        t = x_ref[...] * (1.0 + mod_ref[4]) + mod_ref[3]
        t_hi, t_lo = _split_bf16(t)
        t_ref[...] = t_hi
        both = jnp.dot(t_hi, rw_ref[...], preferred_element_type=F32)
        logits = (both[:, :ROUTER_LANES] + both[:, ROUTER_LANES:]
                  + jnp.dot(t_lo, rw_ref[:, :ROUTER_LANES], preferred_element_type=F32))
        gate_ref[...] = _route(logits + rb_ref[...])
        acc_ref[...] = jnp.zeros_like(acc_ref)

    t = t_ref[...]
    gate = gate_ref[...]
    lane = lax.broadcasted_iota(jnp.int32, gate.shape, 1)
    hcol = lax.broadcasted_iota(jnp.int32, (t.shape[0], EXP_SUB * D_EXPERT), 1) // D_EXPERT
    for k0 in range(0, EXP_STEP, EXP_SUB):
        ks = range(k0, k0 + EXP_SUB)
        w1 = jnp.concatenate([w1_ref[k] for k in ks], axis=1).astype(BF16)
        w3 = jnp.concatenate([w3_ref[k] for k in ks], axis=1).astype(BF16)
        w2 = w2_ref[k0:k0 + EXP_SUB].reshape(EXP_SUB * D_EXPERT, D_MODEL).astype(BF16)
        h1 = jnp.dot(t, w1, preferred_element_type=F32)
        h3 = jnp.dot(t, w3, preferred_element_type=F32)
        gmul = jnp.zeros(h1.shape, F32)
        for k in ks:
            gcol = jnp.sum(jnp.where(lane == e * EXP_STEP + k + N_EGROUPS, gate, 0.0),
                           axis=-1, keepdims=True)
            gmul = jnp.where(hcol == k - k0, gcol, gmul)
        hid = (h1 * _sigmoid(h1) * h3 * gmul).astype(BF16)
        acc_ref[...] += jnp.dot(hid, w2, preferred_element_type=F32)

    @pl.when(e == N_EXPERTS // EXP_STEP - 1)
    def _():
        y = _ln(ALPHA * x_ref[...] + mod_ref[5] * acc_ref[...], lg_ref[...], lb_ref[...])
        if last:
            @pl.when(i < T_PROMPT // TM_MOE)
            def _():
                out0_ref[...] = y

            @pl.when(i >= T_PROMPT // TM_MOE)
            def _():
                out1_ref[...] = y
        else:
            out0_ref[...] = y
            out1_ref[...] = (y * (1.0 + modn_ref[1]) + modn_ref[0]).astype(BF16)


def _moe(l, x, mods, router_w, router_b, w1, w3, w2, ln_g, ln_b):
    n_m = T_ALL // TM_MOE
    bps = DEC_SEQ // TM_MOE
    npb = T_PROMPT // TM_MOE
    last = l + 1 == DEPTH
    l_next = l if last else l + 1
    tok_blk = pl.BlockSpec((TM_MOE, D_MODEL), lambda i, e: (i, 0))
    if last:
        out_specs = [pl.BlockSpec((TM_MOE, D_MODEL), lambda i, e: (jnp.minimum(i, npb - 1), 0)),
                     pl.BlockSpec((TM_MOE, D_MODEL), lambda i, e: (jnp.maximum(i - npb, 0), 0))]
        out_shape = [jax.ShapeDtypeStruct((T_PROMPT, D_MODEL), F32),
                     jax.ShapeDtypeStruct((T_SAMPLE, D_MODEL), F32)]
    else:
        out_specs = [tok_blk, tok_blk]
        out_shape = [jax.ShapeDtypeStruct((T_ALL, D_MODEL), F32),
                     jax.ShapeDtypeStruct((T_ALL, D_MODEL), BF16)]
    return pl.pallas_call(
        functools.partial(_moe_kernel, last),
        name="moe",
        grid=(n_m, N_EXPERTS // EXP_STEP),
        in_specs=[
            tok_blk,
            pl.BlockSpec((None, 6, None, 1, D_MODEL),
                         lambda i, e: (l, 0, _mod_row(i, bps, npb), 0, 0)),
            pl.BlockSpec((None, 6, None, 1, D_MODEL),
                         lambda i, e: (l_next, 0, _mod_row(i, bps, npb), 0, 0)),
            pl.BlockSpec((None, D_MODEL, 2 * ROUTER_LANES), lambda i, e: (l, 0, 0)),
            pl.BlockSpec((None, 1, ROUTER_LANES), lambda i, e: (l, 0, 0)),
            pl.BlockSpec((None, EXP_STEP, D_MODEL, D_EXPERT), lambda i, e: (l, e, 0, 0)),
            pl.BlockSpec((None, EXP_STEP, D_MODEL, D_EXPERT), lambda i, e: (l, e, 0, 0)),
            pl.BlockSpec((None, EXP_STEP, D_EXPERT, D_MODEL), lambda i, e: (l, e, 0, 0)),
            pl.BlockSpec((None, 1, D_MODEL), lambda i, e: (l, 0, 0)),
            pl.BlockSpec((None, 1, D_MODEL), lambda i, e: (l, 0, 0)),
        ],
        out_specs=out_specs,
        out_shape=out_shape,
        scratch_shapes=[pltpu.VMEM((TM_MOE, D_MODEL), BF16),
                        pltpu.VMEM((TM_MOE, ROUTER_LANES), F32),
                        pltpu.VMEM((TM_MOE, D_MODEL), F32)],
        compiler_params=_cparams(("arbitrary", "arbitrary")),
    )(x, mods, mods, router_w, router_b, w1, w3, w2, ln_g, ln_b)


def _rope_tables():
    t = np.arange(DEC_SEQ)
    pos = np.stack([t // GRID_W, t % GRID_W], axis=1).astype(np.float32)
    quarter = HEAD_DIM // 4
    d = np.arange(HEAD_DIM)
    axis = d // (HEAD_DIM // 2)
    freq = d % quarter
    upper = (d % (HEAD_DIM // 2)) >= quarter
    inv = jnp.asarray(ROPE_BASE, F32) ** (-jnp.arange(0, HEAD_DIM // 2, 2, dtype=F32) / (HEAD_DIM // 2))
    ang = jnp.asarray(pos)[:, axis] * inv[freq][None, :]
    cos = jnp.cos(ang)
    sin = jnp.sin(ang)
    sin_up = jnp.where(upper[None, :], 0.0, -sin)
    sin_dn = jnp.where(upper[None, :], sin, 0.0)
    tile = lambda a: jnp.tile(a, (1, N_HEADS))
    return tile(cos), tile(sin_up), tile(sin_dn)


def _bias_tables(na_rpb):
    qc = np.arange(GRID_W)[:, None]
    kc = np.arange(GRID_W)[None, :]
    start = np.clip(qc - WIN_W // 2, 0, GRID_W - WIN_W)
    valid = (kc >= start) & (kc < start + WIN_W)
    n_dc = 2 * WIN_W - 1
    dc = kc - qc + WIN_W - 1
    hit = ((dc[None] == np.arange(n_dc)[:, None, None]) & valid[None]).astype(np.float32)
    sel = np.zeros((2 * n_dc + 1, GRID_W, 2 * GRID_W), np.float32)
    sel[:n_dc, :, :GRID_W] = hit
    sel[n_dc:2 * n_dc, :, GRID_W:] = hit
    sel[2 * n_dc] = np.where(np.concatenate([valid, valid], axis=1), 0.0, NEG_INF)
    ones = jnp.ones(na_rpb.shape[:2] + (2 * WIN_H - 2, 1), F32)
    rows = jnp.concatenate([na_rpb[:, :, :-1], na_rpb[:, :, 1:], ones], axis=-1)
    return jnp.einsum('lhrd,dqk->lhrqk', rows, jnp.asarray(sel), precision=HIGHEST)


def kernel(x_prompt, x_sample, cache_na_k, cache_na_v, c, c_ctx, w_ada, b_ada, w_in, conv_dw,
           conv_b, conv_ln_g, conv_ln_b, conv_pw, na_rpb, na_out, gm_ln_g, gm_ln_b, gm_ws, gm_bs,
           gm_out, w_o, ln1_g, ln1_b, rg_w, rg_b, re_w, re_b, moe_w1, moe_w3, moe_w2, ln2_g, ln2_b):
    cond =jnp.zeros((N_COND, D_MODEL), F32).at[0].set(c_ctx).at[1:1 + DEC_BATCH].set(c)
    mods = _ada(cond, w_ada, b_ada)
    mods = mods.reshape(DEPTH, N_COND, 6, 1, D_MODEL).transpose(0, 2, 1, 3, 4)

    gm_ws_b = gm_ws.astype(BF16)
    vec = lambda a: a.reshape(DEPTH, 1, a.shape[-1])
    gm_bs_t = gm_bs.transpose(0, 2, 1)
    router_w = jnp.concatenate(
        [rg_w, re_w.transpose(0, 2, 1, 3).reshape(DEPTH, D_MODEL, N_EXPERTS)], axis=-1)
    router_w = jnp.pad(router_w, ((0, 0), (0, 0), (0, ROUTER_LANES - N_EGROUPS - N_EXPERTS)))
    router_hi = router_w.astype(BF16)
    router_lo = (router_w - router_hi.astype(F32)).astype(BF16)
    router_w = jnp.concatenate([router_hi, router_lo], axis=-1)
    router_b = jnp.concatenate([rg_b, re_b.reshape(DEPTH, N_EXPERTS)], axis=-1)
    router_b = jnp.pad(router_b, ((0, 0), (0, ROUTER_LANES - N_EGROUPS - N_EXPERTS)))
    router_b = router_b.reshape(DEPTH, 1, ROUTER_LANES)
    cache_k = cache_na_k.transpose(0, 1, 3, 4, 2).reshape(DEC_BATCH, DEPTH, D_NA, PAST_LEN)
    cache_v = cache_na_v.transpose(0, 1, 3, 4, 2).reshape(DEC_BATCH, DEPTH, D_NA, PAST_LEN)
    tz = _bias_tables(na_rpb)
    cos, sin_up, sin_dn = _rope_tables()

    kt_all = vt_all = None
    x, h = _gather_modulate(x_prompt, x_sample, mods)
    for l in range(DEPTH):
        z = _inproj(l, h, w_in)
        yc, ug = _branches(l, z, conv_dw, vec(conv_b), vec(conv_ln_g), vec(conv_ln_b),
                           vec(gm_ln_g), vec(gm_ln_b), gm_ws_b, gm_bs_t)
        att_p, kt_all, vt_all = _ctx_attn(l, z, h, w_in, kt_all, vt_all)
        att_s = _na_attn(l, z, h, w_in, cache_k, cache_v, tz, cos, sin_up, sin_dn)
        x = _merge(l, x, mods, z, yc, ug, att_p, att_s, conv_pw, na_out, gm_out, w_o,
                   vec(ln1_g), vec(ln1_b))
        outs = _moe(l, x, mods, router_w, router_b, moe_w1, moe_w3, moe_w2, vec(ln2_g), vec(ln2_b))
        x, h = outs

    y_prompt, y_sample = outs
    return (y_prompt.reshape(BATCH, SEQ, D_MODEL), y_sample.reshape(DEC_BATCH, DEC_SEQ, D_MODEL),
            kt_all.transpose(0, 1, 4, 2, 3), vt_all.transpose(0, 1, 4, 2, 3))
```

```python
import functools

import jax
import jax.numpy as jnp
import numpy as np
from jax import lax
from jax.experimental import pallas as pl
from jax.experimental.pallas import tpu as pltpu

F32 = jnp.float32
BF16 = jnp.bfloat16
HIGHEST = lax.Precision.HIGHEST

D_MODEL = 1024
BATCH = 16
SEQ = 256
DEPTH = 4
DEC_BATCH = 2
DEC_SEQ = 1024
PAST_LEN = 256
GRID_W = 64
GRID_H = DEC_SEQ // GRID_W
D_CONV = 512
CONV_WIDTH = 31
CONV_HALF = CONV_WIDTH // 2
HEAD_DIM = 64
HEAD_PAIR = 2 * HEAD_DIM
N_HEADS = 8
D_NA = 512
WIN_H = 8
WIN_W = 16
ROPE_BASE = 10000.0
D_GM = 512
GM_CHUNK = 128
GM_GROUPS = 4
D_IN = 6656
N_EGROUPS = 4
EXP_PER_GROUP = 8
N_EXPERTS = 32
D_EXPERT = 128
ALPHA = (2 * DEPTH) ** 0.25
LN_EPS = 1e-5
NEG_INF = -1e30

T_PROMPT = BATCH * SEQ
T_SAMPLE = DEC_BATCH * DEC_SEQ
T_ALL = T_PROMPT + T_SAMPLE
N_COND = 8

COL_BLK = 512
W_CB_Q, W_CB_K, W_CB_V = 2, 3, 4
N_CB_KV = 2
N_CB_MAIN = D_IN // COL_BLK - N_CB_KV
N_CB_PRE_Q = 2
CB_Q = N_CB_MAIN - 1
CB2_AB, CB2_GUV, CB2_GZ = 0, 1, 2
COL_BLK2 = 2 * COL_BLK

TB = 256
N_TB = T_ALL // TB
N_TB_PROMPT = T_PROMPT // TB
TB_PER_SAMPLE = DEC_SEQ // TB
TBM = 512
N_TBM = T_ALL // TBM
N_TBM_PROMPT = T_PROMPT // TBM
TBM_PER_SAMPLE = DEC_SEQ // TBM
GATE_SLOTS = 3
HALO = 16
CONV_ROWS = 256
SUBLANES = 8
SHIFT_ROWS = TB + 2 * HALO - SUBLANES

NA_ROWS = 4
NA_TQ = NA_ROWS * GRID_W
TM_IN = 3072
TM_MOE = 1024
EXP_STEP = 4
EXP_SUB = 4
ROUTER_LANES = 128
VMEM_LIMIT = 56 * 1024 * 1024


def _ln(x, g, b):
    mu = jnp.mean(x, axis=-1, keepdims=True)
    xc = x - mu
    var = jnp.mean(xc * xc, axis=-1, keepdims=True)
    return xc * lax.rsqrt(var + LN_EPS) * g + b


def _sigmoid(x):
    return jax.nn.sigmoid(x)


def _gelu(x):
    return jax.nn.gelu(x, approximate=True)


_NT = (((1,), (1,)), ((), ()))


def _cparams(sem):
    return pltpu.CompilerParams(dimension_semantics=sem, vmem_limit_bytes=VMEM_LIMIT)


def _ada_kernel(c_ref, w_ref, b_ref, o_ref):
    c = c_ref[...]
    s = c * _sigmoid(c)
    w = w_ref[...]
    s_hi, w_hi = s.astype(BF16), w.astype(BF16)
    s_lo = (s - s_hi.astype(F32)).astype(BF16)
    w_lo = (w - w_hi.astype(F32)).astype(BF16)
    o_ref[...] = (jnp.dot(s_hi, w_hi, preferred_element_type=F32)
                  + jnp.dot(s_lo, w_hi, preferred_element_type=F32)
                  + jnp.dot(s_hi, w_lo, preferred_element_type=F32)) + b_ref[...]


def _ada(cond, w_ada, b_ada):
    tn = 3072
    return pl.pallas_call(
        _ada_kernel,
        name="ada",
        grid=(DEPTH, 6 * D_MODEL // tn),
        in_specs=[
            pl.BlockSpec((N_COND, D_MODEL), lambda l, j: (0, 0)),
            pl.BlockSpec((None, D_MODEL, tn), lambda l, j: (l, 0, j)),
            pl.BlockSpec((None, 1, tn), lambda l, j: (l, 0, j)),
        ],
        out_specs=pl.BlockSpec((None, N_COND, tn), lambda l, j: (l, 0, j)),
        out_shape=jax.ShapeDtypeStruct((DEPTH, N_COND, 6 * D_MODEL), F32),
        compiler_params=_cparams(("arbitrary", "arbitrary")),
    )(cond, w_ada, b_ada.reshape(DEPTH, 1, 6 * D_MODEL))


def _mod_row(i, blocks_per_sample, n_prompt_blocks):
    return jnp.where(i < n_prompt_blocks, 0, 1 + (i - n_prompt_blocks) // blocks_per_sample)


def _gather_modulate_kernel(xp_ref, xs_ref, mod_ref, x_ref, h_ref):
    x = jnp.where(pl.program_id(0) < T_PROMPT // TM_MOE, xp_ref[...], xs_ref[...])
    x_ref[...] = x
    h_ref[...] = (x * (1.0 + mod_ref[1]) + mod_ref[0]).astype(BF16)


def _gather_modulate(x_prompt, x_sample, mods):
    bps = DEC_SEQ // TM_MOE
    npb = T_PROMPT // TM_MOE
    tok_blk = pl.BlockSpec((TM_MOE, D_MODEL), lambda i: (i, 0))
    return pl.pallas_call(
        _gather_modulate_kernel,
        name="modulate",
        grid=(T_ALL // TM_MOE,),
        in_specs=[
            pl.BlockSpec((TM_MOE, D_MODEL), lambda i: (jnp.minimum(i, npb - 1), 0)),
            pl.BlockSpec((TM_MOE, D_MODEL), lambda i: (jnp.maximum(i - npb, 0), 0)),
            pl.BlockSpec((None, 6, None, 1, D_MODEL),
                         lambda i: (0, 0, _mod_row(i, bps, npb), 0, 0)),
        ],
        out_specs=[tok_blk, tok_blk],
        out_shape=[jax.ShapeDtypeStruct((T_ALL, D_MODEL), F32),
                   jax.ShapeDtypeStruct((T_ALL, D_MODEL), BF16)],
        compiler_params=_cparams(("arbitrary",)),
    )(x_prompt.reshape(T_PROMPT, D_MODEL), x_sample.reshape(T_SAMPLE, D_MODEL), mods)


def _inproj_kernel(h_ref, w_ref, z_ref):
    rows = pl.ds(pl.multiple_of(pl.program_id(1) * TM_IN, TM_IN), TM_IN)
    z = jnp.dot(h_ref[rows, :], w_ref[...].astype(BF16), preferred_element_type=F32)
    z_ref[...] = z.astype(BF16)


def _inproj(l, h, w_in):
    return pl.pallas_call(
        _inproj_kernel,
        name="inproj",
        grid=(N_CB_MAIN, T_ALL // TM_IN),
        in_specs=[
            pl.BlockSpec((T_ALL, D_MODEL), lambda j, i: (0, 0)),
            pl.BlockSpec((None, D_MODEL, COL_BLK),
                         lambda j, i: (l, 0, jnp.where(j < N_CB_PRE_Q, j,
                                                       jnp.where(j < CB_Q, j + 1 + N_CB_KV, W_CB_Q)))),
        ],
        out_specs=pl.BlockSpec((TM_IN, COL_BLK), lambda j, i: (i, j)),
        out_shape=jax.ShapeDtypeStruct((T_ALL, N_CB_MAIN * COL_BLK), BF16),
        compiler_params=_cparams(("arbitrary", "arbitrary")),
    )(h, w_in)


def _branch_kernel(abp_ref, abc_ref, abn_ref, guv_ref,
                   dw_ref, cb_ref, clg_ref, clb_ref, glg_ref, glb_ref, ws_ref, bst_ref,
                   yc_ref, ug_ref, ypad_ref, ysh_ref):
    i = pl.program_id(0)
    j = i - N_TB_PROMPT
    in_sample = i >= N_TB_PROMPT
    has_prev = jnp.logical_and(in_sample, j % TB_PER_SAMPLE != 0)
    has_next = jnp.logical_and(in_sample, j % TB_PER_SAMPLE != TB_PER_SAMPLE - 1)

    def glu(ab_ref):
        return (ab_ref[:, :D_CONV].astype(F32) * _sigmoid(ab_ref[:, D_CONV:].astype(F32)))

    ypad_ref[0:HALO, :] = jnp.where(has_prev, glu(abp_ref), 0.0)
    ypad_ref[HALO:HALO + TB, :] = glu(abc_ref)
    ypad_ref[HALO + TB:HALO + TB + HALO, :] = jnp.where(has_next, glu(abn_ref), 0.0)

    for b in range(SUBLANES):
        ysh_ref[b] = ypad_ref[b:b + SHIFT_ROWS, :]

    off = HALO - CONV_HALF
    for c in range(TB // CONV_ROWS):
        base = c * CONV_ROWS
        acc = jnp.zeros((CONV_ROWS, D_CONV), F32)
        for k in range(CONV_WIDTH):
            tile, phase = divmod(off + k, SUBLANES)
            start = base + tile * SUBLANES
            acc = acc + ysh_ref[phase, start:start + CONV_ROWS, :] * dw_ref[k:k + 1, :]
        y = _ln(acc + cb_ref[...], clg_ref[...], clb_ref[...])
        yc_ref[base:base + CONV_ROWS, :] = (y * _sigmoid(y)).astype(BF16)

    for n in range(TB // GM_CHUNK):
        rows = slice(n * GM_CHUNK, (n + 1) * GM_CHUNK)
        u = _gelu(guv_ref[rows, :D_GM].astype(F32))
        v = _ln(_gelu(guv_ref[rows, D_GM:].astype(F32)), glg_ref[...], glb_ref[...]).astype(BF16)
        for g in range(GM_GROUPS):
            cols = slice(g * GM_CHUNK, (g + 1) * GM_CHUNK)
            sv = jnp.dot(ws_ref[g], v[:, cols], preferred_element_type=F32) + bst_ref[:, g:g + 1]
            ug_ref[rows, cols] = (u[:, cols] * sv).astype(BF16)


def _branches(l, z, conv_dw, conv_b, conv_ln_g, conv_ln_b, gm_ln_g, gm_ln_b, gm_ws, gm_bs_t):
    halo_per_tb = TB // HALO
    n_halo = T_ALL // HALO

    def cur(cb):
        return pl.BlockSpec((TB, COL_BLK2), lambda i: (i, cb))

    def prev(cb):
        return pl.BlockSpec((HALO, COL_BLK2), lambda i: (jnp.maximum(i * halo_per_tb - 1, 0), cb))

    def nxt(cb):
        return pl.BlockSpec((HALO, COL_BLK2),
                            lambda i: (jnp.minimum((i + 1) * halo_per_tb, n_halo - 1), cb))

    def vec(n):
        return pl.BlockSpec((None, 1, n), lambda i: (l, 0, 0))

    return pl.pallas_call(
        _branch_kernel,
        name="branches",
        grid=(N_TB,),
        in_specs=[
            prev(CB2_AB), cur(CB2_AB), nxt(CB2_AB), cur(CB2_GUV),
            pl.BlockSpec((None, CONV_WIDTH, D_CONV), lambda i: (l, 0, 0)),
            vec(D_CONV), vec(D_CONV), vec(D_CONV), vec(D_GM), vec(D_GM),
            pl.BlockSpec((None, GM_GROUPS, GM_CHUNK, GM_CHUNK), lambda i: (l, 0, 0, 0)),
            pl.BlockSpec((None, GM_CHUNK, GM_GROUPS), lambda i: (l, 0, 0)),
        ],
        out_specs=[pl.BlockSpec((TB, D_CONV), lambda i: (i, 0)),
                   pl.BlockSpec((TB, D_GM), lambda i: (i, 0))],
        out_shape=[jax.ShapeDtypeStruct((T_ALL, D_CONV), BF16),
                   jax.ShapeDtypeStruct((T_ALL, D_GM), BF16)],
        scratch_shapes=[pltpu.VMEM((TB + 2 * HALO, D_CONV), F32),
                        pltpu.VMEM((SUBLANES, SHIFT_ROWS, D_CONV), F32)],
        compiler_params=_cparams(("arbitrary",)),
    )(z, z, z, z, conv_dw, conv_b, conv_ln_g, conv_ln_b, gm_ln_g, gm_ln_b, gm_ws, gm_bs_t)


def _ctx_attn_kernel(first, q_ref, h_ref, wk_ref, wv_ref, *rest):
    o_ref, ko_ref, vo_ref, wkt_ref, wvt_ref = rest[-5:]

    @pl.when(pl.program_id(0) == 0)
    def _():
        wkt_ref[...] = wk_ref[...].T.astype(BF16)
        wvt_ref[...] = wv_ref[...].T.astype(BF16)

    h = h_ref[...]
    kt = lax.dot_general(wkt_ref[...], h, _NT, preferred_element_type=F32)
    vt = lax.dot_general(wvt_ref[...], h, _NT, preferred_element_type=F32)
    if first:
        ko_ref[0] = kt.reshape(N_HEADS, HEAD_DIM, SEQ)
        vo_ref[0] = vt.reshape(N_HEADS, HEAD_DIM, SEQ)
        ko_ref[1:] = jnp.zeros((DEPTH - 1, N_HEADS, HEAD_DIM, SEQ), F32)
        vo_ref[1:] = jnp.zeros((DEPTH - 1, N_HEADS, HEAD_DIM, SEQ), F32)
    else:
        ko_ref[...] = kt.reshape(N_HEADS, HEAD_DIM, SEQ)
        vo_ref[...] = vt.reshape(N_HEADS, HEAD_DIM, SEQ)
    q = (q_ref[...].astype(F32) * HEAD_DIM ** -0.5).astype(BF16)
    kb = kt.astype(BF16)
    vb = vt.astype(BF16)
    lower = lax.broadcasted_iota(jnp.int32, (SEQ, HEAD_PAIR), 1) < HEAD_DIM
    upper = jnp.logical_not(lower)
    heads = range(N_HEADS)
    grp = [slice(h // 2 * HEAD_PAIR, (h // 2 + 1) * HEAD_PAIR) for h in heads]
    qh = [jnp.where(lower if h % 2 == 0 else upper, q[:, grp[h]],
                    jnp.zeros((SEQ, HEAD_PAIR), BF16)) for h in heads]
    s = [jnp.dot(qh[h], kb[grp[h], :], preferred_element_type=F32) for h in heads]
    m = [jnp.max(s[h], axis=-1, keepdims=True) for h in heads]
    p = [jnp.exp(s[h] - m[h]) for h in heads]
    den = [jnp.sum(p[h], axis=-1, keepdims=True) for h in heads]
    o = [lax.dot_general(p[h].astype(BF16), vb[grp[h], :], _NT, preferred_element_type=F32) / den[h]
         for h in heads]
    for h in range(0, N_HEADS, 2):
        o_ref[:, grp[h]] = jnp.where(lower, o[h], o[h + 1]).astype(BF16)


def _ctx_attn(l, z, h, w_in, kt_all=None, vt_all=None):
    first = kt_all is None
    if first:
        cache_blk = pl.BlockSpec((None, DEPTH, N_HEADS, HEAD_DIM, SEQ), lambda b: (b, 0, 0, 0, 0))
        carried, carried_specs, aliases = (), [], {}
    else:
        cache_blk = pl.BlockSpec((None, None, N_HEADS, HEAD_DIM, SEQ), lambda b: (b, l, 0, 0, 0))
        carried = (kt_all, vt_all)
        carried_specs = [pl.BlockSpec(memory_space=pl.ANY), pl.BlockSpec(memory_space=pl.ANY)]
        aliases = {4: 1, 5: 2}
    cache_shape = jax.ShapeDtypeStruct((BATCH, DEPTH, N_HEADS, HEAD_DIM, SEQ), F32)
    return pl.pallas_call(
        functools.partial(_ctx_attn_kernel, first),
        name="ctx_attn",
        grid=(BATCH,),
        in_specs=[pl.BlockSpec((SEQ, COL_BLK), lambda b: (b, CB_Q)),
                  pl.BlockSpec((SEQ, D_MODEL), lambda b: (b, 0)),
                  pl.BlockSpec((None, D_MODEL, COL_BLK), lambda b: (l, 0, W_CB_K)),
                  pl.BlockSpec((None, D_MODEL, COL_BLK), lambda b: (l, 0, W_CB_V))] + carried_specs,
        out_specs=[pl.BlockSpec((SEQ, D_NA), lambda b: (b, 0)), cache_blk, cache_blk],
        out_shape=[jax.ShapeDtypeStruct((T_PROMPT, D_NA), BF16), cache_shape, cache_shape],
        scratch_shapes=[pltpu.VMEM((COL_BLK, D_MODEL), BF16), pltpu.VMEM((COL_BLK, D_MODEL), BF16)],
        input_output_aliases=aliases,
        compiler_params=_cparams(("arbitrary",)),
    )(z, h, w_in, w_in, *carried)


def _rope(x, cos, sin_up, sin_dn):
    return (x * cos + pltpu.roll(x, D_NA - HEAD_DIM // 4, 1) * sin_up
            + pltpu.roll(x, HEAD_DIM // 4, 1) * sin_dn)


def _na_attn_kernel(q_ref, h_ref, wk_ref, wv_ref, ck_ref, cv_ref, tz_ref, cos_ref, sup_ref,
                    sdn_ref, o_ref, krot_ref, vb_ref, ckb_ref, cvb_ref):
    step = pl.program_id(1)

    @pl.when(step == 0)
    def _():
        h = h_ref[...]
        k = jnp.dot(h, wk_ref[...].astype(BF16), preferred_element_type=F32)
        v = jnp.dot(h, wv_ref[...].astype(BF16), preferred_element_type=F32)
        krot_ref[...] = _rope(k, cos_ref[...], sup_ref[...], sdn_ref[...]).astype(BF16)
        vb_ref[...] = v.astype(BF16)
        ckb_ref[...] = ck_ref[...].astype(BF16)
        cvb_ref[...] = cv_ref[...].astype(BF16)

    qrows = pl.ds(pl.multiple_of(step * NA_TQ, NA_TQ), NA_TQ)
    q = _rope(q_ref[...].astype(F32), cos_ref[qrows, :], sup_ref[qrows, :], sdn_ref[qrows, :])
    q = (q * HEAD_DIM ** -0.5).astype(BF16)

    lower = lax.broadcasted_iota(jnp.int32, (GRID_W, HEAD_PAIR), 1) < HEAD_DIM
    upper = jnp.logical_not(lower)
    kwin, vwin, dr0 = [], [], []
    for j in range(NA_ROWS):
        r = step * NA_ROWS + j
        row_start = jnp.clip(r - WIN_H // 2, 0, GRID_H - WIN_H)
        krows = pl.ds(pl.multiple_of(row_start * GRID_W, GRID_W), WIN_H * GRID_W)
        kwin.append(krot_ref[krows, :])
        vwin.append(vb_ref[krows, :])
        dr0.append(row_start - r + WIN_H - 1)

    units = [(j, h) for j in range(NA_ROWS) for h in range(N_HEADS)]
    grp = [slice(h // 2 * HEAD_PAIR, (h // 2 + 1) * HEAD_PAIR) for _, h in units]
    qh = [jnp.where(lower if h % 2 == 0 else upper, q[j * GRID_W:(j + 1) * GRID_W, grp[u]],
                    jnp.zeros((GRID_W, HEAD_PAIR), BF16)) for u, (j, h) in enumerate(units)]
    s_loc = [lax.dot_general(qh[u], kwin[j][:, grp[u]], _NT, preferred_element_type=F32)
             + jnp.concatenate([tz_ref[h, dr0[j] + w] for w in range(0, WIN_H, 2)], axis=1)
             for u, (j, h) in enumerate(units)]
    s_ctx = [jnp.dot(qh[u], ckb_ref[grp[u], :], preferred_element_type=F32)
             for u in range(len(units))]
    m = [jnp.maximum(jnp.max(s_loc[u], axis=-1, keepdims=True),
                     jnp.max(s_ctx[u], axis=-1, keepdims=True)) for u in range(len(units))]
    p_loc = [jnp.exp(s_loc[u] - m[u]) for u in range(len(units))]
    p_ctx = [jnp.exp(s_ctx[u] - m[u]) for u in range(len(units))]
    den = [jnp.sum(p_loc[u], axis=-1, keepdims=True) + jnp.sum(p_ctx[u], axis=-1, keepdims=True)
           for u in range(len(units))]
    o = [(jnp.dot(p_loc[u].astype(BF16), vwin[j][:, grp[u]], preferred_element_type=F32)
          + lax.dot_general(p_ctx[u].astype(BF16), cvb_ref[grp[u], :], _NT,
                            preferred_element_type=F32)) / den[u]
         for u, (j, h) in enumerate(units)]
    for u, (j, h) in enumerate(units):
        if h % 2 == 0:
            o_ref[j * GRID_W:(j + 1) * GRID_W, grp[u]] = jnp.where(lower, o[u], o[u + 1]).astype(BF16)


def _na_attn(l, z, h, w_in, cache_k, cache_v, tz, cos, sin_up, sin_dn):
    seq_blk0 = T_PROMPT // DEC_SEQ
    row_blk0 = T_PROMPT // NA_TQ
    steps = GRID_H // NA_ROWS
    full = pl.BlockSpec((DEC_SEQ, D_NA), lambda b, r: (0, 0))
    return pl.pallas_call(
        _na_attn_kernel,
        name="na_attn",
        grid=(DEC_BATCH, steps),
        in_specs=[
            pl.BlockSpec((NA_TQ, COL_BLK), lambda b, r: (row_blk0 + b * steps + r, CB_Q)),
            pl.BlockSpec((DEC_SEQ, D_MODEL), lambda b, r: (seq_blk0 + b, 0)),
            pl.BlockSpec((None, D_MODEL, COL_BLK), lambda b, r: (l, 0, W_CB_K)),
            pl.BlockSpec((None, D_MODEL, COL_BLK), lambda b, r: (l, 0, W_CB_V)),
            pl.BlockSpec((None, None, D_NA, PAST_LEN), lambda b, r: (b, l, 0, 0)),
            pl.BlockSpec((None, None, D_NA, PAST_LEN), lambda b, r: (b, l, 0, 0)),
            pl.BlockSpec((None, N_HEADS, 2 * WIN_H - 2, GRID_W, 2 * GRID_W),
                         lambda b, r: (l, 0, 0, 0, 0)),
            full, full, full,
        ],
        out_specs=pl.BlockSpec((NA_TQ, D_NA), lambda b, r: (b * steps + r, 0)),
        out_shape=jax.ShapeDtypeStruct((T_SAMPLE, D_NA), BF16),
        scratch_shapes=[pltpu.VMEM((DEC_SEQ, D_NA), BF16), pltpu.VMEM((DEC_SEQ, D_NA), BF16),
                        pltpu.VMEM((D_NA, PAST_LEN), BF16), pltpu.VMEM((D_NA, PAST_LEN), BF16)],
        compiler_params=_cparams(("arbitrary", "arbitrary")),
    )(z, h, w_in, w_in, cache_k, cache_v, tz, cos, sin_up, sin_dn)


def _merge_kernel(x_ref, mod_ref, z_ref, yc_ref, ug_ref, ap_ref, as_ref,
                  pw_ref, no_ref, go_ref, wo_ref, lg_ref, lb_ref, o_ref,
                  pwb_ref, nob_ref, gob_ref, wob_ref, gates_ref, gates_sem):
    i = pl.program_id(0)

    def gates_copy(step):
        slot = step % GATE_SLOTS
        rows = pl.ds(pl.multiple_of(step * TBM, TBM), TBM)
        cols = pl.ds(CB2_GZ * COL_BLK2, 3 * D_MODEL)
        return pltpu.make_async_copy(z_ref.at[rows, cols], gates_ref.at[slot], gates_sem.at[slot])

    @pl.when(i == 0)
    def _():
        for step in range(GATE_SLOTS - 1):
            gates_copy(step).start()
        pwb_ref[...] = pw_ref[...].astype(BF16)
        nob_ref[...] = no_ref[...].astype(BF16)
        gob_ref[...] = go_ref[...].astype(BF16)
        wob_ref[...] = wo_ref[...].astype(BF16)

    @pl.when(i + GATE_SLOTS - 1 < N_TBM)
    def _():
        gates_copy(i + GATE_SLOTS - 1).start()

    gates_copy(i).wait()
    gates = gates_ref.at[i % GATE_SLOTS]
    att = jnp.where(i < N_TBM_PROMPT, ap_ref[...], as_ref[...])
    br_c = jnp.dot(yc_ref[...], pwb_ref[...], preferred_element_type=F32)
    br_a = jnp.dot(att, nob_ref[...], preferred_element_type=F32)
    br_g = jnp.dot(ug_ref[...], gob_ref[...], preferred_element_type=F32)
    merged = (_sigmoid(gates[:, 0:D_MODEL].astype(F32)) * br_c
              + _sigmoid(gates[:, D_MODEL:2 * D_MODEL].astype(F32)) * br_a
              + _sigmoid(gates[:, 2 * D_MODEL:3 * D_MODEL].astype(F32)) * br_g)
    mix = jnp.dot(merged.astype(BF16), wob_ref[...], preferred_element_type=F32)
    o_ref[...] = _ln(ALPHA * x_ref[...] + mod_ref[2] * mix, lg_ref[...], lb_ref[...])


def _merge(l, x, mods, z, yc, ug, att_p, att_s, conv_pw, na_out, gm_out, w_o, ln_g, ln_b):
    def w(k, n):
        return pl.BlockSpec((None, k, n), lambda i: (l, 0, 0))

    blk512 = pl.BlockSpec((TBM, COL_BLK), lambda i: (i, 0))
    return pl.pallas_call(
        _merge_kernel,
        name="merge",
        grid=(N_TBM,),
        in_specs=[
            pl.BlockSpec((TBM, D_MODEL), lambda i: (i, 0)),
            pl.BlockSpec((None, 6, None, 1, D_MODEL),
                         lambda i: (l, 0, _mod_row(i, TBM_PER_SAMPLE, N_TBM_PROMPT), 0, 0)),
            pl.BlockSpec(memory_space=pl.ANY),
            blk512, blk512,
            pl.BlockSpec((TBM, D_NA), lambda i: (jnp.minimum(i, N_TBM_PROMPT - 1), 0)),
            pl.BlockSpec((TBM, D_NA), lambda i: (jnp.maximum(i - N_TBM_PROMPT, 0), 0)),
            w(D_CONV, D_MODEL), w(D_NA, D_MODEL), w(D_GM, D_MODEL), w(D_MODEL, D_MODEL),
            w(1, D_MODEL), w(1, D_MODEL),
        ],
        out_specs=pl.BlockSpec((TBM, D_MODEL), lambda i: (i, 0)),
        out_shape=jax.ShapeDtypeStruct((T_ALL, D_MODEL), F32),
        scratch_shapes=[pltpu.VMEM((D_CONV, D_MODEL), BF16), pltpu.VMEM((D_NA, D_MODEL), BF16),
                        pltpu.VMEM((D_GM, D_MODEL), BF16), pltpu.VMEM((D_MODEL, D_MODEL), BF16),
                        pltpu.VMEM((GATE_SLOTS, TBM, 3 * D_MODEL), BF16),
                        pltpu.SemaphoreType.DMA((GATE_SLOTS,))],
        compiler_params=_cparams(("arbitrary",)),
    )(x, mods, z, yc, ug, att_p, att_s, conv_pw, na_out, gm_out, w_o, ln_g, ln_b)


def _route(logits):
    lane = lax.broadcasted_iota(jnp.int32, logits.shape, 1)
    big = jnp.int32(ROUTER_LANES)
    is_g = lane < N_EGROUPS
    gl = jnp.where(is_g, logits, -jnp.inf)
    gmax = jnp.max(gl, axis=-1, keepdims=True)
    gidx = jnp.min(jnp.where(gl == gmax, lane, big), axis=-1, keepdims=True)
    gp = 1.0 / jnp.sum(jnp.where(is_g, jnp.exp(gl - gmax), 0.0), axis=-1, keepdims=True)
    lo = N_EGROUPS + gidx * EXP_PER_GROUP
    el = jnp.where(jnp.logical_and(lane >= lo, lane < lo + EXP_PER_GROUP), logits, -jnp.inf)
    v1 = jnp.max(el, axis=-1, keepdims=True)
    i1 = jnp.min(jnp.where(el == v1, lane, big), axis=-1, keepdims=True)
    el2 = jnp.where(lane == i1, -jnp.inf, el)
    v2 = jnp.max(el2, axis=-1, keepdims=True)
    i2 = jnp.min(jnp.where(el2 == v2, lane, big), axis=-1, keepdims=True)
    e2 = jnp.exp(v2 - v1)
    w1 = gp / (1.0 + e2)
    w2 = gp * e2 / (1.0 + e2)
    return jnp.where(lane == i1, w1, 0.0) + jnp.where(lane == i2, w2, 0.0)


def _split_bf16(a):
    hi = a.astype(BF16)
    return hi, (a - hi.astype(F32)).astype(BF16)


def _moe_kernel(last, x_ref, mod_ref, modn_ref, rw_ref, rb_ref, w1_ref, w3_ref, w2_ref, lg_ref,
                lb_ref, out0_ref, out1_ref, t_ref, gate_ref, acc_ref):
    i = pl.program_id(0)
    e = pl.program_id(1)

    @pl.when(e == 0)
    def _():
        t = x_ref[...] * (1.0 + mod_ref[4]) + mod_ref[3]
        t_hi, t_lo = _split_bf16(t)
        t_ref[...] = t_hi
        both = jnp.dot(t_hi, rw_ref[...], preferred_element_type=F32)
        logits = (both[:, :ROUTER_LANES] + both[:, ROUTER_LANES:]
                  + jnp.dot(t_lo, rw_ref[:, :ROUTER_LANES], preferred_element_type=F32))
        gate_ref[...] = _route(logits + rb_ref[...])
        acc_ref[...] = jnp.zeros_like(acc_ref)

    t = t_ref[...]
    gate = gate_ref[...]
    lane = lax.broadcasted_iota(jnp.int32, gate.shape, 1)
    hcol = lax.broadcasted_iota(jnp.int32, (t.shape[0], EXP_SUB * D_EXPERT), 1) // D_EXPERT
    for k0 in range(0, EXP_STEP, EXP_SUB):
        ks = range(k0, k0 + EXP_SUB)
        w1 = jnp.concatenate([w1_ref[k] for k in ks], axis=1).astype(BF16)
        w3 = jnp.concatenate([w3_ref[k] for k in ks], axis=1).astype(BF16)
        w2 = w2_ref[k0:k0 + EXP_SUB].reshape(EXP_SUB * D_EXPERT, D_MODEL).astype(BF16)
        h1 = jnp.dot(t, w1, preferred_element_type=F32)
        h3 = jnp.dot(t, w3, preferred_element_type=F32)
        gmul = jnp.zeros(h1.shape, F32)
        for k in ks:
            gcol = jnp.sum(jnp.where(lane == e * EXP_STEP + k + N_EGROUPS, gate, 0.0),
                           axis=-1, keepdims=True)
            gmul = jnp.where(hcol == k - k0, gcol, gmul)
        hid = (h1 * _sigmoid(h1) * h3 * gmul).astype(BF16)
        acc_ref[...] += jnp.dot(hid, w2, preferred_element_type=F32)

    @pl.when(e == N_EXPERTS // EXP_STEP - 1)
    def _():
        y = _ln(ALPHA * x_ref[...] + mod_ref[5] * acc_ref[...], lg_ref[...], lb_ref[...])
        if last:
            @pl.when(i < T_PROMPT // TM_MOE)
            def _():
                out0_ref[...] = y

            @pl.when(i >= T_PROMPT // TM_MOE)
            def _():
                out1_ref[...] = y
        else:
            out0_ref[...] = y
            out1_ref[...] = (y * (1.0 + modn_ref[1]) + modn_ref[0]).astype(BF16)


def _moe(l, x, mods, router_w, router_b, w1, w3, w2, ln_g, ln_b):
    n_m = T_ALL // TM_MOE
    bps = DEC_SEQ // TM_MOE
    npb = T_PROMPT // TM_MOE
    last = l + 1 == DEPTH
    l_next = l if last else l + 1
    tok_blk = pl.BlockSpec((TM_MOE, D_MODEL), lambda i, e: (i, 0))
    if last:
        out_specs = [pl.BlockSpec((TM_MOE, D_MODEL), lambda i, e: (jnp.minimum(i, npb - 1), 0)),
                     pl.BlockSpec((TM_MOE, D_MODEL), lambda i, e: (jnp.maximum(i - npb, 0), 0))]
        out_shape = [jax.ShapeDtypeStruct((T_PROMPT, D_MODEL), F32),
                     jax.ShapeDtypeStruct((T_SAMPLE, D_MODEL), F32)]
    else:
        out_specs = [tok_blk, tok_blk]
        out_shape = [jax.ShapeDtypeStruct((T_ALL, D_MODEL), F32),
                     jax.ShapeDtypeStruct((T_ALL, D_MODEL), BF16)]
    return pl.pallas_call(
        functools.partial(_moe_kernel, last),
        name="moe",
        grid=(n_m, N_EXPERTS // EXP_STEP),
        in_specs=[
            tok_blk,
            pl.BlockSpec((None, 6, None, 1, D_MODEL),
                         lambda i, e: (l, 0, _mod_row(i, bps, npb), 0, 0)),
            pl.BlockSpec((None, 6, None, 1, D_MODEL),
                         lambda i, e: (l_next, 0, _mod_row(i, bps, npb), 0, 0)),
            pl.BlockSpec((None, D_MODEL, 2 * ROUTER_LANES), lambda i, e: (l, 0, 0)),
            pl.BlockSpec((None, 1, ROUTER_LANES), lambda i, e: (l, 0, 0)),
            pl.BlockSpec((None, EXP_STEP, D_MODEL, D_EXPERT), lambda i, e: (l, e, 0, 0)),
            pl.BlockSpec((None, EXP_STEP, D_MODEL, D_EXPERT), lambda i, e: (l, e, 0, 0)),
            pl.BlockSpec((None, EXP_STEP, D_EXPERT, D_MODEL), lambda i, e: (l, e, 0, 0)),
            pl.BlockSpec((None, 1, D_MODEL), lambda i, e: (l, 0, 0)),
            pl.BlockSpec((None, 1, D_MODEL), lambda i, e: (l, 0, 0)),
        ],
        out_specs=out_specs,
        out_shape=out_shape,
        scratch_shapes=[pltpu.VMEM((TM_MOE, D_MODEL), BF16),
                        pltpu.VMEM((TM_MOE, ROUTER_LANES), F32),
                        pltpu.VMEM((TM_MOE, D_MODEL), F32)],
        compiler_params=_cparams(("arbitrary", "arbitrary")),
    )(x, mods, mods, router_w, router_b, w1, w3, w2, ln_g, ln_b)


def _rope_tables():
    t = np.arange(DEC_SEQ)
    pos = np.stack([t // GRID_W, t % GRID_W], axis=1).astype(np.float32)
    quarter = HEAD_DIM // 4
    d = np.arange(HEAD_DIM)
    axis = d // (HEAD_DIM // 2)
    freq = d % quarter
    upper = (d % (HEAD_DIM // 2)) >= quarter
    inv = jnp.asarray(ROPE_BASE, F32) ** (-jnp.arange(0, HEAD_DIM // 2, 2, dtype=F32) / (HEAD_DIM // 2))
    ang = jnp.asarray(pos)[:, axis] * inv[freq][None, :]
    cos = jnp.cos(ang)
    sin = jnp.sin(ang)
    sin_up = jnp.where(upper[None, :], 0.0, -sin)
    sin_dn = jnp.where(upper[None, :], sin, 0.0)
    tile = lambda a: jnp.tile(a, (1, N_HEADS))
    return tile(cos), tile(sin_up), tile(sin_dn)


def _bias_tables(na_rpb):
    qc = np.arange(GRID_W)[:, None]
    kc = np.arange(GRID_W)[None, :]
    start = np.clip(qc - WIN_W // 2, 0, GRID_W - WIN_W)
    valid = (kc >= start) & (kc < start + WIN_W)
    n_dc = 2 * WIN_W - 1
    dc = kc - qc + WIN_W - 1
    hit = ((dc[None] == np.arange(n_dc)[:, None, None]) & valid[None]).astype(np.float32)
    sel = np.zeros((2 * n_dc + 1, GRID_W, 2 * GRID_W), np.float32)
    sel[:n_dc, :, :GRID_W] = hit
    sel[n_dc:2 * n_dc, :, GRID_W:] = hit
    sel[2 * n_dc] = np.where(np.concatenate([valid, valid], axis=1), 0.0, NEG_INF)
    ones = jnp.ones(na_rpb.shape[:2] + (2 * WIN_H - 2, 1), F32)
    rows = jnp.concatenate([na_rpb[:, :, :-1], na_rpb[:, :, 1:], ones], axis=-1)
    return jnp.einsum('lhrd,dqk->lhrqk', rows, jnp.asarray(sel), precision=HIGHEST)


def kernel(x_prompt, x_sample, cache_na_k, cache_na_v, c, c_ctx, w_ada, b_ada, w_in, conv_dw,
           conv_b, conv_ln_g, conv_ln_b, conv_pw, na_rpb, na_out, gm_ln_g, gm_ln_b, gm_ws, gm_bs,
           gm_out, w_o, ln1_g, ln1_b, rg_w, rg_b, re_w, re_b, moe_w1, moe_w3, moe_w2, ln2_g, ln2_b):
    cond =jnp.zeros((N_COND, D_MODEL), F32).at[0].set(c_ctx).at[1:1 + DEC_BATCH].set(c)
    mods = _ada(cond, w_ada, b_ada)
    mods = mods.reshape(DEPTH, N_COND, 6, 1, D_MODEL).transpose(0, 2, 1, 3, 4)

    gm_ws_b = gm_ws.astype(BF16)
    vec = lambda a: a.reshape(DEPTH, 1, a.shape[-1])
    gm_bs_t = gm_bs.transpose(0, 2, 1)
    router_w = jnp.concatenate(
        [rg_w, re_w.transpose(0, 2, 1, 3).reshape(DEPTH, D_MODEL, N_EXPERTS)], axis=-1)
    router_w = jnp.pad(router_w, ((0, 0), (0, 0), (0, ROUTER_LANES - N_EGROUPS - N_EXPERTS)))
    router_hi = router_w.astype(BF16)
    router_lo = (router_w - router_hi.astype(F32)).astype(BF16)
    router_w = jnp.concatenate([router_hi, router_lo], axis=-1)
    router_b = jnp.concatenate([rg_b, re_b.reshape(DEPTH, N_EXPERTS)], axis=-1)
    router_b = jnp.pad(router_b, ((0, 0), (0, ROUTER_LANES - N_EGROUPS - N_EXPERTS)))
    router_b = router_b.reshape(DEPTH, 1, ROUTER_LANES)
    cache_k = cache_na_k.transpose(0, 1, 3, 4, 2).reshape(DEC_BATCH, DEPTH, D_NA, PAST_LEN)
    cache_v = cache_na_v.transpose(0, 1, 3, 4, 2).reshape(DEC_BATCH, DEPTH, D_NA, PAST_LEN)
    tz = _bias_tables(na_rpb)
    cos, sin_up, sin_dn = _rope_tables()

    kt_all = vt_all = None
    x, h = _gather_modulate(x_prompt, x_sample, mods)
    for l in range(DEPTH):
        z = _inproj(l, h, w_in)
        yc, ug = _branches(l, z, conv_dw, vec(conv_b), vec(conv_ln_g), vec(conv_ln_b),
                           vec(gm_ln_g), vec(gm_ln_b), gm_ws_b, gm_bs_t)
        att_p, kt_all, vt_all = _ctx_attn(l, z, h, w_in, kt_all, vt_all)
        att_s = _na_attn(l, z, h, w_in, cache_k, cache_v, tz, cos, sin_up, sin_dn)
        x = _merge(l, x, mods, z, yc, ug, att_p, att_s, conv_pw, na_out, gm_out, w_o,
                   vec(ln1_g), vec(ln1_b))
        outs = _moe(l, x, mods, router_w, router_b, moe_w1, moe_w3, moe_w2, vec(ln2_g), vec(ln2_b))
        x, h = outs

    y_prompt, y_sample = outs
    return (y_prompt.reshape(BATCH, SEQ, D_MODEL), y_sample.reshape(DEC_BATCH, DEC_SEQ, D_MODEL),
            kt_all.transpose(0, 1, 4, 2, 3), vt_all.transpose(0, 1, 4, 2, 3))
```
